```python
import jax, jax.numpy as jnp
from jax import lax
import numpy as np

D_MODEL = 1024
BATCH = 8
SEQ = 2048
DEPTH = 2

GRID_W = 64
CTX_LEN = 256
N_EVEN = (DEPTH + 1) // 2
N_ODD = DEPTH // 2
NORM_EPS = 1e-6
F32 = jnp.float32

GLA_HEADS = 4
GLA_DK = 64
GLA_DV = 128
GLA_LOWRANK = 16
GLA_TAU = 16.0
GLA_CHUNK = 64

SWA_HEADS = 8
SWA_KV_HEADS = 2
SWA_HEAD_DIM = 64
WINDOW = 128
SWA_BLOCK = 128
ROPE_BASE = 10000.0

PROJ_SPLITS = (GLA_HEADS * GLA_DK, GLA_HEADS * GLA_DK, GLA_HEADS * GLA_DV, GLA_HEADS * GLA_DV,
               2 * GLA_LOWRANK, SWA_HEADS * SWA_HEAD_DIM, SWA_KV_HEADS * SWA_HEAD_DIM, SWA_KV_HEADS * SWA_HEAD_DIM)
PROJ_DIM = 2 * GLA_HEADS * GLA_DK + 2 * GLA_HEADS * GLA_DV + 2 * GLA_LOWRANK + (SWA_HEADS + 2 * SWA_KV_HEADS) * SWA_HEAD_DIM
MIX_DIM = GLA_HEADS * GLA_DV + SWA_HEADS * SWA_HEAD_DIM

RW_HEAD_DIM = 64
RW_HEADS = D_MODEL // RW_HEAD_DIM
RW_DECAY_LORA = 64
RW_AAA_LORA = 64
RW_GATE_LORA = 160
RW_GN_EPS = 64e-5

PEER_HEADS = 8
PEER_NKEYS = 128
PEER_EXPERTS = PEER_NKEYS * PEER_NKEYS
PEER_KEY_DIM = 256
PEER_TOPK = 16
PEER_BLOCK = 128

kernel_name = 'hybrid_gla_swa_rwkv7_peer_dit'


def flip(t):
    return t[:, ::-1]


def ident(t):
    return t


def rms_norm(x, g):
    xf = x.astype(F32)
    y = xf * lax.rsqrt(jnp.mean(xf * xf, axis=-1, keepdims=True) + NORM_EPS)
    return (y * g.astype(F32)).astype(x.dtype)


def modulate(x, g, shift, scale):
    return rms_norm(x, g) * (1 + scale) + shift


def axial_rope_tables(T):
    rows = T // GRID_W
    row = jnp.broadcast_to(jnp.arange(rows, dtype=F32)[:, None], (rows, GRID_W)).reshape(-1)
    col = jnp.broadcast_to(jnp.arange(GRID_W, dtype=F32)[None, :], (rows, GRID_W)).reshape(-1)
    n_freq = SWA_HEAD_DIM // 4
    inv_freq = ROPE_BASE ** (-jnp.arange(n_freq, dtype=F32) / n_freq)
    ang = jnp.stack([row[:, None] * inv_freq, col[:, None] * inv_freq], axis=1)
    return jnp.cos(ang), jnp.sin(ang)


def apply_axial_rope(x, cos, sin):
    B, T, H, dh = x.shape
    xr = x.astype(F32).reshape(B, T, H, 2, 2, dh // 4)
    a, b = xr[..., 0, :], xr[..., 1, :]
    cs, sn = cos[None, :, None], sin[None, :, None]
    out = jnp.stack([a * cs - b * sn, b * cs + a * sn], axis=-2)
    return out.reshape(B, T, H, dh).astype(x.dtype)


def gla_prepare(q, k, v, lr, dec_w2, dec_b):
    B, T = q.shape[:2]
    lr = lr.reshape(B, T, 2, GLA_LOWRANK).astype(F32)
    log_a = jax.nn.log_sigmoid(jnp.einsum('btzr,zre->btze', lr, dec_w2.astype(F32)) + dec_b.astype(F32)) / GLA_TAU
    log_a = log_a.reshape(B, T, 2, GLA_HEADS, GLA_DK)
    q = q.reshape(B, T, GLA_HEADS, GLA_DK) * GLA_DK ** -0.5
    return (q, k.reshape(B, T, GLA_HEADS, GLA_DK), v.reshape(B, T, GLA_HEADS, GLA_DV),
            log_a[:, :, 0], log_a[:, :, 1])


def gla_chunked(q, k, v, log_a, s0, exclusive):
    B, T, H, dk = q.shape
    dv = v.shape[-1]
    n = T // GLA_CHUNK

    def blk(t):
        return t.astype(F32).reshape(B, n, GLA_CHUNK, H, -1).transpose(1, 0, 3, 2, 4)

    q, k, v, la = blk(q), blk(k), blk(v), blk(log_a)
    b = jnp.cumsum(la, axis=3)
    total = b[:, :, :, -1]
    q_in = q * jnp.exp(b)
    k_in = k * jnp.exp(-b)
    k_out = k * jnp.exp(total[:, :, :, None] - b)
    idx = jnp.arange(GLA_CHUNK)
    mask = (idx[:, None] > idx[None, :]) if exclusive else (idx[:, None] >= idx[None, :])
    att = jnp.where(mask, jnp.einsum('nbhid,nbhjd->nbhij', q_in, k_in), 0.0)
    o_intra = jnp.einsum('nbhij,nbhjv->nbhiv', att, v)
    d_state = jnp.einsum('nbhjd,nbhjv->nbhdv', k_out, v)

    def step(S, inp):
        dS, dec = inp
        return S * dec[..., None] + dS, S

    s_fin, s_prev = lax.scan(step, s0.astype(F32), (d_state, jnp.exp(total)))
    o = o_intra + jnp.einsum('nbhid,nbhdv->nbhiv', q_in, s_prev)
    return o.transpose(1, 0, 3, 2, 4).reshape(B, T, H, dv), s_fin


def gla_final_state(k, v, log_a):
    b = jnp.cumsum(log_a.astype(F32), axis=1)
    k_out = k.astype(F32) * jnp.exp(b[:, -1:] - b)
    return jnp.einsum('bthd,bthv->bhdv', k_out, v.astype(F32))


def gla_bidirectional(lat, ctx, ctx_out):
    ql, kl, vl, lfl, lbl = lat
    qc, kc, vc, lfc, lbc = ctx
    zeros = jnp.zeros((ql.shape[0], GLA_HEADS, GLA_DK, GLA_DV), F32)
    if ctx_out:
        oc_f, sc_f = gla_chunked(qc, kc, vc, lfc, zeros, False)
        oc_b, sc_b = gla_chunked(flip(qc), flip(kc), flip(vc), flip(lbc), zeros, True)
        o_ctx = oc_f + flip(oc_b)
    else:
        sc_f = gla_final_state(kc, vc, lfc)
        sc_b = gla_final_state(flip(kc), flip(vc), flip(lbc))
        o_ctx = None
    ol_f, _ = gla_chunked(ql, kl, vl, lfl, sc_f, False)
    ol_b, _ = gla_chunked(flip(ql), flip(kl), flip(vl), flip(lbl), sc_b, True)
    return ol_f + flip(ol_b), o_ctx


def gla_output(o, gate, g):
    B, T = o.shape[:2]
    on = o * lax.rsqrt(jnp.mean(o * o, axis=-1, keepdims=True) + NORM_EPS)
    return (on.reshape(B, T, -1) * g.astype(F32) * jax.nn.silu(gate.astype(F32))).astype(gate.dtype)


def swa_latent(q, k, v, kc, vc, sink):
    B, T, Hq, dh = q.shape
    G = k.shape[2]
    R = Hq // G
    W = SWA_BLOCK
    n = T // W
    L = kc.shape[1]
    scale = dh ** -0.5
    qb = q.reshape(B, n, W, G, R, dh)

    def band(t):
        tp = jnp.pad(t, ((0, 0), (W, W), (0, 0), (0, 0))).reshape(B, n + 2, W, G, dh)
        return jnp.concatenate([tp[:, :-2], tp[:, 1:-1], tp[:, 2:]], axis=2)

    kb, vb = band(k), band(v)
    s_loc = jnp.einsum('bnigrd,bnjgd->bngrij', qb, kb).astype(F32) * scale
    qpos = jnp.arange(n)[:, None, None] * W + jnp.arange(W)[None, :, None]
    kpos = (jnp.arange(n)[:, None, None] - 1) * W + jnp.arange(3 * W)[None, None, :]
    valid = (jnp.abs(qpos - kpos) <= WINDOW) & (kpos >= 0) & (kpos < T)
    s_loc = jnp.where(valid[None, :, None, None], s_loc, -jnp.inf)
    s_ctx = jnp.einsum('bnigrd,bjgd->bngrij', qb, kc).astype(F32) * scale
    s_sink = jnp.broadcast_to(sink.astype(F32).reshape(G, R)[None, None, :, :, None, None], (B, n, G, R, W, 1))
    p = jax.nn.softmax(jnp.concatenate([s_loc, s_ctx, s_sink], axis=-1), axis=-1)
    o = (jnp.einsum('bngrij,bnjgd->bnigrd', p[..., :3 * W].astype(v.dtype), vb)
         + jnp.einsum('bngrij,bjgd->bnigrd', p[..., 3 * W:3 * W + L].astype(v.dtype), vc))
    return o.reshape(B, T, Hq * dh)


def ctx_attention(qc, kc, vc, sink):
    B, L, Hq, dh = qc.shape
    G = kc.shape[2]
    R = Hq // G
    s = jnp.einsum('bigrd,bjgd->bgrij', qc.reshape(B, L, G, R, dh), kc).astype(F32) * dh ** -0.5
    s_sink = jnp.broadcast_to(sink.astype(F32).reshape(G, R)[None, :, :, None, None], (B, G, R, L, 1))
    p = jax.nn.softmax(jnp.concatenate([s, s_sink], axis=-1), axis=-1)
    o = jnp.einsum('bgrij,bjgd->bigrd', p[..., :L].astype(vc.dtype), vc)
    return o.reshape(B, L, Hq * dh)


def even_mixer(u_lat, u_ctx, w_in, dec_w2, dec_b, gla_g, sink, w_out, cos, sin, ctx_out):
    B, T, _ = u_lat.shape
    L = u_ctx.shape[1]
    cuts = np.cumsum(PROJ_SPLITS)[:-1].tolist()
    gq, gk, gv, gg, glr, sq, sk, sv = jnp.split(u_lat @ w_in, cuts, axis=-1)
    cq, ck, cv, cg, clr, csq, csk, csv = jnp.split(u_ctx @ w_in, cuts, axis=-1)
    o_lat, o_ctx = gla_bidirectional(gla_prepare(gq, gk, gv, glr, dec_w2, dec_b),
                                     gla_prepare(cq, ck, cv, clr, dec_w2, dec_b), ctx_out)
    q = apply_axial_rope(sq.reshape(B, T, SWA_HEADS, SWA_HEAD_DIM), cos, sin)
    k = apply_axial_rope(sk.reshape(B, T, SWA_KV_HEADS, SWA_HEAD_DIM), cos, sin)
    v = sv.reshape(B, T, SWA_KV_HEADS, SWA_HEAD_DIM)
    kc = csk.reshape(B, L, SWA_KV_HEADS, SWA_HEAD_DIM)
    vc = csv.reshape(B, L, SWA_KV_HEADS, SWA_HEAD_DIM)
    a_lat = swa_latent(q, k, v, kc, vc, sink)
    out_lat = jnp.concatenate([gla_output(o_lat, gg, gla_g), a_lat], axis=-1) @ w_out
    if not ctx_out:
        return out_lat, None
    a_ctx = ctx_attention(csq.reshape(B, L, SWA_HEADS, SWA_HEAD_DIM), kc, vc, sink)
    out_ctx = jnp.concatenate([gla_output(o_ctx, cg, gla_g), a_ctx], axis=-1) @ w_out
    return out_lat, out_ctx


def token_shift(x):
    prev = jnp.pad(x[:, :-1], ((0, 0), (1, 0), (0, 0)))
    nxt = jnp.pad(x[:, 1:], ((0, 0), (0, 1), (0, 0)))
    return 0.5 * (prev + nxt)


def rwkv_prepare(h, mu, w_rkv, w0, w1, w2, a0, a1, a2, g1, g2, k_k, k_a, full):
    B, T, D = h.shape
    xx = token_shift(h) - h
    xs = h[None] + xx[None] * mu[:, None, None, :]
    k = xs[2] @ w_rkv[1]
    v = xs[3] @ w_rkv[2]
    lw = w0 + jnp.einsum('btzr,zrd->btzd', jnp.tanh(jnp.einsum('btd,zdr->btzr', xs[1], w1)), w2)
    decay = jnp.exp(-jnp.exp(-jax.nn.softplus(-lw.astype(F32)) - 0.5))
    a = jax.nn.sigmoid((a0 + jnp.einsum('btzr,zrd->btzd', jnp.einsum('btd,zdr->btzr', xs[4], a1), a2)).astype(F32))
    kk = (k * k_k).astype(F32).reshape(B, T, RW_HEADS, RW_HEAD_DIM)
    kk = kk * lax.rsqrt(jnp.maximum(jnp.sum(kk * kk, axis=-1, keepdims=True), 1e-24))
    k_dir = k.astype(F32)[:, :, None] * (1 + (a - 1) * k_a.astype(F32))

    def heads(t):
        return t.reshape(t.shape[:-1] + (RW_HEADS, RW_HEAD_DIM))

    r = heads(xs[0] @ w_rkv[0]) if full else None
    g = (jax.nn.sigmoid(xs[5] @ g1) @ g2) if full else None
    return r, heads(decay), heads(k_dir), heads(v), kk, heads(a), g


def rwkv7_scan(r, decay, k, v, kk, a, s0):
    emit = r is not None
    seq = [t.astype(F32).transpose(1, 0, 2, 3) for t in (decay, k, v, kk, a)]
    if emit:
        seq.append(r.astype(F32).transpose(1, 0, 2, 3))

    def step(S, inp):
        w_t, k_t, v_t, kk_t, a_t = inp[:5]
        S = (S * w_t[:, :, None, :]
             - jnp.einsum('bhvk,bhk->bhv', S, kk_t)[..., None] * (kk_t * a_t)[:, :, None, :]
             + v_t[..., None] * k_t[:, :, None, :])
        y = jnp.einsum('bhvk,bhk->bhv', S, inp[5]) if emit else None
        return S, y

    s_fin, ys = lax.scan(step, s0, tuple(seq))
    return (ys.transpose(1, 0, 2, 3) if emit else None), s_fin


def rwkv_output(y, r, k_dir, v, g, r_k, ln_g, ln_b, w_o):
    B, T = y.shape[:2]
    mean = jnp.mean(y, axis=-1, keepdims=True)
    var = jnp.mean((y - mean) ** 2, axis=-1, keepdims=True)
    yn = ((y - mean) * lax.rsqrt(var + RW_GN_EPS)).reshape(B, T, -1) * ln_g.astype(F32) + ln_b.astype(F32)
    bonus = jnp.einsum('bthn,btzhn,hn->bth', r.astype(F32), k_dir, r_k.astype(F32))[..., None] * v.astype(F32)
    out = (yn + bonus.reshape(B, T, -1)) * g.astype(F32)
    return out.astype(g.dtype) @ w_o


def odd_mixer(u_lat, u_ctx, mu, w_rkv, w_o, w0, w1, w2, a0, a1, a2, g1, g2, k_k, k_a, r_k, ln_g, ln_b, ctx_out):
    p = (mu, w_rkv, w0, w1, w2, a0, a1, a2, g1, g2, k_k, k_a)
    r_l, dec_l, kd_l, v_l, kk_l, a_l, g_l = rwkv_prepare(u_lat, *p, True)
    r_c, dec_c, kd_c, v_c, kk_c, a_c, g_c = rwkv_prepare(u_ctx, *p, ctx_out)
    zeros = jnp.zeros((u_lat.shape[0], RW_HEADS, RW_HEAD_DIM, RW_HEAD_DIM), F32)
    ys_lat, ys_ctx = [], []
    for z in range(2):
        d = flip if z == 1 else ident
        yc, sc = rwkv7_scan(d(r_c) if ctx_out else None, d(dec_c[:, :, z]), d(kd_c[:, :, z]), d(v_c),
                            d(kk_c), d(a_c[:, :, z]), zeros)
        yl, _ = rwkv7_scan(d(r_l), d(dec_l[:, :, z]), d(kd_l[:, :, z]), d(v_l), d(kk_l), d(a_l[:, :, z]), sc)
        ys_lat.append(d(yl))
        if ctx_out:
            ys_ctx.append(d(yc))
    out_lat = rwkv_output(ys_lat[0] + ys_lat[1], r_l, kd_l, v_l, g_l, r_k, ln_g, ln_b, w_o)
    if not ctx_out:
        return out_lat, None
    out_ctx = rwkv_output(ys_ctx[0] + ys_ctx[1], r_c, kd_c, v_c, g_c, r_k, ln_g, ln_b, w_o)
    return out_lat, out_ctx


def peer_ffn(t, w_q, k1, k2, u, v):
    M, D = t.shape
    half = PEER_KEY_DIM // 2
    q = (t @ w_q).astype(F32).reshape(M, PEER_HEADS, 2, half)
    s1 = jnp.einsum('mhd,hnd->mhn', q[:, :, 0], k1.astype(F32))
    s2 = jnp.einsum('mhd,hnd->mhn', q[:, :, 1], k2.astype(F32))
    v1, i1 = lax.top_k(s1, PEER_TOPK)
    v2, i2 = lax.top_k(s2, PEER_TOPK)
    cand = (v1[..., :, None] + v2[..., None, :]).reshape(M, PEER_HEADS, PEER_TOPK * PEER_TOPK)
    cand_idx = (i1[..., :, None] * PEER_NKEYS + i2[..., None, :]).reshape(M, PEER_HEADS, PEER_TOPK * PEER_TOPK)
    top_s, pos = lax.top_k(cand, PEER_TOPK)
    idx = jnp.take_along_axis(cand_idx, pos, axis=-1)
    gate = jax.nn.softmax(top_s, axis=-1)
    nb = M // PEER_BLOCK

    def block(args):
        tb, ib, gb = args
        ue = u[ib]
        ve = v[ib]
        act = jax.nn.gelu(jnp.einsum('phkd,pd->phk', ue, tb).astype(F32), approximate=False)
        return jnp.einsum('phk,phkd->pd', (gb * act).astype(ve.dtype), ve)

    out = lax.map(block, (t.reshape(nb, PEER_BLOCK, D), idx.reshape(nb, PEER_BLOCK, PEER_HEADS, PEER_TOPK),
                          gate.reshape(nb, PEER_BLOCK, PEER_HEADS, PEER_TOPK)))
    return out.reshape(M, D)


def setup_inputs(seed: int = 0) -> dict:
    key = jax.random.key(seed)
    keys = iter(jax.random.split(key, 48))
    D = D_MODEL

    def nrm(shape, scale):
        return jax.random.normal(next(keys), shape, jnp.float32) * scale

    def gain(shape):
        return 1.0 + nrm(shape, 0.02)

    return {
        'x': nrm((BATCH, SEQ, D), 1.0),
        'c': nrm((BATCH, D), 1.0),
        'ctx': nrm((BATCH, CTX_LEN, D), 1.0),
        'c_ctx': nrm((D,), 1.0),
        'ada_w': nrm((DEPTH, D, 6 * D), 0.5 * D ** -0.5),
        'ada_b': nrm((DEPTH, 6 * D), 0.02),
        'norm1_g': gain((DEPTH, D)),
        'norm2_g': gain((DEPTH, D)),
        'ab_w_in': nrm((N_EVEN, D, PROJ_DIM), D ** -0.5),
        'gla_dec_w2': nrm((N_EVEN, 2, GLA_LOWRANK, GLA_HEADS * GLA_DK), GLA_LOWRANK ** -0.5),
        'gla_dec_b': nrm((N_EVEN, 2, GLA_HEADS * GLA_DK), 0.1),
        'gla_norm_g': gain((N_EVEN, GLA_HEADS * GLA_DV)),
        'swa_sink': nrm((N_EVEN, SWA_HEADS), 0.5),
        'ab_w_out': nrm((N_EVEN, MIX_DIM, D), MIX_DIM ** -0.5),
        'rw_mu': jax.random.uniform(next(keys), (N_ODD, 6, D), jnp.float32),
        'rw_w_rkv': nrm((N_ODD, 3, D, D), D ** -0.5),
        'rw_w_o': nrm((N_ODD, D, D), D ** -0.5),
        'rw_w0': nrm((N_ODD, 2, D), 1.0) - 1.0,
        'rw_w1': nrm((N_ODD, 2, D, RW_DECAY_LORA), 0.5 * D ** -0.5),
        'rw_w2': nrm((N_ODD, 2, RW_DECAY_LORA, D), 0.5 * RW_DECAY_LORA ** -0.5),
        'rw_a0': nrm((N_ODD, 2, D), 0.1),
        'rw_a1': nrm((N_ODD, 2, D, RW_AAA_LORA), 0.5 * D ** -0.5),
        'rw_a2': nrm((N_ODD, 2, RW_AAA_LORA, D), 0.5 * RW_AAA_LORA ** -0.5),
        'rw_g1': nrm((N_ODD, D, RW_GATE_LORA), D ** -0.5),
        'rw_g2': nrm((N_ODD, RW_GATE_LORA, D), RW_GATE_LORA ** -0.5),
        'rw_k_k': 0.85 + nrm((N_ODD, D), 0.02),
        'rw_k_a': gain((N_ODD, D)),
        'rw_r_k': nrm((N_ODD, RW_HEADS, RW_HEAD_DIM), 0.1),
        'rw_ln_g': gain((N_ODD, D)),
        'rw_ln_b': nrm((N_ODD, D), 0.02),
        'peer_w_q': nrm((DEPTH, D, PEER_HEADS * PEER_KEY_DIM), D ** -0.5),
        'peer_k1': nrm((DEPTH, PEER_HEADS, PEER_NKEYS, PEER_KEY_DIM // 2), (PEER_KEY_DIM // 2) ** -0.5),
        'peer_k2': nrm((DEPTH, PEER_HEADS, PEER_NKEYS, PEER_KEY_DIM // 2), (PEER_KEY_DIM // 2) ** -0.5),
        'peer_u': nrm((DEPTH, PEER_EXPERTS, D), D ** -0.5),
        'peer_v': nrm((DEPTH, PEER_EXPERTS, D), 1.0),
        'final_g': gain((D,)),
    }


def reference(x, c, ctx, c_ctx, ada_w, ada_b, norm1_g, norm2_g,
              ab_w_in, gla_dec_w2, gla_dec_b, gla_norm_g, swa_sink, ab_w_out,
              rw_mu, rw_w_rkv, rw_w_o, rw_w0, rw_w1, rw_w2, rw_a0, rw_a1, rw_a2,
              rw_g1, rw_g2, rw_k_k, rw_k_a, rw_r_k, rw_ln_g, rw_ln_b,
              peer_w_q, peer_k1, peer_k2, peer_u, peer_v, final_g):
    B, T, D = x.shape
    cos, sin = axial_rope_tables(T)
    s_lat = jax.nn.silu(c)
    s_ctx = jax.nn.silu(c_ctx)
    for i in range(DEPTH):
        last = i == DEPTH - 1
        j = i // 2
        mod_l = jnp.split((s_lat @ ada_w[i] + ada_b[i])[:, None, :], 6, axis=-1)
        mod_c = jnp.split(s_ctx @ ada_w[i] + ada_b[i], 6, axis=-1)
        u_lat = modulate(x, norm1_g[i], mod_l[0], mod_l[1])
        u_ctx = modulate(ctx, norm1_g[i], mod_c[0], mod_c[1])
        if i % 2 == 0:
            m_lat, m_ctx = even_mixer(u_lat, u_ctx, ab_w_in[j], gla_dec_w2[j], gla_dec_b[j], gla_norm_g[j],
                                      swa_sink[j], ab_w_out[j], cos, sin, not last)
        else:
            m_lat, m_ctx = odd_mixer(u_lat, u_ctx, rw_mu[j], rw_w_rkv[j], rw_w_o[j], rw_w0[j], rw_w1[j], rw_w2[j],
                                     rw_a0[j], rw_a1[j], rw_a2[j], rw_g1[j], rw_g2[j], rw_k_k[j], rw_k_a[j],
                                     rw_r_k[j], rw_ln_g[j], rw_ln_b[j], not last)
        x = x + mod_l[2] * m_lat
        f_lat = modulate(x, norm2_g[i], mod_l[3], mod_l[4])
        if last:
            f = peer_ffn(f_lat.reshape(-1, D), peer_w_q[i], peer_k1[i], peer_k2[i], peer_u[i], peer_v[i])
            x = x + mod_l[5] * f.reshape(B, T, D)
        else:
            ctx = ctx + mod_c[2] * m_ctx
            f_ctx = modulate(ctx, norm2_g[i], mod_c[3], mod_c[4])
            f = peer_ffn(jnp.concatenate([f_lat.reshape(-1, D), f_ctx.reshape(-1, D)], axis=0),
                         peer_w_q[i], peer_k1[i], peer_k2[i], peer_u[i], peer_v[i])
            x = x + mod_l[5] * f[:B * T].reshape(B, T, D)
            ctx = ctx + mod_c[5] * f[B * T:].reshape(B, -1, D)
    return rms_norm(x, final_g)
```

```python
import functools

import numpy as np
import jax
import jax.numpy as jnp
from jax import lax
from jax.experimental import pallas as pl
from jax.experimental.pallas import tpu as pltpu

F32 = jnp.float32
BF16 = jnp.bfloat16
HI = lax.Precision.HIGHEST

NORM_EPS = 1e-6
GLA_HEADS, GLA_DK, GLA_DV, GLA_LOWRANK, GLA_TAU, GLA_CHUNK = 4, 64, 128, 16, 16.0, 64
SWA_HEADS, SWA_KV_HEADS, SWA_HEAD_DIM, WINDOW = 8, 2, 64, 128
ROPE_BASE = 10000.0
GRID_W = 64
RW_HEAD_DIM = 64
RW_GN_EPS = 64e-5
PEER_HEADS, PEER_NKEYS, PEER_TOPK = 8, 128, 16
NEG = -1e30

TM = 256
VMEM_LIMIT = 56 * 1024 * 1024


def _cparams(sem, vmem=None):
    return pltpu.CompilerParams(dimension_semantics=sem, vmem_limit_bytes=vmem)


def _bdot(a, b):
    return jnp.dot(a.astype(BF16), b.astype(BF16), preferred_element_type=F32)


def _hdot(a, b):
    return jnp.dot(a, b, precision=HI, preferred_element_type=F32)


def _sigmoid(x):
    return 1.0 / (1.0 + jnp.exp(-x))


def _rms(x):
    return x * lax.rsqrt(jnp.mean(x * x, axis=-1, keepdims=True) + NORM_EPS)


def _mod_rows(mod_ref, k, d):
    return mod_ref[0, :, k * d:(k + 1) * d]


def _mod_spec(mod, nb, nbatch):
    return pl.BlockSpec((1, 1, mod.shape[2]), lambda i: (jnp.where(i % nb == 0, nbatch, i // nb), 0, 0))


def _full(shape):
    n = len(shape)
    return pl.BlockSpec(shape, lambda *_: (0,) * n)


def _ada_kernel(c_ref, w_ref, b_ref, o_ref):
    c = c_ref[...]
    s = c * _sigmoid(c)
    o_ref[...] = _hdot(s, w_ref[...]) + b_ref[...]


def _ada_table(cc, w, b):
    rows, d = cc.shape
    n = w.shape[1]
    tn = 512
    return pl.pallas_call(
        _ada_kernel,
        grid=(n // tn,),
        in_specs=[_full((rows, d)), pl.BlockSpec((d, tn), lambda j: (0, j)),
                  pl.BlockSpec((1, tn), lambda j: (0, j))],
        out_specs=pl.BlockSpec((rows, tn), lambda j: (0, j)),
        out_shape=jax.ShapeDtypeStruct((rows, n), F32),
        compiler_params=_cparams(("arbitrary",)),
        name="ada_table",
    )(cc, w, b.reshape(1, n))


_P_GQ, _P_GK, _P_GV, _P_GG, _P_SQ, _P_SK, _P_SV, _P_LR, _P_END = 0, 256, 512, 1024, 1536, 2048, 2176, 2304, 2432


def _rope(x, cos, sin):
    lane = lax.broadcasted_iota(jnp.int32, x.shape, 1)
    up = pltpu.roll(x, 112, axis=1)
    dn = pltpu.roll(x, 16, axis=1)
    sw = jnp.where((lane % 32) < 16, up, dn)
    return x * cos + sw * sin


def _proj0_kernel(x_ref, mod_ref, g_ref, w_ref, w2_ref, db_ref, cos_ref, sin_ref,
                  gq_ref, gk_ref, gv_ref, gg_ref, laf_ref, lab_ref, sq_ref, sk_ref, sv_ref,
                  *, d):
    u = (_rms(x_ref[...]) * g_ref[...] * (1.0 + _mod_rows(mod_ref, 1, d))
         + _mod_rows(mod_ref, 0, d)).astype(BF16)

    def seg(a, b):
        return jnp.dot(u, w_ref[:, a:b], preferred_element_type=F32)

    gq_ref[...] = seg(_P_GQ, _P_GK) * (GLA_DK ** -0.5)
    gk_ref[...] = seg(_P_GK, _P_GV)
    gv_ref[...] = seg(_P_GV, _P_GG)
    gg_ref[...] = seg(_P_GG, _P_SQ)
    lr = seg(_P_LR, _P_END)
    z = _hdot(lr, w2_ref[...]) + db_ref[...]
    la = (jnp.minimum(z, 0.0) - jnp.log(1.0 + jnp.exp(-jnp.abs(z)))) * (1.0 / GLA_TAU)
    hk = GLA_HEADS * GLA_DK
    laf_ref[...] = la[:, :hk]
    lab_ref[...] = la[:, hk:]
    cos = cos_ref[...]
    sin = sin_ref[...]
    sq = seg(_P_SQ, _P_SK) * (SWA_HEAD_DIM ** -0.5)
    for c in range(4):
        sq_ref[:, 128 * c:128 * (c + 1)] = _rope(sq[:, 128 * c:128 * (c + 1)], cos, sin)
    sk_ref[...] = _rope(seg(_P_SK, _P_SV), cos, sin)
    sv_ref[...] = seg(_P_SV, _P_LR)


def _proj0(xs, mod, g, w, w2, db, cos, sin, nb, nbatch):
    m, d = xs.shape
    widths = (256, 256, 512, 512, 256, 256, 512, 128, 128)
    return pl.pallas_call(
        functools.partial(_proj0_kernel, d=d),
        grid=(m // TM,),
        in_specs=[pl.BlockSpec((TM, d), lambda i: (i, 0)), _mod_spec(mod, nb, nbatch), _full((1, d)),
                  _full(w.shape), _full(w2.shape), _full(db.shape),
                  pl.BlockSpec((TM, 128), lambda i: (i % nb, 0)),
                  pl.BlockSpec((TM, 128), lambda i: (i % nb, 0))],
        out_specs=[pl.BlockSpec((TM, n), lambda i: (i, 0)) for n in widths],
        out_shape=[jax.ShapeDtypeStruct((m, n), F32) for n in widths],
        compiler_params=_cparams(("arbitrary",), VMEM_LIMIT),
        name="proj0",
    )(xs, mod, g, w, w2, db, cos, sin)


def _gla_dir(q_ref, k_ref, v_ref, la_ref, o_ref, st, cum, mask, tot_row):
    c = GLA_CHUNK
    la = la_ref[...]
    bc = _hdot(cum, la)
    tot = bc[tot_row:tot_row + 1, :]
    q_in = q_ref[...] * jnp.exp(bc)
    k_in = (k_ref[...] * jnp.exp(-bc)).astype(BF16)
    k_out = (k_ref[...] * jnp.exp(tot - bc)).astype(BF16)
    ones = jnp.ones((c, GLA_DV), F32)
    dcol = jnp.exp(lax.dot_general(la, ones, (((0,), (0,)), ((), ())), precision=HI,
                                   preferred_element_type=F32))
    lane = lax.broadcasted_iota(jnp.int32, q_in.shape, 1)
    s_prev = st[...].astype(BF16)
    for h in range(GLA_HEADS):
        qh = jnp.where((lane // GLA_DK) == h, q_in, 0.0).astype(BF16)
        att = lax.dot_general(qh, k_in, (((1,), (1,)), ((), ())), preferred_element_type=F32)
        att = jnp.where(mask, att, 0.0).astype(BF16)
        vh = v_ref[:, h * GLA_DV:(h + 1) * GLA_DV].astype(BF16)
        o = (jnp.dot(att, vh, preferred_element_type=F32)
             + jnp.dot(qh, s_prev, preferred_element_type=F32))
        o_ref[:, h * GLA_DV:(h + 1) * GLA_DV] = o
        upd = lax.dot_general(k_out, vh, (((0,), (0,)), ((), ())), preferred_element_type=F32)
        r0, r1 = h * GLA_DK, (h + 1) * GLA_DK
        st[r0:r1, :] = st[r0:r1, :] * dcol[r0:r1, :] + upd[r0:r1, :]


def _gla_kernel(qf, kf, vf, laf, qb, kb, vb, lab, of_ref, ob_ref, sf, sb):
    s = pl.program_id(1)

    @pl.when(s == 0)
    def _():
        sf[...] = jnp.zeros_like(sf)
        sb[...] = jnp.zeros_like(sb)

    c = GLA_CHUNK
    ri = lax.broadcasted_iota(jnp.int32, (c, c), 0)
    ci = lax.broadcasted_iota(jnp.int32, (c, c), 1)
    _gla_dir(qf, kf, vf, laf, of_ref, sf, (ri >= ci).astype(F32), ri >= ci, c - 1)
    _gla_dir(qb, kb, vb, lab, ob_ref, sb, (ri <= ci).astype(F32), ci > ri, 0)


def _gla(gq, gk, gv, laf, lab, nbatch, nctx_chunks):
    m = gq.shape[0]
    c = GLA_CHUNK
    nch = m // nbatch // c
    hk, hv = GLA_HEADS * GLA_DK, GLA_HEADS * GLA_DV

    def fi(b, s):
        return (b * nch + s, 0)

    def bi(b, s):
        cb = jnp.where(s < nctx_chunks, nctx_chunks - 1 - s, nch - 1 + nctx_chunks - s)
        return (b * nch + cb, 0)

    sk = lambda im: pl.BlockSpec((c, hk), im)
    sv = lambda im: pl.BlockSpec((c, hv), im)
    return pl.pallas_call(
        _gla_kernel,
        grid=(nbatch, nch),
        in_specs=[sk(fi), sk(fi), sv(fi), sk(fi), sk(bi), sk(bi), sv(bi), sk(bi)],
        out_specs=[sv(fi), sv(bi)],
        out_shape=[jax.ShapeDtypeStruct((m, hv), F32)] * 2,
        scratch_shapes=[pltpu.VMEM((hk, GLA_DV), F32), pltpu.VMEM((hk, GLA_DV), F32)],
        compiler_params=_cparams(("arbitrary", "arbitrary")),
        name="gla_scan",
    )(gq, gk, gv, laf, gq, gk, gv, lab)


def _swa_kernel(q_ref, kc_ref, vc_ref, kp_ref, kcur_ref, kn_ref, vp_ref, vcur_ref, vn_ref, sink_ref,
                o_ref, *, nctx_blocks, nlat_blocks):
    j = pl.program_id(1)
    w = WINDOW
    n = j - nctx_blocks
    is_lat = j >= nctx_blocks
    nkc = kc_ref.shape[0]
    kall = jnp.concatenate([kc_ref[...], kp_ref[...], kcur_ref[...], kn_ref[...]], axis=0)
    vall = jnp.concatenate([vc_ref[...], vp_ref[...], vcur_ref[...], vn_ref[...]], axis=0)
    nk = kall.shape[0]
    lane = lax.broadcasted_iota(jnp.int32, kall.shape, 1)
    lo = lane < SWA_HEAD_DIM
    kroll = pltpu.roll(kall, SWA_HEAD_DIM, axis=1)
    vroll = pltpu.roll(vall, SWA_HEAD_DIM, axis=1)
    k2 = [jnp.where(lo, kall, kroll).astype(BF16), jnp.where(lo, kroll, kall).astype(BF16)]
    v2 = [jnp.where(lo, vall, vroll).astype(BF16), jnp.where(lo, vroll, vall).astype(BF16)]
    qi = lax.broadcasted_iota(jnp.int32, (w, nk), 0)
    kj = lax.broadcasted_iota(jnp.int32, (w, nk), 1)
    qpos = n * w + qi
    kpos = (n - 1) * w + (kj - nkc)
    loc_ok = (jnp.abs(qpos - kpos) <= WINDOW) & (kpos >= 0) & (kpos < nlat_blocks * w) & is_lat
    valid = (kj < nkc) | ((kj >= nkc) & loc_ok)
    qlane = lax.broadcasted_iota(jnp.int32, (w, 128), 1)
    qlo = qlane < SWA_HEAD_DIM
    rep = SWA_HEADS // SWA_KV_HEADS
    for pr in range(SWA_HEADS // 2):
        g = (2 * pr) // rep
        qp = q_ref[:, 128 * pr:128 * (pr + 1)]
        outs = []
        for half in range(2):
            h = 2 * pr + half
            qh = jnp.where(qlo if half == 0 else ~qlo, qp, 0.0).astype(BF16)
            s = lax.dot_general(qh, k2[g], (((1,), (1,)), ((), ())), preferred_element_type=F32)
            s = jnp.where(valid, s, NEG)
            sink = sink_ref[h:h + 1, 0:1]
            mx = jnp.maximum(jnp.max(s, axis=-1, keepdims=True), sink)
            p = jnp.exp(s - mx)
            den = jnp.sum(p, axis=-1, keepdims=True) + jnp.exp(sink - mx)
            o = jnp.dot(p.astype(BF16), v2[g], preferred_element_type=F32)
            outs.append(o / den)
        o_ref[:, 128 * pr:128 * (pr + 1)] = jnp.where(qlo, outs[0], outs[1])


def _swa(sq, sk, sv, sink, nbatch, nctx_blocks, nlat_blocks):
    m = sq.shape[0]
    w = WINDOW
    nblk = nctx_blocks + nlat_blocks
    nkc = nctx_blocks * w

    def qmap(b, j):
        return (b * nblk + j, 0)

    def cmap(b, j):
        return (b, 0)

    def nmap(off):
        def f(b, j):
            n = jnp.clip(j - nctx_blocks + off, 0, nlat_blocks - 1)
            return (b * nblk + nctx_blocks + n, 0)
        return f

    kvw = sk.shape[1]
    kcs = pl.BlockSpec((nkc, kvw), lambda b, j: (b * (nblk * w // nkc), 0))
    kvs = lambda off: pl.BlockSpec((w, kvw), nmap(off))
    return pl.pallas_call(
        functools.partial(_swa_kernel, nctx_blocks=nctx_blocks, nlat_blocks=nlat_blocks),
        grid=(nbatch, nblk),
        in_specs=[pl.BlockSpec((w, sq.shape[1]), qmap), kcs, kcs, kvs(-1), kvs(0), kvs(1),
                  kvs(-1), kvs(0), kvs(1), _full(sink.shape)],
        out_specs=pl.BlockSpec((w, sq.shape[1]), qmap),
        out_shape=jax.ShapeDtypeStruct(sq.shape, F32),
        compiler_params=_cparams(("arbitrary", "arbitrary")),
        name="swa_attn",
    )(sq, sk, sv, sk, sk, sk, sv, sv, sv, sink)


def _residual_epilogue(x, mix, mod_ref, n2_ref, xo_ref, f_ref, ft_ref, d):
    xn = x + _mod_rows(mod_ref, 2, d) * mix
    xo_ref[...] = xn
    f = _rms(xn) * n2_ref[...] * (1.0 + _mod_rows(mod_ref, 4, d)) + _mod_rows(mod_ref, 3, d)
    f_ref[...] = f
    ft_ref[...] = f.T.astype(BF16)


def _out0_kernel(x_ref, of_ref, ob_ref, gg_ref, a_ref, mod_ref, gn_ref, wo_ref, n2_ref,
                 xo_ref, f_ref, ft_ref, *, d):
    o = of_ref[...] + ob_ref[...]
    gate = gg_ref[...]
    gate = gate * _sigmoid(gate)
    parts = []
    for h in range(GLA_HEADS):
        oh = o[:, h * GLA_DV:(h + 1) * GLA_DV]
        parts.append(oh * lax.rsqrt(jnp.mean(oh * oh, axis=-1, keepdims=True) + NORM_EPS))
    on = jnp.concatenate(parts, axis=1) * gn_ref[...] * gate
    hv = GLA_HEADS * GLA_DV
    mix = _bdot(on, wo_ref[0:hv, :]) + _bdot(a_ref[...], wo_ref[hv:, :])
    _residual_epilogue(x_ref[...], mix, mod_ref, n2_ref, xo_ref, f_ref, ft_ref, d)


def _tok_spec(n):
    return pl.BlockSpec((TM, n), lambda i: (i, 0))


def _epilogue_specs(m, d):
    out_specs = [_tok_spec(d), _tok_spec(d), pl.BlockSpec((d, TM), lambda i: (0, i))]
    out_shape = [jax.ShapeDtypeStruct((m, d), F32), jax.ShapeDtypeStruct((m, d), F32),
                 jax.ShapeDtypeStruct((d, m), BF16)]
    return out_specs, out_shape


def _out0(xs, of, ob, gg, a, mod, gn, wo, n2, nb, nbatch):
    m, d = xs.shape
    out_specs, out_shape = _epilogue_specs(m, d)
    return pl.pallas_call(
        functools.partial(_out0_kernel, d=d),
        grid=(m // TM,),
        in_specs=[_tok_spec(d), _tok_spec(of.shape[1]), _tok_spec(ob.shape[1]), _tok_spec(gg.shape[1]),
                  _tok_spec(a.shape[1]), _mod_spec(mod, nb, nbatch), _full(gn.shape), _full(wo.shape),
                  _full(n2.shape)],
        out_specs=out_specs, out_shape=out_shape,
        compiler_params=_cparams(("arbitrary",), VMEM_LIMIT),
        name="out0",
    )(xs, of, ob, gg, a, mod, gn, wo, n2)


def _top_vals(s, k):
    rows = lax.broadcasted_iota(jnp.int32, (k, s.shape[1]), 0)
    vals = jnp.zeros((k, s.shape[1]), F32)
    cur = s
    for t in range(k):
        mx = jnp.max(cur, axis=0, keepdims=True)
        vals = jnp.where(rows == t, mx, vals)
        cur = jnp.where(cur == mx, NEG, cur)
    return vals


def _peer_q_kernel(f_ref, wh_ref, wl_ref, k1_ref, k2_ref, s1_ref, s2_ref, e1_ref, e2_ref, tau_ref):
    f = f_ref[...]
    fh = f.astype(BF16)
    fl = (f - fh.astype(F32)).astype(BF16)
    kk = PEER_TOPK
    for h in range(PEER_HEADS):
        c0 = h * 2 * PEER_NKEYS
        wh = wh_ref[:, c0:c0 + 2 * PEER_NKEYS]
        wl = wl_ref[:, c0:c0 + 2 * PEER_NKEYS]
        q = (jnp.dot(fh, wh, preferred_element_type=F32) + jnp.dot(fl, wh, preferred_element_type=F32)
             + jnp.dot(fh, wl, preferred_element_type=F32))
        nt = (((1,), (1,)), ((), ()))
        s1 = lax.dot_general(k1_ref[h], q[:, :PEER_NKEYS], nt, precision=HI, preferred_element_type=F32)
        s2 = lax.dot_general(k2_ref[h], q[:, PEER_NKEYS:], nt, precision=HI, preferred_element_type=F32)
        v1 = _top_vals(s1, kk)
        v2 = _top_vals(s2, kk)
        cand = jnp.concatenate([v1[r:r + 1, :] + v2 for r in range(kk)], axis=0)
        top = _top_vals(cand, kk)
        tau = top[kk - 1:kk, :]
        z = jnp.sum(jnp.exp(top - top[0:1, :]), axis=0, keepdims=True)
        s1_ref[h] = s1
        s2_ref[h] = s2
        e1_ref[h] = jnp.exp(s1 - v1[0:1, :])
        e2_ref[h] = jnp.exp(s2 - v2[0:1, :]) / z
        tau_ref[h:h + 1, :] = tau


def _peer_q(f, wh, wl, k1, k2):
    m, d = f.shape
    hh, nk = PEER_HEADS, PEER_NKEYS
    big = pl.BlockSpec((hh, nk, TM), lambda i: (0, 0, i))
    bshape = jax.ShapeDtypeStruct((hh, nk, m), F32)
    return pl.pallas_call(
        _peer_q_kernel,
        grid=(m // TM,),
        in_specs=[_tok_spec(d), _full(wh.shape), _full(wl.shape), _full(k1.shape), _full(k2.shape)],
        out_specs=[big, big, big, big, pl.BlockSpec((hh, TM), lambda i: (0, i))],
        out_shape=[bshape, bshape, bshape, bshape, jax.ShapeDtypeStruct((hh, m), F32)],
        compiler_params=_cparams(("arbitrary",), VMEM_LIMIT),
        name="peer_query",
    )(f, wh, wl, k1, k2)


PD_TM = 512
PD_TE = 1024


def _gelu(x):
    return 0.5 * x * (1.0 + lax.erf(x * (2.0 ** -0.5)))


def _peer_dense_kernel(ft_ref, u_ref, vt_ref, s1_ref, s2_ref, e1_ref, e2_ref, tau_ref, o_ref,
                       acc_ref, ht_ref, ct_ref):
    k = pl.program_id(1)
    nk = PEER_NKEYS
    nslab = PD_TE // nk

    @pl.when(k == 0)
    def _():
        acc_ref[...] = jnp.zeros_like(acc_ref)

    ht_ref[...] = jnp.dot(u_ref[...], ft_ref[...], preferred_element_type=F32)

    assert nslab == 8
    a0 = pl.multiple_of(k * nslab, nslab)
    for al in range(nslab):
        wg = jnp.zeros((nk, PD_TM), F32)
        for h in range(PEER_HEADS):
            s1t = s1_ref[h, pl.ds(a0, nslab), :]
            e1t = e1_ref[h, pl.ds(a0, nslab), :]
            sc = s1t[al:al + 1, :] + s2_ref[h]
            wg = wg + jnp.where(sc >= tau_ref[h:h + 1, :], e1t[al:al + 1, :] * e2_ref[h], 0.0)
        r0 = al * nk
        ct_ref[r0:r0 + nk, :] = (_gelu(ht_ref[r0:r0 + nk, :]) * wg).astype(BF16)
    acc_ref[...] += jnp.dot(vt_ref[...], ct_ref[...], preferred_element_type=F32)

    @pl.when(k == pl.num_programs(1) - 1)
    def _():
        o_ref[...] = acc_ref[...].T


def _peer_dense(ft, u, vt, s1, s2, e1, e2, tau):
    d, m = ft.shape
    ne = u.shape[0]
    hh, nk = PEER_HEADS, PEER_NKEYS
    big = pl.BlockSpec((hh, nk, PD_TM), lambda i, k: (0, 0, i))
    return pl.pallas_call(
        _peer_dense_kernel,
        grid=(m // PD_TM, ne // PD_TE),
        in_specs=[pl.BlockSpec((d, PD_TM), lambda i, k: (0, i)),
                  pl.BlockSpec((PD_TE, d), lambda i, k: (k, 0)),
                  pl.BlockSpec((d, PD_TE), lambda i, k: (0, k)),
                  big, big, big, big, pl.BlockSpec((hh, PD_TM), lambda i, k: (0, i))],
        out_specs=pl.BlockSpec((PD_TM, d), lambda i, k: (i, 0)),
        out_shape=jax.ShapeDtypeStruct((m, d), F32),
        scratch_shapes=[pltpu.VMEM((d, PD_TM), F32), pltpu.VMEM((PD_TE, PD_TM), F32),
                        pltpu.VMEM((PD_TE, PD_TM), BF16)],
        compiler_params=_cparams(("arbitrary", "arbitrary"), VMEM_LIMIT),
        name="peer_dense",
    )(ft, u, vt, s1, s2, e1, e2, tau)


def _res_kernel(x_ref, p_ref, mod_ref, modn_ref, g_ref, xo_ref, u_ref, *, d, final):
    xn = x_ref[...] + _mod_rows(mod_ref, 5, d) * p_ref[...]
    xo_ref[...] = xn
    y = _rms(xn) * g_ref[...]
    if not final:
        y = y * (1.0 + _mod_rows(modn_ref, 1, d)) + _mod_rows(modn_ref, 0, d)
    u_ref[...] = y


def _res(xs, p, mod, modn, g, nb, nbatch, final):
    m, d = xs.shape
    return pl.pallas_call(
        functools.partial(_res_kernel, d=d, final=final),
        grid=(m // TM,),
        in_specs=[_tok_spec(d), _tok_spec(d), _mod_spec(mod, nb, nbatch), _mod_spec(modn, nb, nbatch),
                  _full(g.shape)],
        out_specs=[_tok_spec(d), _tok_spec(d)],
        out_shape=[jax.ShapeDtypeStruct((m, d), F32)] * 2,
        compiler_params=_cparams(("arbitrary",)),
        name="peer_residual",
    )(xs, p, mod, modn, g)


def _seg_sum(x, g_ref, gt_ref):
    return _hdot(_hdot(x, g_ref[...]), gt_ref[...])


def _rwkv_prep_kernel(u_ref, up_ref, un_ref, mu_ref, wr_ref, wk_ref, wv_ref, w0_ref, w1_ref, w2_ref,
                      a0_ref, a1_ref, a2_ref, g1_ref, g2_ref, kk_ref_, ka_ref, rk_ref, sg_ref, sgt_ref,
                      wf_o, wb_o, kf_o, kb_o, bf_o, bb_o, r_o, v_o, kk_o, g_o, bonus_o, *, nb, d):
    i = pl.program_id(0)
    p = i % nb
    u = u_ref[...]
    has_prev = (p >= 2).astype(F32)
    has_next = ((p >= 1) & (p <= nb - 2)).astype(F32)
    prev_row = up_ref[7:8, :] * has_prev
    next_row = un_ref[0:1, :] * has_next
    rows = lax.broadcasted_iota(jnp.int32, u.shape, 0)
    up = jnp.where(rows == 0, prev_row, pltpu.roll(u, 1, axis=0))
    un = jnp.where(rows == TM - 1, next_row, pltpu.roll(u, TM - 1, axis=0))
    xx = 0.5 * (up + un) - u

    def mix(zi):
        return (u + xx * mu_ref[zi:zi + 1, :]).astype(BF16)

    r = jnp.dot(mix(0), wr_ref[...], preferred_element_type=F32)
    k = jnp.dot(mix(2), wk_ref[...], preferred_element_type=F32)
    v = jnp.dot(mix(3), wv_ref[...], preferred_element_type=F32)
    lw = w0_ref[...] + _bdot(jnp.tanh(jnp.dot(mix(1), w1_ref[...], preferred_element_type=F32)), w2_ref[...])
    decay = jnp.exp(-_sigmoid(lw) * float(np.exp(-0.5)))
    a = _sigmoid(a0_ref[...] + _bdot(jnp.dot(mix(4), a1_ref[...], preferred_element_type=F32), a2_ref[...]))
    g = _bdot(_sigmoid(jnp.dot(mix(5), g1_ref[...], preferred_element_type=F32)), g2_ref[...])
    kk = k * kk_ref_[...]
    kk = kk * lax.rsqrt(jnp.maximum(_seg_sum(kk * kk, sg_ref, sgt_ref), 1e-24))
    ka = ka_ref[...]
    kf = k * (1.0 + (a[:, :d] - 1.0) * ka)
    kb = k * (1.0 + (a[:, d:] - 1.0) * ka)
    bonus = _seg_sum(r * (kf + kb) * rk_ref[...], sg_ref, sgt_ref) * v
    wf_o[...] = decay[:, :d]
    wb_o[...] = decay[:, d:]
    kf_o[...] = kf
    kb_o[...] = kb
    bf_o[...] = kk * a[:, :d]
    bb_o[...] = kk * a[:, d:]
    r_o[...] = r
    v_o[...] = v
    kk_o[...] = kk
    g_o[...] = g
    bonus_o[...] = bonus


def _rwkv_prep(u, consts, nb):
    m, d = u.shape
    r8 = TM // 8
    nb8 = m // 8
    in_specs = [_tok_spec(d),
                pl.BlockSpec((8, d), lambda i: (jnp.maximum(i * r8 - 1, 0), 0)),
                pl.BlockSpec((8, d), lambda i: (jnp.minimum((i + 1) * r8, nb8 - 1), 0))]
    in_specs += [_full(c.shape) for c in consts]
    return pl.pallas_call(
        functools.partial(_rwkv_prep_kernel, nb=nb, d=d),
        grid=(m // TM,),
        in_specs=in_specs,
        out_specs=[_tok_spec(d)] * 11,
        out_shape=[jax.ShapeDtypeStruct((m, d), F32)] * 11,
        compiler_params=_cparams(("arbitrary",), VMEM_LIMIT),
        name="rwkv_prep",
    )(u, u, u, *consts)


RW_TC = 256


def _rwkv_scan_kernel(wf, kf, bf, rf, vf, kkf, wb, kb, bb, rb, vb, kkb, yf_ref, yb_ref, st, *, npair):
    s = pl.program_id(1)
    n = RW_HEAD_DIM
    grp = 8

    @pl.when(s == 0)
    def _():
        st[...] = jnp.zeros_like(st)

    ri = lax.broadcasted_iota(jnp.int32, (n, 2 * n), 0)
    li = lax.broadcasted_iota(jnp.int32, (n, 2 * n), 1)
    eye = (li % n) == ri
    bi = lax.broadcasted_iota(jnp.int32, (2 * n, 2 * n), 0)
    bj = lax.broadcasted_iota(jnp.int32, (2 * n, 2 * n), 1)
    ones_blk = ((bi // n) == (bj // n)).astype(BF16)
    rows8 = lax.broadcasted_iota(jnp.int32, (grp, 2 * n), 0)
    dirs = ((wf, kf, bf, rf, vf, kkf, yf_ref), (wb, kb, bb, rb, vb, kkb, yb_ref))
    nch = 2 * npair
    ngrp = RW_TC // grp

    def split(x):
        hi = x.astype(BF16)
        lo = (x - hi.astype(F32)).astype(BF16)
        return hi, lo

    def group(ig, carry):
        bases = (pl.multiple_of(ig * grp, grp), pl.multiple_of((ngrp - 1 - ig) * grp, grp))
        tiles = []
        for dd in range(2):
            for p in range(npair):
                ln = slice(2 * n * p, 2 * n * (p + 1))
                tiles.append([ref[pl.ds(bases[dd], grp), ln] for ref in dirs[dd][:6]])
        ytiles = [jnp.zeros((grp, 2 * n), F32) for _ in range(nch)]
        states = [st[c] for c in range(nch)]
        for j in range(grp):
            xs, xv, rows = [], [], []
            for c in range(nch):
                jj = j if c < npair else grp - 1 - j
                w_t, k_t, b_t, r_t, v_t, kk_t = tiles[c]
                row = lambda tl: jnp.broadcast_to(tl[jj:jj + 1, :], (n, 2 * n))
                rows.append((row(w_t), row(k_t), row(b_t), row(r_t)))
                xs.append(states[c] * row(kk_t))
                xv.append(jnp.where(eye, row(v_t), 0.0))
            hi, lo = split(jnp.concatenate(xs + xv, axis=0))
            red = (jnp.dot(hi, ones_blk, preferred_element_type=F32)
                   + jnp.dot(lo, ones_blk, preferred_element_type=F32))
            ys = []
            for c in range(nch):
                w, k, b, r = rows[c]
                sa = red[c * n:(c + 1) * n, :]
                vc = red[(nch + c) * n:(nch + c + 1) * n, :]
                states[c] = states[c] * w - sa * b + vc * k
                ys.append((states[c] * r).astype(BF16))
            yr = jnp.dot(jnp.concatenate(ys, axis=0), ones_blk, preferred_element_type=F32)
            for c in range(nch):
                jj = j if c < npair else grp - 1 - j
                yrow = jnp.sum(jnp.where(eye, yr[c * n:(c + 1) * n, :], 0.0), axis=0, keepdims=True)
                ytiles[c] = jnp.where(rows8 == jj, yrow, ytiles[c])
        for c in range(nch):
            st[c] = states[c]
            dd, p = divmod(c, npair)
            dirs[dd][6][pl.ds(bases[dd], grp), 2 * n * p:2 * n * (p + 1)] = ytiles[c]
        return carry

    lax.fori_loop(0, ngrp, group, 0)


def _rwkv_scan(wf, kf, bf, wb, kb, bb, r, v, kk, nbatch):
    m, d = r.shape
    nch = m // nbatch // RW_TC
    npair = d // (2 * RW_HEAD_DIM)

    def fi(b, s):
        return (b * nch + s, 0)

    def bi(b, s):
        return (b * nch + jnp.where(s == 0, 0, nch - s), 0)

    sf = pl.BlockSpec((RW_TC, d), fi)
    sb = pl.BlockSpec((RW_TC, d), bi)
    return pl.pallas_call(
        functools.partial(_rwkv_scan_kernel, npair=npair),
        grid=(nbatch, nch),
        in_specs=[sf] * 6 + [sb] * 6,
        out_specs=[sf, sb],
        out_shape=[jax.ShapeDtypeStruct((m, d), F32)] * 2,
        scratch_shapes=[pltpu.VMEM((2 * npair, RW_HEAD_DIM, 2 * RW_HEAD_DIM), F32)],
        compiler_params=_cparams(("arbitrary", "arbitrary"), VMEM_LIMIT),
        name="rwkv_scan",
    )(wf, kf, bf, r, v, kk, wb, kb, bb, r, v, kk)


def _rwkv_out_kernel(x_ref, yf_ref, yb_ref, bonus_ref, g_ref, mod_ref, lg_ref, lb_ref, wo_ref, n2_ref,
                     sg_ref, sgt_ref, xo_ref, f_ref, ft_ref, *, d):
    y = yf_ref[...] + yb_ref[...]
    inv = 1.0 / RW_HEAD_DIM
    mean = _seg_sum(y, sg_ref, sgt_ref) * inv
    yc = y - mean
    var = _seg_sum(yc * yc, sg_ref, sgt_ref) * inv
    yn = yc * lax.rsqrt(var + RW_GN_EPS) * lg_ref[...] + lb_ref[...]
    out = (yn + bonus_ref[...]) * g_ref[...]
    mix = _bdot(out, wo_ref[...])
    _residual_epilogue(x_ref[...], mix, mod_ref, n2_ref, xo_ref, f_ref, ft_ref, d)


def _rwkv_out(xs, yf, yb, bonus, g, mod, lg, lb, wo, n2, sg, sgt, nb, nbatch):
    m, d = xs.shape
    out_specs, out_shape = _epilogue_specs(m, d)
    return pl.pallas_call(
        functools.partial(_rwkv_out_kernel, d=d),
        grid=(m // TM,),
        in_specs=[_tok_spec(d)] * 5 + [_mod_spec(mod, nb, nbatch)]
        + [_full(a.shape) for a in (lg, lb, wo, n2, sg, sgt)],
        out_specs=out_specs, out_shape=out_shape,
        compiler_params=_cparams(("arbitrary",), VMEM_LIMIT),
        name="rwkv_out",
    )(xs, yf, yb, bonus, g, mod, lg, lb, wo, n2, sg, sgt)


def _rope_tables(t_lat, n_ctx):
    rows = t_lat // GRID_W
    row = jnp.broadcast_to(jnp.arange(rows, dtype=F32)[:, None], (rows, GRID_W)).reshape(-1)
    col = jnp.broadcast_to(jnp.arange(GRID_W, dtype=F32)[None, :], (rows, GRID_W)).reshape(-1)
    n_freq = SWA_HEAD_DIM // 4
    inv_freq = ROPE_BASE ** (-jnp.arange(n_freq, dtype=F32) / n_freq)
    ar = row[:, None] * inv_freq
    ac = col[:, None] * inv_freq
    cos = jnp.concatenate([jnp.cos(ar), jnp.cos(ar), jnp.cos(ac), jnp.cos(ac)], axis=1)
    sin = jnp.concatenate([-jnp.sin(ar), jnp.sin(ar), -jnp.sin(ac), jnp.sin(ac)], axis=1)
    cos = jnp.concatenate([jnp.ones((n_ctx, SWA_HEAD_DIM), F32), cos], axis=0)
    sin = jnp.concatenate([jnp.zeros((n_ctx, SWA_HEAD_DIM), F32), sin], axis=0)
    return jnp.tile(cos, (1, 2)), jnp.tile(sin, (1, 2))


def _peer(f, ft, w_q, k1, k2, u, v):
    wh = w_q.astype(BF16)
    wl = (w_q - wh.astype(F32)).astype(BF16)
    s1, s2, e1, e2, tau = _peer_q(f, wh, wl, k1, k2)
    return _peer_dense(ft, u.astype(BF16), v.T.astype(BF16), s1, s2, e1, e2, tau)


def kernel(x, c, ctx, c_ctx, ada_w, ada_b, norm1_g, norm2_g, ab_w_in, gla_dec_w2, gla_dec_b, gla_norm_g,
           swa_sink, ab_w_out, rw_mu, rw_w_rkv, rw_w_o, rw_w0, rw_w1, rw_w2, rw_a0, rw_a1, rw_a2, rw_g1,
           rw_g2, rw_k_k, rw_k_a, rw_r_k, rw_ln_g, rw_ln_b, peer_w_q, peer_k1, peer_k2, peer_u, peer_v,
           final_g):
    nbatch, t_lat, d = x.shape
    n_ctx = ctx.shape[1]
    assert n_ctx == TM and t_lat % TM == 0 and d == 1024
    s_tok = n_ctx + t_lat
    nb = s_tok // TM
    m = nbatch * s_tok
    xs = jnp.concatenate([ctx, x], axis=1).reshape(m, d)

    cc = jnp.zeros((16, d), F32).at[:nbatch].set(c).at[nbatch].set(c_ctx)
    mod0 = _ada_table(cc, ada_w[0], ada_b[0]).reshape(16, 1, 6 * d)
    mod1 = _ada_table(cc, ada_w[1], ada_b[1]).reshape(16, 1, 6 * d)
    row2 = lambda a: a.reshape(1, -1)

    w_in = ab_w_in[0]
    cuts = np.cumsum((256, 256, 512, 512, 32, 512, 128, 128))[:-1].tolist()
    wgq, wgk, wgv, wgg, wlr, wsq, wsk, wsv = jnp.split(w_in, cuts, axis=1)
    wp = jnp.concatenate([wgq, wgk, wgv, wgg, wsq, wsk, wsv, wlr,
                          jnp.zeros((d, _P_END - _P_LR - 32), F32)], axis=1).astype(BF16)
    hk = GLA_HEADS * GLA_DK
    w2p = jnp.zeros((128, 2 * hk), F32)
    w2p = w2p.at[0:GLA_LOWRANK, 0:hk].set(gla_dec_w2[0, 0]).at[GLA_LOWRANK:2 * GLA_LOWRANK, hk:].set(gla_dec_w2[0, 1])
    db = gla_dec_b[0].reshape(1, 2 * hk)
    cos, sin = _rope_tables(t_lat, n_ctx)
    gq, gk, gv, gg, laf, lab, sq, sk, sv = _proj0(xs, mod0, row2(norm1_g[0]), wp, w2p, db, cos, sin, nb, nbatch)
    of, ob = _gla(gq, gk, gv, laf, lab, nbatch, n_ctx // GLA_CHUNK)
    sink = jnp.broadcast_to(swa_sink[0][:, None], (SWA_HEADS, 128))
    att = _swa(sq, sk, sv, sink, nbatch, n_ctx // WINDOW, t_lat // WINDOW)
    xs, f, ft = _out0(xs, of, ob, gg, att, mod0, row2(gla_norm_g[0]), ab_w_out[0].astype(BF16),
                      row2(norm2_g[0]), nb, nbatch)
    p = _peer(f, ft, peer_w_q[0], peer_k1[0], peer_k2[0], peer_u[0], peer_v[0])
    xs, u = _res(xs, p, mod0, mod1, row2(norm1_g[1]), nb, nbatch, final=False)

    lora = rw_w1.shape[-1]
    w1c = jnp.concatenate([rw_w1[0, 0], rw_w1[0, 1]], axis=1).astype(BF16)
    w2c = jnp.zeros((2 * lora, 2 * d), F32).at[:lora, :d].set(rw_w2[0, 0]).at[lora:, d:].set(rw_w2[0, 1]).astype(BF16)
    la = rw_a1.shape[-1]
    a1c = jnp.concatenate([rw_a1[0, 0], rw_a1[0, 1]], axis=1).astype(BF16)
    a2c = jnp.zeros((2 * la, 2 * d), F32).at[:la, :d].set(rw_a2[0, 0]).at[la:, d:].set(rw_a2[0, 1]).astype(BF16)
    lg = rw_g1.shape[-1]
    g1p = jnp.zeros((d, 256), F32).at[:, :lg].set(rw_g1[0]).astype(BF16)
    g2p = jnp.zeros((256, d), F32).at[:lg].set(rw_g2[0]).astype(BF16)
    nheads = d // RW_HEAD_DIM
    sg = (jnp.arange(d)[:, None] // RW_HEAD_DIM == jnp.arange(128)[None, :]).astype(F32)
    sgt = sg.T
    consts = [rw_mu[0], rw_w_rkv[0, 0].astype(BF16), rw_w_rkv[0, 1].astype(BF16), rw_w_rkv[0, 2].astype(BF16),
              rw_w0[0].reshape(1, 2 * d), w1c, w2c, rw_a0[0].reshape(1, 2 * d), a1c, a2c, g1p, g2p,
              row2(rw_k_k[0]), row2(rw_k_a[0]), rw_r_k[0].reshape(1, d), sg, sgt]
    mu8 = jnp.zeros((8, d), F32).at[:6].set(rw_mu[0])
    consts[0] = mu8
    wf, wb, kf, kb, bf, bb, r, v, kk, g, bonus = _rwkv_prep(u, consts, nb)
    yf, yb = _rwkv_scan(wf, kf, bf, wb, kb, bb, r, v, kk, nbatch)
    xs, f, ft = _rwkv_out(xs, yf, yb, bonus, g, mod1, row2(rw_ln_g[0]), row2(rw_ln_b[0]),
                          rw_w_o[0].astype(BF16), row2(norm2_g[1]), sg, sgt, nb, nbatch)
    p = _peer(f, ft, peer_w_q[1], peer_k1[1], peer_k2[1], peer_u[1], peer_v[1])
    _, y = _res(xs, p, mod1, mod1, row2(final_g), nb, nbatch, final=True)
    return y.reshape(nbatch, s_tok, d)[:, n_ctx:]
```

```python
import functools

import numpy as np
import jax
import jax.numpy as jnp
from jax import lax
from jax.experimental import pallas as pl
from jax.experimental.pallas import tpu as pltpu

F32 = jnp.float32
BF16 = jnp.bfloat16
HI = lax.Precision.HIGHEST

NORM_EPS = 1e-6
GLA_HEADS, GLA_DK, GLA_DV, GLA_LOWRANK, GLA_TAU, GLA_CHUNK = 4, 64, 128, 16, 16.0, 64
SWA_HEADS, SWA_KV_HEADS, SWA_HEAD_DIM, WINDOW = 8, 2, 64, 128
ROPE_BASE = 10000.0
GRID_W = 64
RW_HEAD_DIM = 64
RW_GN_EPS = 64e-5
PEER_HEADS, PEER_NKEYS, PEER_TOPK = 8, 128, 16
NEG = -1e30

TM = 256
VMEM_LIMIT = 56 * 1024 * 1024


def _cparams(sem, vmem=None):
    return pltpu.CompilerParams(dimension_semantics=sem, vmem_limit_bytes=vmem)


def _bdot(a, b):
    return jnp.dot(a.astype(BF16), b.astype(BF16), preferred_element_type=F32)


def _hdot(a, b):
    return jnp.dot(a, b, precision=HI, preferred_element_type=F32)


def _sigmoid(x):
    return 1.0 / (1.0 + jnp.exp(-x))


def _rms(x):
    return x * lax.rsqrt(jnp.mean(x * x, axis=-1, keepdims=True) + NORM_EPS)


def _mod_rows(mod_ref, k, d):
    return mod_ref[0, :, k * d:(k + 1) * d]


def _mod_spec(mod, nb, nbatch):
    return pl.BlockSpec((1, 1, mod.shape[2]), lambda i: (jnp.where(i % nb == 0, nbatch, i // nb), 0, 0))


def _full(shape):
    n = len(shape)
    return pl.BlockSpec(shape, lambda *_: (0,) * n)


def _ada_kernel(c_ref, w_ref, b_ref, o_ref):
    c = c_ref[...]
    s = c * _sigmoid(c)
    o_ref[...] = _hdot(s, w_ref[...]) + b_ref[...]


def _ada_table(cc, w, b):
    rows, d = cc.shape
    n = w.shape[1]
    tn = 512
    return pl.pallas_call(
        _ada_kernel,
        grid=(n // tn,),
        in_specs=[_full((rows, d)), pl.BlockSpec((d, tn), lambda j: (0, j)),
                  pl.BlockSpec((1, tn), lambda j: (0, j))],
        out_specs=pl.BlockSpec((rows, tn), lambda j: (0, j)),
        out_shape=jax.ShapeDtypeStruct((rows, n), F32),
        compiler_params=_cparams(("arbitrary",)),
        name="ada_table",
    )(cc, w, b.reshape(1, n))


_P_GQ, _P_GK, _P_GV, _P_GG, _P_SQ, _P_SK, _P_SV, _P_LR, _P_END = 0, 256, 512, 1024, 1536, 2048, 2176, 2304, 2432


def _rope(x, cos, sin):
    lane = lax.broadcasted_iota(jnp.int32, x.shape, 1)
    up = pltpu.roll(x, 112, axis=1)
    dn = pltpu.roll(x, 16, axis=1)
    sw = jnp.where((lane % 32) < 16, up, dn)
    return x * cos + sw * sin


def _proj0_kernel(x_ref, mod_ref, g_ref, w_ref, w2_ref, db_ref, cos_ref, sin_ref,
                  gq_ref, gk_ref, gv_ref, gg_ref, laf_ref, lab_ref, sq_ref, sk_ref, sv_ref,
                  *, d):
    u = (_rms(x_ref[...]) * g_ref[...] * (1.0 + _mod_rows(mod_ref, 1, d))
         + _mod_rows(mod_ref, 0, d)).astype(BF16)

    def seg(a, b):
        return jnp.dot(u, w_ref[:, a:b], preferred_element_type=F32)

    gq_ref[...] = seg(_P_GQ, _P_GK) * (GLA_DK ** -0.5)
    gk_ref[...] = seg(_P_GK, _P_GV)
    gv_ref[...] = seg(_P_GV, _P_GG)
    gg_ref[...] = seg(_P_GG, _P_SQ)
    lr = seg(_P_LR, _P_END)
    z = _hdot(lr, w2_ref[...]) + db_ref[...]
    la = (jnp.minimum(z, 0.0) - jnp.log(1.0 + jnp.exp(-jnp.abs(z)))) * (1.0 / GLA_TAU)
    hk = GLA_HEADS * GLA_DK
    laf_ref[...] = la[:, :hk]
    lab_ref[...] = la[:, hk:]
    cos = cos_ref[...]
    sin = sin_ref[...]
    sq = seg(_P_SQ, _P_SK) * (SWA_HEAD_DIM ** -0.5)
    for c in range(4):
        sq_ref[:, 128 * c:128 * (c + 1)] = _rope(sq[:, 128 * c:128 * (c + 1)], cos, sin)
    sk_ref[...] = _rope(seg(_P_SK, _P_SV), cos, sin)
    sv_ref[...] = seg(_P_SV, _P_LR)


def _proj0(xs, mod, g, w, w2, db, cos, sin, nb, nbatch):
    m, d = xs.shape
    widths = (256, 256, 512, 512, 256, 256, 512, 128, 128)
    return pl.pallas_call(
        functools.partial(_proj0_kernel, d=d),
        grid=(m // TM,),
        in_specs=[pl.BlockSpec((TM, d), lambda i: (i, 0)), _mod_spec(mod, nb, nbatch), _full((1, d)),
                  _full(w.shape), _full(w2.shape), _full(db.shape),
                  pl.BlockSpec((TM, 128), lambda i: (i % nb, 0)),
                  pl.BlockSpec((TM, 128), lambda i: (i % nb, 0))],
        out_specs=[pl.BlockSpec((TM, n), lambda i: (i, 0)) for n in widths],
        out_shape=[jax.ShapeDtypeStruct((m, n), F32) for n in widths],
        compiler_params=_cparams(("arbitrary",), VMEM_LIMIT),
        name="proj0",
    )(xs, mod, g, w, w2, db, cos, sin)


def _gla_dir(q_ref, k_ref, v_ref, la_ref, o_ref, st, cum, mask, tot_row):
    c = GLA_CHUNK
    la = la_ref[...]
    bc = _hdot(cum, la)
    tot = bc[tot_row:tot_row + 1, :]
    q_in = q_ref[...] * jnp.exp(bc)
    k_in = (k_ref[...] * jnp.exp(-bc)).astype(BF16)
    k_out = (k_ref[...] * jnp.exp(tot - bc)).astype(BF16)
    ones = jnp.ones((c, GLA_DV), F32)
    dcol = jnp.exp(lax.dot_general(la, ones, (((0,), (0,)), ((), ())), precision=HI,
                                   preferred_element_type=F32))
    lane = lax.broadcasted_iota(jnp.int32, q_in.shape, 1)
    s_prev = st[...].astype(BF16)
    for h in range(GLA_HEADS):
        qh = jnp.where((lane // GLA_DK) == h, q_in, 0.0).astype(BF16)
        att = lax.dot_general(qh, k_in, (((1,), (1,)), ((), ())), preferred_element_type=F32)
        att = jnp.where(mask, att, 0.0).astype(BF16)
        vh = v_ref[:, h * GLA_DV:(h + 1) * GLA_DV].astype(BF16)
        o = (jnp.dot(att, vh, preferred_element_type=F32)
             + jnp.dot(qh, s_prev, preferred_element_type=F32))
        o_ref[:, h * GLA_DV:(h + 1) * GLA_DV] = o
        upd = lax.dot_general(k_out, vh, (((0,), (0,)), ((), ())), preferred_element_type=F32)
        r0, r1 = h * GLA_DK, (h + 1) * GLA_DK
        st[r0:r1, :] = st[r0:r1, :] * dcol[r0:r1, :] + upd[r0:r1, :]


def _gla_kernel(qf, kf, vf, laf, qb, kb, vb, lab, of_ref, ob_ref, sf, sb):
    s = pl.program_id(1)

    @pl.when(s == 0)
    def _():
        sf[...] = jnp.zeros_like(sf)
        sb[...] = jnp.zeros_like(sb)

    c = GLA_CHUNK
    ri = lax.broadcasted_iota(jnp.int32, (c, c), 0)
    ci = lax.broadcasted_iota(jnp.int32, (c, c), 1)
    _gla_dir(qf, kf, vf, laf, of_ref, sf, (ri >= ci).astype(F32), ri >= ci, c - 1)
    _gla_dir(qb, kb, vb, lab, ob_ref, sb, (ri <= ci).astype(F32), ci > ri, 0)


def _gla(gq, gk, gv, laf, lab, nbatch, nctx_chunks):
    m = gq.shape[0]
    c = GLA_CHUNK
    nch = m // nbatch // c
    hk, hv = GLA_HEADS * GLA_DK, GLA_HEADS * GLA_DV

    def fi(b, s):
        return (b * nch + s, 0)

    def bi(b, s):
        cb = jnp.where(s < nctx_chunks, nctx_chunks - 1 - s, nch - 1 + nctx_chunks - s)
        return (b * nch + cb, 0)

    sk = lambda im: pl.BlockSpec((c, hk), im)
    sv = lambda im: pl.BlockSpec((c, hv), im)
    return pl.pallas_call(
        _gla_kernel,
        grid=(nbatch, nch),
        in_specs=[sk(fi), sk(fi), sv(fi), sk(fi), sk(bi), sk(bi), sv(bi), sk(bi)],
        out_specs=[sv(fi), sv(bi)],
        out_shape=[jax.ShapeDtypeStruct((m, hv), F32)] * 2,
        scratch_shapes=[pltpu.VMEM((hk, GLA_DV), F32), pltpu.VMEM((hk, GLA_DV), F32)],
        compiler_params=_cparams(("arbitrary", "arbitrary")),
        name="gla_scan",
    )(gq, gk, gv, laf, gq, gk, gv, lab)


def _swa_kernel(q_ref, kc_ref, vc_ref, kp_ref, kcur_ref, kn_ref, vp_ref, vcur_ref, vn_ref, sink_ref,
                o_ref, *, nctx_blocks, nlat_blocks):
    j = pl.program_id(1)
    w = WINDOW
    n = j - nctx_blocks
    is_lat = j >= nctx_blocks
    nkc = kc_ref.shape[0]
    kall = jnp.concatenate([kc_ref[...], kp_ref[...], kcur_ref[...], kn_ref[...]], axis=0)
    vall = jnp.concatenate([vc_ref[...], vp_ref[...], vcur_ref[...], vn_ref[...]], axis=0)
    nk = kall.shape[0]
    lane = lax.broadcasted_iota(jnp.int32, kall.shape, 1)
    lo = lane < SWA_HEAD_DIM
    kroll = pltpu.roll(kall, SWA_HEAD_DIM, axis=1)
    vroll = pltpu.roll(vall, SWA_HEAD_DIM, axis=1)
    k2 = [jnp.where(lo, kall, kroll).astype(BF16), jnp.where(lo, kroll, kall).astype(BF16)]
    v2 = [jnp.where(lo, vall, vroll).astype(BF16), jnp.where(lo, vroll, vall).astype(BF16)]
    qi = lax.broadcasted_iota(jnp.int32, (w, nk), 0)
    kj = lax.broadcasted_iota(jnp.int32, (w, nk), 1)
    qpos = n * w + qi
    kpos = (n - 1) * w + (kj - nkc)
    loc_ok = (jnp.abs(qpos - kpos) <= WINDOW) & (kpos >= 0) & (kpos < nlat_blocks * w) & is_lat
    valid = (kj < nkc) | ((kj >= nkc) & loc_ok)
    qlane = lax.broadcasted_iota(jnp.int32, (w, 128), 1)
    qlo = qlane < SWA_HEAD_DIM
    rep = SWA_HEADS // SWA_KV_HEADS
    for pr in range(SWA_HEADS // 2):
        g = (2 * pr) // rep
        qp = q_ref[:, 128 * pr:128 * (pr + 1)]
        outs = []
        for half in range(2):
            h = 2 * pr + half
            qh = jnp.where(qlo if half == 0 else ~qlo, qp, 0.0).astype(BF16)
            s = lax.dot_general(qh, k2[g], (((1,), (1,)), ((), ())), preferred_element_type=F32)
            s = jnp.where(valid, s, NEG)
            sink = sink_ref[h:h + 1, 0:1]
            mx = jnp.maximum(jnp.max(s, axis=-1, keepdims=True), sink)
            p = jnp.exp(s - mx)
            den = jnp.sum(p, axis=-1, keepdims=True) + jnp.exp(sink - mx)
            o = jnp.dot(p.astype(BF16), v2[g], preferred_element_type=F32)
            outs.append(o / den)
        o_ref[:, 128 * pr:128 * (pr + 1)] = jnp.where(qlo, outs[0], outs[1])


def _swa(sq, sk, sv, sink, nbatch, nctx_blocks, nlat_blocks):
    m = sq.shape[0]
    w = WINDOW
    nblk = nctx_blocks + nlat_blocks
    nkc = nctx_blocks * w

    def qmap(b, j):
        return (b * nblk + j, 0)

    def cmap(b, j):
        return (b, 0)

    def nmap(off):
        def f(b, j):
            n = jnp.clip(j - nctx_blocks + off, 0, nlat_blocks - 1)
            return (b * nblk + nctx_blocks + n, 0)
        return f

    kvw = sk.shape[1]
    kcs = pl.BlockSpec((nkc, kvw), lambda b, j: (b * (nblk * w // nkc), 0))
    kvs = lambda off: pl.BlockSpec((w, kvw), nmap(off))
    return pl.pallas_call(
        functools.partial(_swa_kernel, nctx_blocks=nctx_blocks, nlat_blocks=nlat_blocks),
        grid=(nbatch, nblk),
        in_specs=[pl.BlockSpec((w, sq.shape[1]), qmap), kcs, kcs, kvs(-1), kvs(0), kvs(1),
                  kvs(-1), kvs(0), kvs(1), _full(sink.shape)],
        out_specs=pl.BlockSpec((w, sq.shape[1]), qmap),
        out_shape=jax.ShapeDtypeStruct(sq.shape, F32),
        compiler_params=_cparams(("arbitrary", "arbitrary")),
        name="swa_attn",
    )(sq, sk, sv, sk, sk, sk, sv, sv, sv, sink)


def _residual_epilogue(x, mix, mod_ref, n2_ref, xo_ref, f_ref, ft_ref, d):
    xn = x + _mod_rows(mod_ref, 2, d) * mix
    xo_ref[...] = xn
    f = _rms(xn) * n2_ref[...] * (1.0 + _mod_rows(mod_ref, 4, d)) + _mod_rows(mod_ref, 3, d)
    f_ref[...] = f
    ft_ref[...] = f.T.astype(BF16)


def _out0_kernel(x_ref, of_ref, ob_ref, gg_ref, a_ref, mod_ref, gn_ref, wo_ref, n2_ref,
                 xo_ref, f_ref, ft_ref, *, d):
    o = of_ref[...] + ob_ref[...]
    gate = gg_ref[...]
    gate = gate * _sigmoid(gate)
    parts = []
    for h in range(GLA_HEADS):
        oh = o[:, h * GLA_DV:(h + 1) * GLA_DV]
        parts.append(oh * lax.rsqrt(jnp.mean(oh * oh, axis=-1, keepdims=True) + NORM_EPS))
    on = jnp.concatenate(parts, axis=1) * gn_ref[...] * gate
    hv = GLA_HEADS * GLA_DV
    mix = _bdot(on, wo_ref[0:hv, :]) + _bdot(a_ref[...], wo_ref[hv:, :])
    _residual_epilogue(x_ref[...], mix, mod_ref, n2_ref, xo_ref, f_ref, ft_ref, d)


def _tok_spec(n):
    return pl.BlockSpec((TM, n), lambda i: (i, 0))


def _epilogue_specs(m, d):
    out_specs = [_tok_spec(d), _tok_spec(d), pl.BlockSpec((d, TM), lambda i: (0, i))]
    out_shape = [jax.ShapeDtypeStruct((m, d), F32), jax.ShapeDtypeStruct((m, d), F32),
                 jax.ShapeDtypeStruct((d, m), BF16)]
    return out_specs, out_shape


def _out0(xs, of, ob, gg, a, mod, gn, wo, n2, nb, nbatch):
    m, d = xs.shape
    out_specs, out_shape = _epilogue_specs(m, d)
    return pl.pallas_call(
        functools.partial(_out0_kernel, d=d),
        grid=(m // TM,),
        in_specs=[_tok_spec(d), _tok_spec(of.shape[1]), _tok_spec(ob.shape[1]), _tok_spec(gg.shape[1]),
                  _tok_spec(a.shape[1]), _mod_spec(mod, nb, nbatch), _full(gn.shape), _full(wo.shape),
                  _full(n2.shape)],
        out_specs=out_specs, out_shape=out_shape,
        compiler_params=_cparams(("arbitrary",), VMEM_LIMIT),
        name="out0",
    )(xs, of, ob, gg, a, mod, gn, wo, n2)


NO_RANK = 99.0


def _top_vals(s, k, want_rank=False):
    rows = lax.broadcasted_iota(jnp.int32, (k, s.shape[1]), 0)
    vals = jnp.zeros((k, s.shape[1]), F32)
    rank = jnp.full(s.shape, NO_RANK, F32)
    cur = s
    for t in range(k):
        mx = jnp.max(cur, axis=0, keepdims=True)
        vals = jnp.where(rows == t, mx, vals)
        hit = cur == mx
        if want_rank:
            rank = jnp.where(hit, float(t), rank)
        cur = jnp.where(hit, NEG, cur)
    return (vals, rank) if want_rank else vals


def _peer_q_kernel(f_ref, wh_ref, wl_ref, k1_ref, k2_ref, r2_ref, cnt_ref, e1_ref, e2_ref):
    f = f_ref[...]
    fh = f.astype(BF16)
    fl = (f - fh.astype(F32)).astype(BF16)
    kk = PEER_TOPK
    for h in range(PEER_HEADS):
        c0 = h * 2 * PEER_NKEYS
        wh = wh_ref[:, c0:c0 + 2 * PEER_NKEYS]
        wl = wl_ref[:, c0:c0 + 2 * PEER_NKEYS]
        q = (jnp.dot(fh, wh, preferred_element_type=F32) + jnp.dot(fl, wh, preferred_element_type=F32)
             + jnp.dot(fh, wl, preferred_element_type=F32))
        nt = (((1,), (1,)), ((), ()))
        s1 = lax.dot_general(k1_ref[h], q[:, :PEER_NKEYS], nt, precision=HI, preferred_element_type=F32)
        s2 = lax.dot_general(k2_ref[h], q[:, PEER_NKEYS:], nt, precision=HI, preferred_element_type=F32)
        v1 = _top_vals(s1, kk)
        v2, rank2 = _top_vals(s2, kk, want_rank=True)
        row8 = lax.broadcasted_iota(jnp.int32, (8, s1.shape[1]), 0)
        blocks = [v1[0:1, :] + v2]
        for r in range(1, 8):
            blocks.append(jnp.where(row8 < kk // (r + 1), v1[r:r + 1, :] + v2[0:8, :], NEG))
        blocks.append(v1[8:kk, :] + v2[0:1, :])
        top = _top_vals(jnp.concatenate(blocks, axis=0), kk)
        tau = top[kk - 1:kk, :]
        z = jnp.sum(jnp.exp(top - top[0:1, :]), axis=0, keepdims=True)
        cnt = jnp.zeros(s1.shape, F32)
        for c in range(kk):
            cnt = cnt + jnp.where(s1 + v2[c:c + 1, :] >= tau, 1.0, 0.0)
        r2_ref[h] = rank2.astype(BF16)
        cnt_ref[h] = cnt
        e1_ref[h] = jnp.exp(s1 - v1[0:1, :])
        e2_ref[h] = (jnp.exp(s2 - v2[0:1, :]) / z).astype(BF16)


def _peer_q(f, wh, wl, k1, k2):
    m, d = f.shape
    hh, nk = PEER_HEADS, PEER_NKEYS
    big = pl.BlockSpec((hh, nk, TM), lambda i: (0, 0, i))
    sh32 = jax.ShapeDtypeStruct((hh, nk, m), F32)
    sh16 = jax.ShapeDtypeStruct((hh, nk, m), BF16)
    return pl.pallas_call(
        _peer_q_kernel,
        grid=(m // TM,),
        in_specs=[_tok_spec(d), _full(wh.shape), _full(wl.shape), _full(k1.shape), _full(k2.shape)],
        out_specs=[big, big, big, big],
        out_shape=[sh16, sh32, sh32, sh16],
        compiler_params=_cparams(("arbitrary",), VMEM_LIMIT),
        name="peer_query",
    )(f, wh, wl, k1, k2)


PD_TM = 512
PD_TE = 2048
PD_SUB = 1024


def _gelu(x):
    return 0.5 * x * (1.0 + lax.erf(x * (2.0 ** -0.5)))


def _row_bf16(tile, r, rows):
    one = jnp.broadcast_to(tile[r:r + 1, :], (8, tile.shape[1]))
    one = jnp.concatenate([one, one], axis=0).astype(BF16)
    return jnp.concatenate([one] * (rows // 16), axis=0)


def _peer_dense_kernel(ft_ref, u_ref, vt_ref, r2_ref, cnt_ref, e1_ref, e2_ref, o_ref, acc_ref):
    k = pl.program_id(1)
    nk = PEER_NKEYS
    nslab = PD_TE // nk
    per_sub = PD_SUB // nk

    @pl.when(k == 0)
    def _():
        acc_ref[...] = jnp.zeros_like(acc_ref)

    zero = jnp.zeros((nk, PD_TM), BF16)
    nsub = PD_TE // PD_SUB

    def hidden(sb):
        return jnp.dot(u_ref[sb * PD_SUB:(sb + 1) * PD_SUB, :], ft_ref[...], preferred_element_type=F32)

    ht_next = hidden(0)
    for sb in range(nsub):
        e0 = sb * PD_SUB
        ht = ht_next
        if sb + 1 < nsub:
            ht_next = hidden(sb + 1)
        gates = []
        for al in range(sb * per_sub, (sb + 1) * per_sub):
            wg = zero
            a0 = pl.multiple_of(k * nslab + 8 * (al // 8), 8)
            for h in range(PEER_HEADS):
                cnt = _row_bf16(cnt_ref[h, pl.ds(a0, 8), :], al % 8, nk)
                e1 = _row_bf16(e1_ref[h, pl.ds(a0, 8), :], al % 8, nk)
                wg = wg + jnp.where(r2_ref[h] < cnt, e2_ref[h], zero) * e1
            gates.append(wg)
        ct = _gelu(ht).astype(BF16) * jnp.concatenate(gates, axis=0)
        acc_ref[...] += jnp.dot(vt_ref[:, e0:e0 + PD_SUB], ct, preferred_element_type=F32)

    @pl.when(k == pl.num_programs(1) - 1)
    def _():
        o_ref[...] = acc_ref[...].T


def _peer_dense(ft, u, vt, r2, cnt, e1, e2):
    d, m = ft.shape
    ne = u.shape[0]
    hh, nk = PEER_HEADS, PEER_NKEYS
    big = pl.BlockSpec((hh, nk, PD_TM), lambda i, k: (0, 0, i))
    return pl.pallas_call(
        _peer_dense_kernel,
        grid=(m // PD_TM, ne // PD_TE),
        in_specs=[pl.BlockSpec((d, PD_TM), lambda i, k: (0, i)),
                  pl.BlockSpec((PD_TE, d), lambda i, k: (k, 0)),
                  pl.BlockSpec((d, PD_TE), lambda i, k: (0, k)),
                  big, big, big, big],
        out_specs=pl.BlockSpec((PD_TM, d), lambda i, k: (i, 0)),
        out_shape=jax.ShapeDtypeStruct((m, d), F32),
        scratch_shapes=[pltpu.VMEM((d, PD_TM), F32)],
        compiler_params=_cparams(("arbitrary", "arbitrary"), VMEM_LIMIT),
        name="peer_dense",
    )(ft, u, vt, r2, cnt, e1, e2)


def _res_kernel(x_ref, p_ref, mod_ref, modn_ref, g_ref, xo_ref, u_ref, *, d, final):
    xn = x_ref[...] + _mod_rows(mod_ref, 5, d) * p_ref[...]
    xo_ref[...] = xn
    y = _rms(xn) * g_ref[...]
    if not final:
        y = y * (1.0 + _mod_rows(modn_ref, 1, d)) + _mod_rows(modn_ref, 0, d)
    u_ref[...] = y


def _res(xs, p, mod, modn, g, nb, nbatch, final):
    m, d = xs.shape
    return pl.pallas_call(
        functools.partial(_res_kernel, d=d, final=final),
        grid=(m // TM,),
        in_specs=[_tok_spec(d), _tok_spec(d), _mod_spec(mod, nb, nbatch), _mod_spec(modn, nb, nbatch),
                  _full(g.shape)],
        out_specs=[_tok_spec(d), _tok_spec(d)],
        out_shape=[jax.ShapeDtypeStruct((m, d), F32)] * 2,
        compiler_params=_cparams(("arbitrary",)),
        name="peer_residual",
    )(xs, p, mod, modn, g)


def _seg_sum(x, g_ref, gt_ref):
    return _hdot(_hdot(x, g_ref[...]), gt_ref[...])


def _rwkv_prep_kernel(u_ref, up_ref, un_ref, mu_ref, wr_ref, wk_ref, wv_ref, w0_ref, w1_ref, w2_ref,
                      a0_ref, a1_ref, a2_ref, g1_ref, g2_ref, kk_ref_, ka_ref, rk_ref, sg_ref, sgt_ref,
                      wf_o, wb_o, kf_o, kb_o, bf_o, bb_o, r_o, v_o, kk_o, g_o, bonus_o, *, nb, d):
    i = pl.program_id(0)
    p = i % nb
    u = u_ref[...]
    has_prev = (p >= 2).astype(F32)
    has_next = ((p >= 1) & (p <= nb - 2)).astype(F32)
    prev_row = up_ref[7:8, :] * has_prev
    next_row = un_ref[0:1, :] * has_next
    rows = lax.broadcasted_iota(jnp.int32, u.shape, 0)
    up = jnp.where(rows == 0, prev_row, pltpu.roll(u, 1, axis=0))
    un = jnp.where(rows == TM - 1, next_row, pltpu.roll(u, TM - 1, axis=0))
    xx = 0.5 * (up + un) - u

    def mix(zi):
        return (u + xx * mu_ref[zi:zi + 1, :]).astype(BF16)

    r = jnp.dot(mix(0), wr_ref[...], preferred_element_type=F32)
    k = jnp.dot(mix(2), wk_ref[...], preferred_element_type=F32)
    v = jnp.dot(mix(3), wv_ref[...], preferred_element_type=F32)
    lw = w0_ref[...] + _bdot(jnp.tanh(jnp.dot(mix(1), w1_ref[...], preferred_element_type=F32)), w2_ref[...])
    decay = jnp.exp(-_sigmoid(lw) * float(np.exp(-0.5)))
    a = _sigmoid(a0_ref[...] + _bdot(jnp.dot(mix(4), a1_ref[...], preferred_element_type=F32), a2_ref[...]))
    g = _bdot(_sigmoid(jnp.dot(mix(5), g1_ref[...], preferred_element_type=F32)), g2_ref[...])
    kk = k * kk_ref_[...]
    kk = kk * lax.rsqrt(jnp.maximum(_seg_sum(kk * kk, sg_ref, sgt_ref), 1e-24))
    ka = ka_ref[...]
    kf = k * (1.0 + (a[:, :d] - 1.0) * ka)
    kb = k * (1.0 + (a[:, d:] - 1.0) * ka)
    bonus = _seg_sum(r * (kf + kb) * rk_ref[...], sg_ref, sgt_ref) * v
    wf_o[...] = decay[:, :d]
    wb_o[...] = decay[:, d:]
    kf_o[...] = kf
    kb_o[...] = kb
    bf_o[...] = kk * a[:, :d]
    bb_o[...] = kk * a[:, d:]
    r_o[...] = r
    v_o[...] = v
    kk_o[...] = kk
    g_o[...] = g
    bonus_o[...] = bonus


def _rwkv_prep(u, consts, nb):
    m, d = u.shape
    r8 = TM // 8
    nb8 = m // 8
    in_specs = [_tok_spec(d),
                pl.BlockSpec((8, d), lambda i: (jnp.maximum(i * r8 - 1, 0), 0)),
                pl.BlockSpec((8, d), lambda i: (jnp.minimum((i + 1) * r8, nb8 - 1), 0))]
    in_specs += [_full(c.shape) for c in consts]
    return pl.pallas_call(
        functools.partial(_rwkv_prep_kernel, nb=nb, d=d),
        grid=(m // TM,),
        in_specs=in_specs,
        out_specs=[_tok_spec(d)] * 11,
        out_shape=[jax.ShapeDtypeStruct((m, d), F32)] * 11,
        compiler_params=_cparams(("arbitrary",), VMEM_LIMIT),
        name="rwkv_prep",
    )(u, u, u, *consts)


RW_TC = 256


def _rwkv_scan_kernel(wf, kf, bf, rf, vf, kkf, wb, kb, bb, rb, vb, kkb, yf_ref, yb_ref, st, *, npair):
    s = pl.program_id(1)
    n = RW_HEAD_DIM
    grp = 8

    @pl.when(s == 0)
    def _():
        st[...] = jnp.zeros_like(st)

    w2 = 2 * n
    ri = lax.broadcasted_iota(jnp.int32, (n, w2), 0)
    li = lax.broadcasted_iota(jnp.int32, (n, w2), 1)
    eye_a = (li == ri).astype(BF16)
    bi = lax.broadcasted_iota(jnp.int32, (w2, w2), 0)
    bj = lax.broadcasted_iota(jnp.int32, (w2, w2), 1)
    ones_blk = ((bi // n) == (bj // n)).astype(BF16)
    b4i = lax.broadcasted_iota(jnp.int32, (2 * w2, 2 * w2), 0)
    b4j = lax.broadcasted_iota(jnp.int32, (2 * w2, 2 * w2), 1)
    ones_blk2 = ((b4i // n) == (b4j // n)).astype(BF16)
    lane8 = lax.broadcasted_iota(jnp.int32, (grp, w2), 1)
    rows16 = lax.broadcasted_iota(jnp.int32, (2 * grp, w2), 0)
    lane16 = lax.broadcasted_iota(jnp.int32, (2 * grp, w2), 1)
    zpad = jnp.zeros((w2 - 2 * grp, w2), F32)
    untr = ((lane16 == rows16) | (lane16 == rows16 + n)).astype(BF16)
    dirs = ((wf, kf, bf, rf, vf, kkf, yf_ref), (wb, kb, bb, rb, vb, kkb, yb_ref))
    nch = 2 * npair
    ngrp = RW_TC // grp
    nt = (((1,), (1,)), ((), ()))

    def halves(tile):
        return jnp.concatenate([jnp.where(lane8 < n, tile, 0.0), jnp.where(lane8 >= n, tile, 0.0)], axis=0)

    def group(ig, carry):
        bases = (pl.multiple_of(ig * grp, grp), pl.multiple_of((ngrp - 1 - ig) * grp, grp))
        tiles = []
        for c in range(nch):
            dd, p = divmod(c, npair)
            ln = slice(w2 * p, w2 * (p + 1))
            tiles.append([ref[pl.ds(bases[dd], grp), ln] for ref in dirs[dd][:6]])
        v2 = jnp.concatenate(
            [jnp.concatenate([jnp.where(lane8 < n, t[4], 0.0),
                              jnp.where(lane8 < n, pltpu.roll(t[4], n, axis=1), 0.0)], axis=0) for t in tiles],
            axis=0)
        vt_all = lax.dot_general(eye_a, v2.astype(BF16), nt, preferred_element_type=F32).astype(BF16)
        pre = []
        for c in range(nch):
            w_t, k_t, b_t, r_t, v_t, kk_t = tiles[c]
            k2 = halves(k_t)
            kblk = jnp.concatenate(
                [jnp.where((rows16 == j) | (rows16 == grp + j), k2, 0.0) for j in range(grp)], axis=1)
            vt = vt_all[:, 2 * grp * c:2 * grp * (c + 1)]
            vk = jnp.dot(vt, kblk.astype(BF16), preferred_element_type=F32)
            pre.append((w_t, b_t, kk_t, r_t, vk))
        states = [st[c] for c in range(nch)]
        ycol = [jnp.zeros((n, w2), F32) for _ in range(nch)]
        zero_half = jnp.zeros((n, w2), BF16)
        srp = [zero_half] * nch
        for j in range(grp + 1):
            lhs = []
            for c in range(nch):
                kk_t = pre[c][2]
                jj = min(j, grp - 1) if c < npair else max(grp - 1 - j, 0)
                sk = (states[c] * jnp.broadcast_to(kk_t[jj:jj + 1, :], (n, w2))).astype(BF16)
                lhs.append(jnp.concatenate([sk, srp[c]], axis=1))
            red = jnp.dot(jnp.concatenate(lhs, axis=0), ones_blk2, preferred_element_type=F32)
            for c in range(nch):
                w_t, b_t, kk_t, r_t, vk = pre[c]
                jj = j if c < npair else grp - 1 - j
                jp = j - 1 if c < npair else grp - j
                row = lambda tl: jnp.broadcast_to(tl[jj:jj + 1, :], (n, w2))
                sa = red[c * n:(c + 1) * n, :w2]
                yb = red[c * n:(c + 1) * n, w2:]
                if j > 0:
                    ycol[c] = jnp.where((li % n) == jp, yb, ycol[c])
                if j < grp:
                    states[c] = states[c] * row(w_t) - sa * row(b_t) + vk[:, jj * w2:(jj + 1) * w2]
                    srp[c] = (states[c] * row(r_t)).astype(BF16)
        for c in range(nch):
            dd, p = divmod(c, npair)
            st[c] = states[c]
            ycs = jnp.concatenate([jnp.where(li < n, ycol[c], 0.0), jnp.where(li >= n, ycol[c], 0.0)], axis=0)
            yhi = ycs.astype(BF16)
            ylo = (ycs - yhi.astype(F32)).astype(BF16)
            yt = (lax.dot_general(untr, yhi, nt, preferred_element_type=F32)
                  + lax.dot_general(untr, ylo, nt, preferred_element_type=F32))
            dirs[dd][6][pl.ds(bases[dd], grp), w2 * p:w2 * (p + 1)] = yt[0:grp, :]
        return carry

    lax.fori_loop(0, ngrp, group, 0)


def _rwkv_scan(wf, kf, bf, wb, kb, bb, r, v, kk, nbatch):
    m, d = r.shape
    nch = m // nbatch // RW_TC
    npair = d // (2 * RW_HEAD_DIM)

    def fi(b, s):
        return (b * nch + s, 0)

    def bi(b, s):
        return (b * nch + jnp.where(s == 0, 0, nch - s), 0)

    sf = pl.BlockSpec((RW_TC, d), fi)
    sb = pl.BlockSpec((RW_TC, d), bi)
    return pl.pallas_call(
        functools.partial(_rwkv_scan_kernel, npair=npair),
        grid=(nbatch, nch),
        in_specs=[sf] * 6 + [sb] * 6,
        out_specs=[sf, sb],
        out_shape=[jax.ShapeDtypeStruct((m, d), F32)] * 2,
        scratch_shapes=[pltpu.VMEM((2 * npair, RW_HEAD_DIM, 2 * RW_HEAD_DIM), F32)],
        compiler_params=_cparams(("arbitrary", "arbitrary"), VMEM_LIMIT),
        name="rwkv_scan",
    )(wf, kf, bf, r, v, kk, wb, kb, bb, r, v, kk)


def _rwkv_out_kernel(x_ref, yf_ref, yb_ref, bonus_ref, g_ref, mod_ref, lg_ref, lb_ref, wo_ref, n2_ref,
                     sg_ref, sgt_ref, xo_ref, f_ref, ft_ref, *, d):
    y = yf_ref[...] + yb_ref[...]
    inv = 1.0 / RW_HEAD_DIM
    mean = _seg_sum(y, sg_ref, sgt_ref) * inv
    yc = y - mean
    var = _seg_sum(yc * yc, sg_ref, sgt_ref) * inv
    yn = yc * lax.rsqrt(var + RW_GN_EPS) * lg_ref[...] + lb_ref[...]
    out = (yn + bonus_ref[...]) * g_ref[...]
    mix = _bdot(out, wo_ref[...])
    _residual_epilogue(x_ref[...], mix, mod_ref, n2_ref, xo_ref, f_ref, ft_ref, d)


def _rwkv_out(xs, yf, yb, bonus, g, mod, lg, lb, wo, n2, sg, sgt, nb, nbatch):
    m, d = xs.shape
    out_specs, out_shape = _epilogue_specs(m, d)
    return pl.pallas_call(
        functools.partial(_rwkv_out_kernel, d=d),
        grid=(m // TM,),
        in_specs=[_tok_spec(d)] * 5 + [_mod_spec(mod, nb, nbatch)]
        + [_full(a.shape) for a in (lg, lb, wo, n2, sg, sgt)],
        out_specs=out_specs, out_shape=out_shape,
        compiler_params=_cparams(("arbitrary",), VMEM_LIMIT),
        name="rwkv_out",
    )(xs, yf, yb, bonus, g, mod, lg, lb, wo, n2, sg, sgt)


def _rope_tables(t_lat, n_ctx):
    rows = t_lat // GRID_W
    row = jnp.broadcast_to(jnp.arange(rows, dtype=F32)[:, None], (rows, GRID_W)).reshape(-1)
    col = jnp.broadcast_to(jnp.arange(GRID_W, dtype=F32)[None, :], (rows, GRID_W)).reshape(-1)
    n_freq = SWA_HEAD_DIM // 4
    inv_freq = ROPE_BASE ** (-jnp.arange(n_freq, dtype=F32) / n_freq)
    ar = row[:, None] * inv_freq
    ac = col[:, None] * inv_freq
    cos = jnp.concatenate([jnp.cos(ar), jnp.cos(ar), jnp.cos(ac), jnp.cos(ac)], axis=1)
    sin = jnp.concatenate([-jnp.sin(ar), jnp.sin(ar), -jnp.sin(ac), jnp.sin(ac)], axis=1)
    cos = jnp.concatenate([jnp.ones((n_ctx, SWA_HEAD_DIM), F32), cos], axis=0)
    sin = jnp.concatenate([jnp.zeros((n_ctx, SWA_HEAD_DIM), F32), sin], axis=0)
    return jnp.tile(cos, (1, 2)), jnp.tile(sin, (1, 2))


def _peer(f, ft, w_q, k1, k2, u, v):
    wh = w_q.astype(BF16)
    wl = (w_q - wh.astype(F32)).astype(BF16)
    r2, cnt, e1, e2 = _peer_q(f, wh, wl, k1, k2)
    return _peer_dense(ft, u.astype(BF16), v.T.astype(BF16), r2, cnt, e1, e2)


def kernel(x, c, ctx, c_ctx, ada_w, ada_b, norm1_g, norm2_g, ab_w_in, gla_dec_w2, gla_dec_b, gla_norm_g,
           swa_sink, ab_w_out, rw_mu, rw_w_rkv, rw_w_o, rw_w0, rw_w1, rw_w2, rw_a0, rw_a1, rw_a2, rw_g1,
           rw_g2, rw_k_k, rw_k_a, rw_r_k, rw_ln_g, rw_ln_b, peer_w_q, peer_k1, peer_k2, peer_u, peer_v,
           final_g):
    nbatch, t_lat, d = x.shape
    n_ctx = ctx.shape[1]
    assert n_ctx == TM and t_lat % TM == 0 and d == 1024
    s_tok = n_ctx + t_lat
    nb = s_tok // TM
    m = nbatch * s_tok
    xs = jnp.concatenate([ctx, x], axis=1).reshape(m, d)

    cc = jnp.zeros((16, d), F32).at[:nbatch].set(c).at[nbatch].set(c_ctx)
    mod0 = _ada_table(cc, ada_w[0], ada_b[0]).reshape(16, 1, 6 * d)
    mod1 = _ada_table(cc, ada_w[1], ada_b[1]).reshape(16, 1, 6 * d)
    row2 = lambda a: a.reshape(1, -1)

    w_in = ab_w_in[0]
    cuts = np.cumsum((256, 256, 512, 512, 32, 512, 128, 128))[:-1].tolist()
    wgq, wgk, wgv, wgg, wlr, wsq, wsk, wsv = jnp.split(w_in, cuts, axis=1)
    wp = jnp.concatenate([wgq, wgk, wgv, wgg, wsq, wsk, wsv, wlr,
                          jnp.zeros((d, _P_END - _P_LR - 32), F32)], axis=1).astype(BF16)
    hk = GLA_HEADS * GLA_DK
    w2p = jnp.zeros((128, 2 * hk), F32)
    w2p = w2p.at[0:GLA_LOWRANK, 0:hk].set(gla_dec_w2[0, 0]).at[GLA_LOWRANK:2 * GLA_LOWRANK, hk:].set(gla_dec_w2[0, 1])
    db = gla_dec_b[0].reshape(1, 2 * hk)
    cos, sin = _rope_tables(t_lat, n_ctx)
    gq, gk, gv, gg, laf, lab, sq, sk, sv = _proj0(xs, mod0, row2(norm1_g[0]), wp, w2p, db, cos, sin, nb, nbatch)
    of, ob = _gla(gq, gk, gv, laf, lab, nbatch, n_ctx // GLA_CHUNK)
    sink = jnp.broadcast_to(swa_sink[0][:, None], (SWA_HEADS, 128))
    att = _swa(sq, sk, sv, sink, nbatch, n_ctx // WINDOW, t_lat // WINDOW)
    xs, f, ft = _out0(xs, of, ob, gg, att, mod0, row2(gla_norm_g[0]), ab_w_out[0].astype(BF16),
                      row2(norm2_g[0]), nb, nbatch)
    p = _peer(f, ft, peer_w_q[0], peer_k1[0], peer_k2[0], peer_u[0], peer_v[0])
    xs, u = _res(xs, p, mod0, mod1, row2(norm1_g[1]), nb, nbatch, final=False)

    lora = rw_w1.shape[-1]
    w1c = jnp.concatenate([rw_w1[0, 0], rw_w1[0, 1]], axis=1).astype(BF16)
    w2c = jnp.zeros((2 * lora, 2 * d), F32).at[:lora, :d].set(rw_w2[0, 0]).at[lora:, d:].set(rw_w2[0, 1]).astype(BF16)
    la = rw_a1.shape[-1]
    a1c = jnp.concatenate([rw_a1[0, 0], rw_a1[0, 1]], axis=1).astype(BF16)
    a2c = jnp.zeros((2 * la, 2 * d), F32).at[:la, :d].set(rw_a2[0, 0]).at[la:, d:].set(rw_a2[0, 1]).astype(BF16)
    lg = rw_g1.shape[-1]
    g1p = jnp.zeros((d, 256), F32).at[:, :lg].set(rw_g1[0]).astype(BF16)
    g2p = jnp.zeros((256, d), F32).at[:lg].set(rw_g2[0]).astype(BF16)
    nheads = d // RW_HEAD_DIM
    sg = (jnp.arange(d)[:, None] // RW_HEAD_DIM == jnp.arange(128)[None, :]).astype(F32)
    sgt = sg.T
    consts = [rw_mu[0], rw_w_rkv[0, 0].astype(BF16), rw_w_rkv[0, 1].astype(BF16), rw_w_rkv[0, 2].astype(BF16),
              rw_w0[0].reshape(1, 2 * d), w1c, w2c, rw_a0[0].reshape(1, 2 * d), a1c, a2c, g1p, g2p,
              row2(rw_k_k[0]), row2(rw_k_a[0]), rw_r_k[0].reshape(1, d), sg, sgt]
    mu8 = jnp.zeros((8, d), F32).at[:6].set(rw_mu[0])
    consts[0] = mu8
    wf, wb, kf, kb, bf, bb, r, v, kk, g, bonus = _rwkv_prep(u, consts, nb)
    yf, yb = _rwkv_scan(wf, kf, bf, wb, kb, bb, r, v, kk, nbatch)
    xs, f, ft = _rwkv_out(xs, yf, yb, bonus, g, mod1, row2(rw_ln_g[0]), row2(rw_ln_b[0]),
                          rw_w_o[0].astype(BF16), row2(norm2_g[1]), sg, sgt, nb, nbatch)
    p = _peer(f, ft, peer_w_q[1], peer_k1[1], peer_k2[1], peer_u[1], peer_v[1])
    _, y = _res(xs, p, mod1, mod1, row2(final_g), nb, nbatch, final=True)
    return y.reshape(nbatch, s_tok, d)[:, n_ctx:]
```

```python
import functools

import numpy as np
import jax
import jax.numpy as jnp
from jax import lax
from jax.experimental import pallas as pl
from jax.experimental.pallas import tpu as pltpu

F32 = jnp.float32
BF16 = jnp.bfloat16
HI = lax.Precision.HIGHEST

NORM_EPS = 1e-6
GLA_HEADS, GLA_DK, GLA_DV, GLA_LOWRANK, GLA_TAU, GLA_CHUNK = 4, 64, 128, 16, 16.0, 64
SWA_HEADS, SWA_KV_HEADS, SWA_HEAD_DIM, WINDOW = 8, 2, 64, 128
ROPE_BASE = 10000.0
GRID_W = 64
RW_HEAD_DIM = 64
RW_GN_EPS = 64e-5
PEER_HEADS, PEER_NKEYS, PEER_TOPK = 8, 128, 16
NEG = -1e30

TM = 256
VMEM_LIMIT = 56 * 1024 * 1024


def _cparams(sem, vmem=None):
    return pltpu.CompilerParams(dimension_semantics=sem, vmem_limit_bytes=vmem)


def _bdot(a, b):
    return jnp.dot(a.astype(BF16), b.astype(BF16), preferred_element_type=F32)


def _hdot(a, b):
    return jnp.dot(a, b, precision=HI, preferred_element_type=F32)


def _sigmoid(x):
    return 1.0 / (1.0 + jnp.exp(-x))


def _rms(x):
    return x * lax.rsqrt(jnp.mean(x * x, axis=-1, keepdims=True) + NORM_EPS)


def _mod_rows(mod_ref, k, d):
    return mod_ref[0, :, k * d:(k + 1) * d]


def _mod_spec(mod, nb, nbatch):
    return pl.BlockSpec((1, 1, mod.shape[2]), lambda i: (jnp.where(i % nb == 0, nbatch, i // nb), 0, 0))


def _mod_spec_lat(mod, nbl):
    return pl.BlockSpec((1, 1, mod.shape[2]), lambda i: (i // nbl, 0, 0))


def _lat_tok_spec(n, nb):
    return pl.BlockSpec((TM, n), lambda i: ((i // (nb - 1)) * nb + 1 + i % (nb - 1), 0))


def _full(shape):
    n = len(shape)
    return pl.BlockSpec(shape, lambda *_: (0,) * n)


def _ada_kernel(c_ref, w_ref, b_ref, o_ref):
    c = c_ref[...]
    s = c * _sigmoid(c)
    o_ref[...] = _hdot(s, w_ref[...]) + b_ref[...]


def _ada_table(cc, w, b):
    rows, d = cc.shape
    n = w.shape[1]
    tn = 512
    return pl.pallas_call(
        _ada_kernel,
        grid=(n // tn,),
        in_specs=[_full((rows, d)), pl.BlockSpec((d, tn), lambda j: (0, j)),
                  pl.BlockSpec((1, tn), lambda j: (0, j))],
        out_specs=pl.BlockSpec((rows, tn), lambda j: (0, j)),
        out_shape=jax.ShapeDtypeStruct((rows, n), F32),
        compiler_params=_cparams(("arbitrary",)),
        name="ada_table",
    )(cc, w, b.reshape(1, n))


_P_GQ, _P_GK, _P_GV, _P_GG, _P_SQ, _P_SK, _P_SV, _P_LR, _P_END = 0, 256, 512, 1024, 1536, 2048, 2176, 2304, 2432


def _rope(x, cos, sin):
    lane = lax.broadcasted_iota(jnp.int32, x.shape, 1)
    up = pltpu.roll(x, 112, axis=1)
    dn = pltpu.roll(x, 16, axis=1)
    sw = jnp.where((lane % 32) < 16, up, dn)
    return x * cos + sw * sin


def _proj0_kernel(x_ref, mod_ref, g_ref, w_ref, w2_ref, db_ref, cos_ref, sin_ref,
                  gq_ref, gk_ref, gv_ref, gg_ref, laf_ref, lab_ref, sq_ref, sk_ref, sv_ref,
                  *, d):
    u = (_rms(x_ref[...]) * g_ref[...] * (1.0 + _mod_rows(mod_ref, 1, d))
         + _mod_rows(mod_ref, 0, d)).astype(BF16)

    def seg(a, b):
        return jnp.dot(u, w_ref[:, a:b], preferred_element_type=F32)

    gq_ref[...] = seg(_P_GQ, _P_GK) * (GLA_DK ** -0.5)
    gk_ref[...] = seg(_P_GK, _P_GV)
    gv_ref[...] = seg(_P_GV, _P_GG)
    gg_ref[...] = seg(_P_GG, _P_SQ)
    lr = seg(_P_LR, _P_END)
    z = _hdot(lr, w2_ref[...]) + db_ref[...]
    la = (jnp.minimum(z, 0.0) - jnp.log(1.0 + jnp.exp(-jnp.abs(z)))) * (1.0 / GLA_TAU)
    hk = GLA_HEADS * GLA_DK
    laf_ref[...] = la[:, :hk]
    lab_ref[...] = la[:, hk:]
    cos = cos_ref[...]
    sin = sin_ref[...]
    sq = seg(_P_SQ, _P_SK) * (SWA_HEAD_DIM ** -0.5)
    for c in range(4):
        sq_ref[:, 128 * c:128 * (c + 1)] = _rope(sq[:, 128 * c:128 * (c + 1)], cos, sin)
    sk_ref[...] = _rope(seg(_P_SK, _P_SV), cos, sin)
    sv_ref[...] = seg(_P_SV, _P_LR)


def _proj0(xs, mod, g, w, w2, db, cos, sin, nb, nbatch):
    m, d = xs.shape
    widths = (256, 256, 512, 512, 256, 256, 512, 128, 128)
    return pl.pallas_call(
        functools.partial(_proj0_kernel, d=d),
        grid=(m // TM,),
        in_specs=[pl.BlockSpec((TM, d), lambda i: (i, 0)), _mod_spec(mod, nb, nbatch), _full((1, d)),
                  _full(w.shape), _full(w2.shape), _full(db.shape),
                  pl.BlockSpec((TM, 128), lambda i: (i % nb, 0)),
                  pl.BlockSpec((TM, 128), lambda i: (i % nb, 0))],
        out_specs=[pl.BlockSpec((TM, n), lambda i: (i, 0)) for n in widths],
        out_shape=[jax.ShapeDtypeStruct((m, n), F32) for n in widths],
        compiler_params=_cparams(("arbitrary",), VMEM_LIMIT),
        name="proj0",
    )(xs, mod, g, w, w2, db, cos, sin)


def _gla_dir(q_ref, k_ref, v_ref, la_ref, o_ref, st, cum, mask, tot_row):
    c = GLA_CHUNK
    la = la_ref[...]
    bc = _hdot(cum, la)
    tot = bc[tot_row:tot_row + 1, :]
    q_in = q_ref[...] * jnp.exp(bc)
    k_in = (k_ref[...] * jnp.exp(-bc)).astype(BF16)
    k_out = (k_ref[...] * jnp.exp(tot - bc)).astype(BF16)
    ones = jnp.ones((c, GLA_DV), F32)
    dcol = jnp.exp(lax.dot_general(la, ones, (((0,), (0,)), ((), ())), precision=HI,
                                   preferred_element_type=F32))
    lane = lax.broadcasted_iota(jnp.int32, q_in.shape, 1)
    s_prev = st[...].astype(BF16)
    for h in range(GLA_HEADS):
        qh = jnp.where((lane // GLA_DK) == h, q_in, 0.0).astype(BF16)
        att = lax.dot_general(qh, k_in, (((1,), (1,)), ((), ())), preferred_element_type=F32)
        att = jnp.where(mask, att, 0.0).astype(BF16)
        vh = v_ref[:, h * GLA_DV:(h + 1) * GLA_DV].astype(BF16)
        o = (jnp.dot(att, vh, preferred_element_type=F32)
             + jnp.dot(qh, s_prev, preferred_element_type=F32))
        o_ref[:, h * GLA_DV:(h + 1) * GLA_DV] = o
        upd = lax.dot_general(k_out, vh, (((0,), (0,)), ((), ())), preferred_element_type=F32)
        r0, r1 = h * GLA_DK, (h + 1) * GLA_DK
        st[r0:r1, :] = st[r0:r1, :] * dcol[r0:r1, :] + upd[r0:r1, :]


def _gla_kernel(qf, kf, vf, laf, qb, kb, vb, lab, of_ref, ob_ref, sf, sb):
    s = pl.program_id(1)

    @pl.when(s == 0)
    def _():
        sf[...] = jnp.zeros_like(sf)
        sb[...] = jnp.zeros_like(sb)

    c = GLA_CHUNK
    ri = lax.broadcasted_iota(jnp.int32, (c, c), 0)
    ci = lax.broadcasted_iota(jnp.int32, (c, c), 1)
    _gla_dir(qf, kf, vf, laf, of_ref, sf, (ri >= ci).astype(F32), ri >= ci, c - 1)
    _gla_dir(qb, kb, vb, lab, ob_ref, sb, (ri <= ci).astype(F32), ci > ri, 0)


def _gla(gq, gk, gv, laf, lab, nbatch, nctx_chunks):
    m = gq.shape[0]
    c = GLA_CHUNK
    nch = m // nbatch // c
    hk, hv = GLA_HEADS * GLA_DK, GLA_HEADS * GLA_DV

    def fi(b, s):
        return (b * nch + s, 0)

    def bi(b, s):
        cb = jnp.where(s < nctx_chunks, nctx_chunks - 1 - s, nch - 1 + nctx_chunks - s)
        return (b * nch + cb, 0)

    sk = lambda im: pl.BlockSpec((c, hk), im)
    sv = lambda im: pl.BlockSpec((c, hv), im)
    return pl.pallas_call(
        _gla_kernel,
        grid=(nbatch, nch),
        in_specs=[sk(fi), sk(fi), sv(fi), sk(fi), sk(bi), sk(bi), sv(bi), sk(bi)],
        out_specs=[sv(fi), sv(bi)],
        out_shape=[jax.ShapeDtypeStruct((m, hv), F32)] * 2,
        scratch_shapes=[pltpu.VMEM((hk, GLA_DV), F32), pltpu.VMEM((hk, GLA_DV), F32)],
        compiler_params=_cparams(("arbitrary", "arbitrary")),
        name="gla_scan",
    )(gq, gk, gv, laf, gq, gk, gv, lab)


def _swa_kernel(q_ref, kc_ref, vc_ref, kp_ref, kcur_ref, kn_ref, vp_ref, vcur_ref, vn_ref, sink_ref,
                o_ref, *, nctx_blocks, nlat_blocks):
    j = pl.program_id(1)
    w = WINDOW
    n = j - nctx_blocks
    is_lat = j >= nctx_blocks
    nkc = kc_ref.shape[0]
    kall = jnp.concatenate([kc_ref[...], kp_ref[...], kcur_ref[...], kn_ref[...]], axis=0)
    vall = jnp.concatenate([vc_ref[...], vp_ref[...], vcur_ref[...], vn_ref[...]], axis=0)
    nk = kall.shape[0]
    lane = lax.broadcasted_iota(jnp.int32, kall.shape, 1)
    lo = lane < SWA_HEAD_DIM
    kroll = pltpu.roll(kall, SWA_HEAD_DIM, axis=1)
    vroll = pltpu.roll(vall, SWA_HEAD_DIM, axis=1)
    k2 = [jnp.where(lo, kall, kroll).astype(BF16), jnp.where(lo, kroll, kall).astype(BF16)]
    v2 = [jnp.where(lo, vall, vroll).astype(BF16), jnp.where(lo, vroll, vall).astype(BF16)]
    qi = lax.broadcasted_iota(jnp.int32, (w, nk), 0)
    kj = lax.broadcasted_iota(jnp.int32, (w, nk), 1)
    qpos = n * w + qi
    kpos = (n - 1) * w + (kj - nkc)
    loc_ok = (jnp.abs(qpos - kpos) <= WINDOW) & (kpos >= 0) & (kpos < nlat_blocks * w) & is_lat
    valid = (kj < nkc) | ((kj >= nkc) & loc_ok)
    qlane = lax.broadcasted_iota(jnp.int32, (w, 128), 1)
    qlo = qlane < SWA_HEAD_DIM
    rep = SWA_HEADS // SWA_KV_HEADS
    for pr in range(SWA_HEADS // 2):
        g = (2 * pr) // rep
        qp = q_ref[:, 128 * pr:128 * (pr + 1)]
        outs = []
        for half in range(2):
            h = 2 * pr + half
            qh = jnp.where(qlo if half == 0 else ~qlo, qp, 0.0).astype(BF16)
            s = lax.dot_general(qh, k2[g], (((1,), (1,)), ((), ())), preferred_element_type=F32)
            s = jnp.where(valid, s, NEG)
            sink = sink_ref[h:h + 1, 0:1]
            mx = jnp.maximum(jnp.max(s, axis=-1, keepdims=True), sink)
            p = jnp.exp(s - mx)
            den = jnp.sum(p, axis=-1, keepdims=True) + jnp.exp(sink - mx)
            o = jnp.dot(p.astype(BF16), v2[g], preferred_element_type=F32)
            outs.append(o / den)
        o_ref[:, 128 * pr:128 * (pr + 1)] = jnp.where(qlo, outs[0], outs[1])


def _swa(sq, sk, sv, sink, nbatch, nctx_blocks, nlat_blocks):
    m = sq.shape[0]
    w = WINDOW
    nblk = nctx_blocks + nlat_blocks
    nkc = nctx_blocks * w

    def qmap(b, j):
        return (b * nblk + j, 0)

    def cmap(b, j):
        return (b, 0)

    def nmap(off):
        def f(b, j):
            n = jnp.clip(j - nctx_blocks + off, 0, nlat_blocks - 1)
            return (b * nblk + nctx_blocks + n, 0)
        return f

    kvw = sk.shape[1]
    kcs = pl.BlockSpec((nkc, kvw), lambda b, j: (b * (nblk * w // nkc), 0))
    kvs = lambda off: pl.BlockSpec((w, kvw), nmap(off))
    return pl.pallas_call(
        functools.partial(_swa_kernel, nctx_blocks=nctx_blocks, nlat_blocks=nlat_blocks),
        grid=(nbatch, nblk),
        in_specs=[pl.BlockSpec((w, sq.shape[1]), qmap), kcs, kcs, kvs(-1), kvs(0), kvs(1),
                  kvs(-1), kvs(0), kvs(1), _full(sink.shape)],
        out_specs=pl.BlockSpec((w, sq.shape[1]), qmap),
        out_shape=jax.ShapeDtypeStruct(sq.shape, F32),
        compiler_params=_cparams(("arbitrary", "arbitrary")),
        name="swa_attn",
    )(sq, sk, sv, sk, sk, sk, sv, sv, sv, sink)


def _residual_epilogue(x, mix, mod_ref, n2_ref, xo_ref, f_ref, ft_ref, d):
    xn = x + _mod_rows(mod_ref, 2, d) * mix
    xo_ref[...] = xn
    f = _rms(xn) * n2_ref[...] * (1.0 + _mod_rows(mod_ref, 4, d)) + _mod_rows(mod_ref, 3, d)
    f_ref[...] = f
    ft_ref[...] = f.T.astype(BF16)


def _out0_kernel(x_ref, of_ref, ob_ref, gg_ref, a_ref, mod_ref, gn_ref, wo_ref, n2_ref,
                 xo_ref, f_ref, ft_ref, *, d):
    o = of_ref[...] + ob_ref[...]
    gate = gg_ref[...]
    gate = gate * _sigmoid(gate)
    parts = []
    for h in range(GLA_HEADS):
        oh = o[:, h * GLA_DV:(h + 1) * GLA_DV]
        parts.append(oh * lax.rsqrt(jnp.mean(oh * oh, axis=-1, keepdims=True) + NORM_EPS))
    on = jnp.concatenate(parts, axis=1) * gn_ref[...] * gate
    hv = GLA_HEADS * GLA_DV
    mix = _bdot(on, wo_ref[0:hv, :]) + _bdot(a_ref[...], wo_ref[hv:, :])
    _residual_epilogue(x_ref[...], mix, mod_ref, n2_ref, xo_ref, f_ref, ft_ref, d)


def _tok_spec(n):
    return pl.BlockSpec((TM, n), lambda i: (i, 0))


def _epilogue_specs(m, d):
    out_specs = [_tok_spec(d), _tok_spec(d), pl.BlockSpec((d, TM), lambda i: (0, i))]
    out_shape = [jax.ShapeDtypeStruct((m, d), F32), jax.ShapeDtypeStruct((m, d), F32),
                 jax.ShapeDtypeStruct((d, m), BF16)]
    return out_specs, out_shape


def _out0(xs, of, ob, gg, a, mod, gn, wo, n2, nb, nbatch):
    m, d = xs.shape
    out_specs, out_shape = _epilogue_specs(m, d)
    return pl.pallas_call(
        functools.partial(_out0_kernel, d=d),
        grid=(m // TM,),
        in_specs=[_tok_spec(d), _tok_spec(of.shape[1]), _tok_spec(ob.shape[1]), _tok_spec(gg.shape[1]),
                  _tok_spec(a.shape[1]), _mod_spec(mod, nb, nbatch), _full(gn.shape), _full(wo.shape),
                  _full(n2.shape)],
        out_specs=out_specs, out_shape=out_shape,
        compiler_params=_cparams(("arbitrary",), VMEM_LIMIT),
        name="out0",
    )(xs, of, ob, gg, a, mod, gn, wo, n2)


NO_RANK = 99.0
RANK_STEP = 1e27


def _top_vals(s, k, want_rank=False):
    lw = 128
    ng = s.shape[1] // lw
    rows = lax.broadcasted_iota(jnp.int32, (k, lw), 0)
    cur = [s[:, q * lw:(q + 1) * lw] for q in range(ng)]
    vals = [jnp.zeros((k, lw), F32) for _ in range(ng)]
    for t in range(k):
        for q in range(ng):
            mx = jnp.max(cur[q], axis=0, keepdims=True)
            vals[q] = jnp.where(rows == t, mx, vals[q])
            cur[q] = jnp.where(cur[q] == mx, NEG - t * RANK_STEP, cur[q])
    vals = jnp.concatenate(vals, axis=1)
    if not want_rank:
        return vals
    cur = jnp.concatenate(cur, axis=1)
    rank = jnp.where(cur < 0.5 * NEG, jnp.floor((NEG - cur) * (1.0 / RANK_STEP) + 0.5), NO_RANK)
    return vals, rank


def _peer_q_kernel(f_ref, wh_ref, wl_ref, k1_ref, k2_ref, r2_ref, cnt_ref, e1_ref, e2_ref):
    f = f_ref[...]
    fh = f.astype(BF16)
    fl = (f - fh.astype(F32)).astype(BF16)
    kk = PEER_TOPK

    def scores(h):
        c0 = h * 2 * PEER_NKEYS
        wh = wh_ref[:, c0:c0 + 2 * PEER_NKEYS]
        wl = wl_ref[:, c0:c0 + 2 * PEER_NKEYS]
        q = (jnp.dot(fh, wh, preferred_element_type=F32) + jnp.dot(fl, wh, preferred_element_type=F32)
             + jnp.dot(fh, wl, preferred_element_type=F32))
        nt = (((1,), (1,)), ((), ()))
        s1 = lax.dot_general(k1_ref[h], q[:, :PEER_NKEYS], nt, precision=HI, preferred_element_type=F32)
        s2 = lax.dot_general(k2_ref[h], q[:, PEER_NKEYS:], nt, precision=HI, preferred_element_type=F32)
        return s1, s2

    for h in range(PEER_HEADS):
        s1, s2 = scores(h)
        v1 = _top_vals(s1, kk)
        v2, rank2 = _top_vals(s2, kk, want_rank=True)
        row8 = lax.broadcasted_iota(jnp.int32, (8, s1.shape[1]), 0)
        blocks = [v1[0:1, :] + v2]
        for r in range(1, 8):
            blocks.append(jnp.where(row8 < kk // (r + 1), v1[r:r + 1, :] + v2[0:8, :], NEG))
        blocks.append(v1[8:kk, :] + v2[0:1, :])
        top = _top_vals(jnp.concatenate(blocks, axis=0), kk)
        tau = top[kk - 1:kk, :]
        z = jnp.sum(jnp.exp(top - top[0:1, :]), axis=0, keepdims=True)
        for q in range(s1.shape[1] // 128):
            ln = slice(128 * q, 128 * (q + 1))
            v2b = [jnp.broadcast_to(v2[c:c + 1, ln], (8, 128)) for c in range(kk)]
            taub = jnp.broadcast_to(tau[:, ln], (8, 128))
            for rb in range(PEER_NKEYS // 8):
                x = s1[8 * rb:8 * (rb + 1), ln]
                cnt = jnp.zeros((8, 128), F32)
                for c in range(kk):
                    cnt = cnt + jnp.where(x + v2b[c] >= taub, 1.0, 0.0)
                cnt_ref[h, 8 * rb:8 * (rb + 1), ln] = cnt
        r2_ref[h] = rank2.astype(BF16)
        e1_ref[h] = jnp.exp(s1 - v1[0:1, :])
        e2_ref[h] = (jnp.exp(s2 - v2[0:1, :]) / z).astype(BF16)


def _peer_q(f, wh, wl, k1, k2):
    m, d = f.shape
    hh, nk = PEER_HEADS, PEER_NKEYS
    big = pl.BlockSpec((hh, nk, TM), lambda i: (0, 0, i))
    sh32 = jax.ShapeDtypeStruct((hh, nk, m), F32)
    sh16 = jax.ShapeDtypeStruct((hh, nk, m), BF16)
    return pl.pallas_call(
        _peer_q_kernel,
        grid=(m // TM,),
        in_specs=[_tok_spec(d), _full(wh.shape), _full(wl.shape), _full(k1.shape), _full(k2.shape)],
        out_specs=[big, big, big, big],
        out_shape=[sh16, sh32, sh32, sh16],
        compiler_params=_cparams(("arbitrary",), VMEM_LIMIT),
        name="peer_query",
    )(f, wh, wl, k1, k2)


PD_TM = 512
PD_TE = 2048
PD_SUB = 1024


def _gelu(x):
    return 0.5 * x * (1.0 + lax.erf(x * (2.0 ** -0.5)))


def _row_bf16(tile, r, rows):
    one = jnp.broadcast_to(tile[r:r + 1, :], (8, tile.shape[1]))
    one = jnp.concatenate([one, one], axis=0).astype(BF16)
    return jnp.concatenate([one] * (rows // 16), axis=0)


def _run_if(cond, fn):
    def body(_, carry):
        fn()
        return carry
    lax.fori_loop(0, cond.astype(jnp.int32), body, 0)


def _peer_dense_kernel(ft_ref, u_ref, vt_ref, r2_ref, cnt_ref, e1_ref, e2_ref, o_ref, acc_ref):
    k = pl.program_id(1)
    nk = PEER_NKEYS
    nslab = PD_TE // nk
    per_sub = PD_SUB // nk
    zero = jnp.zeros((nk, PD_TM), BF16)
    nsub = PD_TE // PD_SUB

    def first():
        acc_ref[...] = jnp.zeros_like(acc_ref)

    _run_if(k == 0, first)

    def hidden(sb):
        return jnp.dot(u_ref[sb * PD_SUB:(sb + 1) * PD_SUB, :], ft_ref[...], preferred_element_type=F32)

    ht_next = hidden(0)
    for sb in range(nsub):
        e0 = sb * PD_SUB
        ht = ht_next
        if sb + 1 < nsub:
            ht_next = hidden(sb + 1)
        gates = []
        for al in range(sb * per_sub, (sb + 1) * per_sub):
            wg = zero
            a0 = pl.multiple_of(k * nslab + 8 * (al // 8), 8)
            for h in range(PEER_HEADS):
                cnt = _row_bf16(cnt_ref[h, pl.ds(a0, 8), :], al % 8, nk)
                e1 = _row_bf16(e1_ref[h, pl.ds(a0, 8), :], al % 8, nk)
                wg = wg + jnp.where(r2_ref[h] < cnt, e2_ref[h], zero) * e1
            gates.append(wg)
        ct = _gelu(ht).astype(BF16) * jnp.concatenate(gates, axis=0)
        acc_ref[...] += jnp.dot(vt_ref[:, e0:e0 + PD_SUB], ct, preferred_element_type=F32)

    def last():
        o_ref[...] = acc_ref[...].T

    _run_if(k == pl.num_programs(1) - 1, last)


def _peer_dense(ft, u, vt, r2, cnt, e1, e2):
    d, m = ft.shape
    ne = u.shape[0]
    hh, nk = PEER_HEADS, PEER_NKEYS
    big = pl.BlockSpec((hh, nk, PD_TM), lambda i, k: (0, 0, i))
    return pl.pallas_call(
        _peer_dense_kernel,
        grid=(m // PD_TM, ne // PD_TE),
        in_specs=[pl.BlockSpec((d, PD_TM), lambda i, k: (0, i)),
                  pl.BlockSpec((PD_TE, d), lambda i, k: (k, 0)),
                  pl.BlockSpec((d, PD_TE), lambda i, k: (0, k)),
                  big, big, big, big],
        out_specs=pl.BlockSpec((PD_TM, d), lambda i, k: (i, 0)),
        out_shape=jax.ShapeDtypeStruct((m, d), F32),
        scratch_shapes=[pltpu.VMEM((d, PD_TM), F32)],
        compiler_params=_cparams(("arbitrary", "arbitrary"), VMEM_LIMIT),
        name="peer_dense",
    )(ft, u, vt, r2, cnt, e1, e2)


def _res_kernel(x_ref, p_ref, mod_ref, modn_ref, g_ref, xo_ref, u_ref, *, d, final):
    xn = x_ref[...] + _mod_rows(mod_ref, 5, d) * p_ref[...]
    xo_ref[...] = xn
    y = _rms(xn) * g_ref[...]
    if not final:
        y = y * (1.0 + _mod_rows(modn_ref, 1, d)) + _mod_rows(modn_ref, 0, d)
    u_ref[...] = y


def _res(xs, p, mod, modn, g, nb, nbatch, final):
    m, d = xs.shape
    mspec = (lambda t: _mod_spec_lat(t, nb - 1)) if final else (lambda t: _mod_spec(t, nb, nbatch))
    return pl.pallas_call(
        functools.partial(_res_kernel, d=d, final=final),
        grid=(m // TM,),
        in_specs=[_tok_spec(d), _tok_spec(d), mspec(mod), mspec(modn), _full(g.shape)],
        out_specs=[_tok_spec(d), _tok_spec(d)],
        out_shape=[jax.ShapeDtypeStruct((m, d), F32)] * 2,
        compiler_params=_cparams(("arbitrary",)),
        name="peer_residual",
    )(xs, p, mod, modn, g)


def _split_dot(x, w):
    hi = x.astype(BF16)
    lo = (x - hi.astype(F32)).astype(BF16)
    return jnp.dot(hi, w, preferred_element_type=F32) + jnp.dot(lo, w, preferred_element_type=F32)


def _seg_sum(x, g_ref, gt_ref):
    return _split_dot(_split_dot(x, g_ref[...]), gt_ref[...])


def _rwkv_prep_kernel(u_ref, up_ref, un_ref, mu_ref, wr_ref, wk_ref, wv_ref, w0_ref, w1_ref, w2_ref,
                      a0_ref, a1_ref, a2_ref, g1_ref, g2_ref, kk_ref_, ka_ref, rk_ref, sg_ref, sgt_ref,
                      wf_o, wb_o, kf_o, kb_o, bf_o, bb_o, r_o, v_o, kk_o, g_o, bonus_o, *, nb, d):
    i = pl.program_id(0)
    p = i % nb
    u = u_ref[...]
    has_prev = (p >= 2).astype(F32)
    has_next = ((p >= 1) & (p <= nb - 2)).astype(F32)
    prev_row = up_ref[7:8, :] * has_prev
    next_row = un_ref[0:1, :] * has_next
    rows = lax.broadcasted_iota(jnp.int32, u.shape, 0)
    up = jnp.where(rows == 0, prev_row, pltpu.roll(u, 1, axis=0))
    un = jnp.where(rows == TM - 1, next_row, pltpu.roll(u, TM - 1, axis=0))
    xx = 0.5 * (up + un) - u

    def mix(zi):
        return (u + xx * mu_ref[zi:zi + 1, :]).astype(BF16)

    r = jnp.dot(mix(0), wr_ref[...], preferred_element_type=F32)
    k = jnp.dot(mix(2), wk_ref[...], preferred_element_type=F32)
    v = jnp.dot(mix(3), wv_ref[...], preferred_element_type=F32)
    lw = w0_ref[...] + _bdot(jnp.tanh(jnp.dot(mix(1), w1_ref[...], preferred_element_type=F32)), w2_ref[...])
    decay = jnp.exp(-_sigmoid(lw) * float(np.exp(-0.5)))
    a = _sigmoid(a0_ref[...] + _bdot(jnp.dot(mix(4), a1_ref[...], preferred_element_type=F32), a2_ref[...]))
    g = _bdot(_sigmoid(jnp.dot(mix(5), g1_ref[...], preferred_element_type=F32)), g2_ref[...])
    kk = k * kk_ref_[...]
    kk = kk * lax.rsqrt(jnp.maximum(_seg_sum(kk * kk, sg_ref, sgt_ref), 1e-24))
    ka = ka_ref[...]
    kf = k * (1.0 + (a[:, :d] - 1.0) * ka)
    kb = k * (1.0 + (a[:, d:] - 1.0) * ka)
    bonus = _seg_sum(r * (kf + kb) * rk_ref[...], sg_ref, sgt_ref) * v
    wf_o[...] = decay[:, :d]
    wb_o[...] = decay[:, d:]
    kf_o[...] = kf
    kb_o[...] = kb
    bf_o[...] = kk * a[:, :d]
    bb_o[...] = kk * a[:, d:]
    r_o[...] = r
    v_o[...] = v
    kk_o[...] = kk
    g_o[...] = g
    bonus_o[...] = bonus


def _rwkv_prep(u, consts, nb):
    m, d = u.shape
    r8 = TM // 8
    nb8 = m // 8
    in_specs = [_tok_spec(d),
                pl.BlockSpec((8, d), lambda i: (jnp.maximum(i * r8 - 1, 0), 0)),
                pl.BlockSpec((8, d), lambda i: (jnp.minimum((i + 1) * r8, nb8 - 1), 0))]
    in_specs += [_full(c.shape) for c in consts]
    return pl.pallas_call(
        functools.partial(_rwkv_prep_kernel, nb=nb, d=d),
        grid=(m // TM,),
        in_specs=in_specs,
        out_specs=[_tok_spec(d)] * 11,
        out_shape=[jax.ShapeDtypeStruct((m, d), F32)] * 11,
        compiler_params=_cparams(("arbitrary",), VMEM_LIMIT),
        name="rwkv_prep",
    )(u, u, u, *consts)


RW_TC = 256


def _rwkv_scan_kernel(wf, kf, bf, rf, vf, kkf, wb, kb, bb, rb, vb, kkb, yf_ref, yb_ref,
                      st, vk_a, vk_b, ycol_scr, *, npair):
    s = pl.program_id(1)
    n = RW_HEAD_DIM
    grp = 8

    @pl.when(s == 0)
    def _():
        st[...] = jnp.zeros_like(st)
        ycol_scr[...] = jnp.zeros_like(ycol_scr)

    w2 = 2 * n
    ri = lax.broadcasted_iota(jnp.int32, (n, w2), 0)
    li = lax.broadcasted_iota(jnp.int32, (n, w2), 1)
    eye_a = (li == ri).astype(BF16)
    bi = lax.broadcasted_iota(jnp.int32, (w2, w2), 0)
    bj = lax.broadcasted_iota(jnp.int32, (w2, w2), 1)
    ones_blk = ((bi // n) == (bj // n)).astype(BF16)
    b4i = lax.broadcasted_iota(jnp.int32, (2 * w2, 2 * w2), 0)
    b4j = lax.broadcasted_iota(jnp.int32, (2 * w2, 2 * w2), 1)
    ones_blk2 = ((b4i // n) == (b4j // n)).astype(BF16)
    lane8 = lax.broadcasted_iota(jnp.int32, (grp, w2), 1)
    rows16 = lax.broadcasted_iota(jnp.int32, (2 * grp, w2), 0)
    lane16 = lax.broadcasted_iota(jnp.int32, (2 * grp, w2), 1)
    zpad = jnp.zeros((w2 - 2 * grp, w2), F32)
    untr = ((lane16 == rows16) | (lane16 == rows16 + n)).astype(BF16)
    dirs = ((wf, kf, bf, rf, vf, kkf, yf_ref), (wb, kb, bb, rb, vb, kkb, yb_ref))
    nch = 2 * npair
    ngrp = RW_TC // grp
    nt = (((1,), (1,)), ((), ()))

    def halves(tile):
        return jnp.concatenate([jnp.where(lane8 < n, tile, 0.0), jnp.where(lane8 >= n, tile, 0.0)], axis=0)

    assert nch == 2 * grp

    def group_base(g, dd):
        base = (g if dd == 0 else ngrp - 1 - g) * grp
        return base if isinstance(base, int) else pl.multiple_of(base, grp)

    def lanes(c):
        p = c % npair
        return slice(w2 * p, w2 * (p + 1))

    def precompute(g, chains, vk_dst):
        gq = jnp.minimum(g, ngrp - 1)
        v2, ks = [], []
        for c in chains:
            dd = c // npair
            base = group_base(gq, dd)
            v_t = dirs[dd][4][pl.ds(base, grp), lanes(c)]
            ks.append(dirs[dd][1][pl.ds(base, grp), lanes(c)])
            v2 += [jnp.where(lane8 < n, v_t, 0.0), jnp.where(lane8 < n, pltpu.roll(v_t, n, axis=1), 0.0)]
        vt = lax.dot_general(eye_a, jnp.concatenate(v2, axis=0).astype(BF16), nt,
                             preferred_element_type=F32).astype(BF16)
        for idx, c in enumerate(chains):
            k2 = halves(ks[idx])
            kblk = jnp.concatenate(
                [jnp.where((rows16 == j) | (rows16 == grp + j), k2, 0.0) for j in range(grp)], axis=1)
            vk_dst[c] = jnp.dot(vt[:, 2 * grp * idx:2 * grp * (idx + 1)], kblk.astype(BF16),
                                preferred_element_type=F32)

    def untranspose(g, chains):
        for c in chains:
            dd = c // npair
            yc = ycol_scr[c]
            ycs = jnp.concatenate([jnp.where(li < n, yc, 0.0), jnp.where(li >= n, yc, 0.0)], axis=0)
            yt = lax.dot_general(untr, ycs.astype(BF16), nt, preferred_element_type=F32)
            dirs[dd][6][pl.ds(group_base(g, dd), grp), lanes(c)] = yt[0:grp, :]

    def run_group(g, vk_cur, vk_next):
        tiles = []
        for c in range(nch):
            dd = c // npair
            base = group_base(g, dd)
            tiles.append([dirs[dd][q][pl.ds(base, grp), lanes(c)] for q in (0, 2, 3, 5)])
        gprev = jnp.maximum(g - 1, 0)
        states = [st[c] for c in range(nch)]
        ycol = [jnp.zeros((n, w2), F32) for _ in range(nch)]
        zero_half = jnp.zeros((n, w2), BF16)
        srp = [zero_half] * nch
        for j in range(grp + 1):
            lhs = []
            for c in range(nch):
                kk_t = tiles[c][3]
                jj = min(j, grp - 1) if c < npair else max(grp - 1 - j, 0)
                sk = (states[c] * jnp.broadcast_to(kk_t[jj:jj + 1, :], (n, w2))).astype(BF16)
                lhs.append(jnp.concatenate([sk, srp[c]], axis=1))
            red = jnp.dot(jnp.concatenate(lhs, axis=0), ones_blk2, preferred_element_type=F32)
            for c in range(nch):
                w_t, b_t, r_t, kk_t = tiles[c]
                jj = j if c < npair else grp - 1 - j
                jp = j - 1 if c < npair else grp - j
                row = lambda tl: jnp.broadcast_to(tl[jj:jj + 1, :], (n, w2))
                sa = red[c * n:(c + 1) * n, :w2]
                yb = red[c * n:(c + 1) * n, w2:]
                if j > 0:
                    ycol[c] = jnp.where((li % n) == jp, yb, ycol[c])
                if j < grp:
                    states[c] = states[c] * row(w_t) - sa * row(b_t) + vk_cur[c, :, jj * w2:(jj + 1) * w2]
                    srp[c] = (states[c] * row(r_t)).astype(BF16)
            if j < grp:
                pair = (2 * j, 2 * j + 1)
                precompute(g + 1, pair, vk_next)
                untranspose(gprev, pair)
        for c in range(nch):
            st[c] = states[c]
            ycol_scr[c] = ycol[c]

    precompute(0, tuple(range(nch)), vk_a)

    def two_groups(i, carry):
        run_group(2 * i, vk_a, vk_b)
        run_group(2 * i + 1, vk_b, vk_a)
        return carry

    lax.fori_loop(0, ngrp // 2, two_groups, 0)
    untranspose(ngrp - 1, tuple(range(nch)))


def _rwkv_scan(wf, kf, bf, wb, kb, bb, r, v, kk, nbatch):
    m, d = r.shape
    nch = m // nbatch // RW_TC
    npair = d // (2 * RW_HEAD_DIM)

    def fi(b, s):
        return (b * nch + s, 0)

    def bi(b, s):
        return (b * nch + jnp.where(s == 0, 0, nch - s), 0)

    sf = pl.BlockSpec((RW_TC, d), fi)
    sb = pl.BlockSpec((RW_TC, d), bi)
    return pl.pallas_call(
        functools.partial(_rwkv_scan_kernel, npair=npair),
        grid=(nbatch, nch),
        in_specs=[sf] * 6 + [sb] * 6,
        out_specs=[sf, sb],
        out_shape=[jax.ShapeDtypeStruct((m, d), F32)] * 2,
        scratch_shapes=[pltpu.VMEM((2 * npair, RW_HEAD_DIM, 2 * RW_HEAD_DIM), F32),
                        pltpu.VMEM((2 * npair, RW_HEAD_DIM, 8 * 2 * RW_HEAD_DIM), F32),
                        pltpu.VMEM((2 * npair, RW_HEAD_DIM, 8 * 2 * RW_HEAD_DIM), F32),
                        pltpu.VMEM((2 * npair, RW_HEAD_DIM, 2 * RW_HEAD_DIM), F32)],
        compiler_params=_cparams(("arbitrary", "arbitrary"), VMEM_LIMIT),
        name="rwkv_scan",
    )(wf, kf, bf, r, v, kk, wb, kb, bb, r, v, kk)


def _rwkv_out_kernel(x_ref, yf_ref, yb_ref, bonus_ref, g_ref, mod_ref, lg_ref, lb_ref, wo_ref, n2_ref,
                     sg_ref, sgt_ref, xo_ref, f_ref, ft_ref, *, d):
    y = yf_ref[...] + yb_ref[...]
    inv = 1.0 / RW_HEAD_DIM
    mean = _seg_sum(y, sg_ref, sgt_ref) * inv
    yc = y - mean
    var = _seg_sum(yc * yc, sg_ref, sgt_ref) * inv
    yn = yc * lax.rsqrt(var + RW_GN_EPS) * lg_ref[...] + lb_ref[...]
    out = (yn + bonus_ref[...]) * g_ref[...]
    mix = _bdot(out, wo_ref[...])
    _residual_epilogue(x_ref[...], mix, mod_ref, n2_ref, xo_ref, f_ref, ft_ref, d)


def _rwkv_out(xs, yf, yb, bonus, g, mod, lg, lb, wo, n2, sg, sgt, nb, nbatch):
    d = xs.shape[1]
    m = nbatch * (nb - 1) * TM
    out_specs, out_shape = _epilogue_specs(m, d)
    return pl.pallas_call(
        functools.partial(_rwkv_out_kernel, d=d),
        grid=(m // TM,),
        in_specs=[_lat_tok_spec(d, nb)] * 5 + [_mod_spec_lat(mod, nb - 1)]
        + [_full(a.shape) for a in (lg, lb, wo, n2, sg, sgt)],
        out_specs=out_specs, out_shape=out_shape,
        compiler_params=_cparams(("arbitrary",), VMEM_LIMIT),
        name="rwkv_out",
    )(xs, yf, yb, bonus, g, mod, lg, lb, wo, n2, sg, sgt)


def _rope_tables(t_lat, n_ctx):
    rows = t_lat // GRID_W
    row = jnp.broadcast_to(jnp.arange(rows, dtype=F32)[:, None], (rows, GRID_W)).reshape(-1)
    col = jnp.broadcast_to(jnp.arange(GRID_W, dtype=F32)[None, :], (rows, GRID_W)).reshape(-1)
    n_freq = SWA_HEAD_DIM // 4
    inv_freq = ROPE_BASE ** (-jnp.arange(n_freq, dtype=F32) / n_freq)
    ar = row[:, None] * inv_freq
    ac = col[:, None] * inv_freq
    cos = jnp.concatenate([jnp.cos(ar), jnp.cos(ar), jnp.cos(ac), jnp.cos(ac)], axis=1)
    sin = jnp.concatenate([-jnp.sin(ar), jnp.sin(ar), -jnp.sin(ac), jnp.sin(ac)], axis=1)
    cos = jnp.concatenate([jnp.ones((n_ctx, SWA_HEAD_DIM), F32), cos], axis=0)
    sin = jnp.concatenate([jnp.zeros((n_ctx, SWA_HEAD_DIM), F32), sin], axis=0)
    return jnp.tile(cos, (1, 2)), jnp.tile(sin, (1, 2))


def _peer(f, ft, w_q, k1, k2, u, v):
    wh = w_q.astype(BF16)
    wl = (w_q - wh.astype(F32)).astype(BF16)
    r2, cnt, e1, e2 = _peer_q(f, wh, wl, k1, k2)
    return _peer_dense(ft, u.astype(BF16), v.T.astype(BF16), r2, cnt, e1, e2)


def kernel(x, c, ctx, c_ctx, ada_w, ada_b, norm1_g, norm2_g, ab_w_in, gla_dec_w2, gla_dec_b, gla_norm_g,
           swa_sink, ab_w_out, rw_mu, rw_w_rkv, rw_w_o, rw_w0, rw_w1, rw_w2, rw_a0, rw_a1, rw_a2, rw_g1,
           rw_g2, rw_k_k, rw_k_a, rw_r_k, rw_ln_g, rw_ln_b, peer_w_q, peer_k1, peer_k2, peer_u, peer_v,
           final_g):
    nbatch, t_lat, d = x.shape
    n_ctx = ctx.shape[1]
    assert n_ctx == TM and t_lat % TM == 0 and d == 1024
    s_tok = n_ctx + t_lat
    nb = s_tok // TM
    m = nbatch * s_tok
    xs = jnp.concatenate([ctx, x], axis=1).reshape(m, d)

    cc = jnp.zeros((16, d), F32).at[:nbatch].set(c).at[nbatch].set(c_ctx)
    mod0 = _ada_table(cc, ada_w[0], ada_b[0]).reshape(16, 1, 6 * d)
    mod1 = _ada_table(cc, ada_w[1], ada_b[1]).reshape(16, 1, 6 * d)
    row2 = lambda a: a.reshape(1, -1)

    w_in = ab_w_in[0]
    cuts = np.cumsum((256, 256, 512, 512, 32, 512, 128, 128))[:-1].tolist()
    wgq, wgk, wgv, wgg, wlr, wsq, wsk, wsv = jnp.split(w_in, cuts, axis=1)
    wp = jnp.concatenate([wgq, wgk, wgv, wgg, wsq, wsk, wsv, wlr,
                          jnp.zeros((d, _P_END - _P_LR - 32), F32)], axis=1).astype(BF16)
    hk = GLA_HEADS * GLA_DK
    w2p = jnp.zeros((128, 2 * hk), F32)
    w2p = w2p.at[0:GLA_LOWRANK, 0:hk].set(gla_dec_w2[0, 0]).at[GLA_LOWRANK:2 * GLA_LOWRANK, hk:].set(gla_dec_w2[0, 1])
    db = gla_dec_b[0].reshape(1, 2 * hk)
    cos, sin = _rope_tables(t_lat, n_ctx)
    gq, gk, gv, gg, laf, lab, sq, sk, sv = _proj0(xs, mod0, row2(norm1_g[0]), wp, w2p, db, cos, sin, nb, nbatch)
    of, ob = _gla(gq, gk, gv, laf, lab, nbatch, n_ctx // GLA_CHUNK)
    sink = jnp.broadcast_to(swa_sink[0][:, None], (SWA_HEADS, 128))
    att = _swa(sq, sk, sv, sink, nbatch, n_ctx // WINDOW, t_lat // WINDOW)
    xs, f, ft = _out0(xs, of, ob, gg, att, mod0, row2(gla_norm_g[0]), ab_w_out[0].astype(BF16),
                      row2(norm2_g[0]), nb, nbatch)
    p = _peer(f, ft, peer_w_q[0], peer_k1[0], peer_k2[0], peer_u[0], peer_v[0])
    xs, u = _res(xs, p, mod0, mod1, row2(norm1_g[1]), nb, nbatch, final=False)

    lora = rw_w1.shape[-1]
    w1c = jnp.concatenate([rw_w1[0, 0], rw_w1[0, 1]], axis=1).astype(BF16)
    w2c = jnp.zeros((2 * lora, 2 * d), F32).at[:lora, :d].set(rw_w2[0, 0]).at[lora:, d:].set(rw_w2[0, 1]).astype(BF16)
    la = rw_a1.shape[-1]
    a1c = jnp.concatenate([rw_a1[0, 0], rw_a1[0, 1]], axis=1).astype(BF16)
    a2c = jnp.zeros((2 * la, 2 * d), F32).at[:la, :d].set(rw_a2[0, 0]).at[la:, d:].set(rw_a2[0, 1]).astype(BF16)
    lg = rw_g1.shape[-1]
    g1p = jnp.zeros((d, 256), F32).at[:, :lg].set(rw_g1[0]).astype(BF16)
    g2p = jnp.zeros((256, d), F32).at[:lg].set(rw_g2[0]).astype(BF16)
    nheads = d // RW_HEAD_DIM
    sg = (jnp.arange(d)[:, None] // RW_HEAD_DIM == jnp.arange(128)[None, :]).astype(BF16)
    sgt = sg.T
    consts = [rw_mu[0], rw_w_rkv[0, 0].astype(BF16), rw_w_rkv[0, 1].astype(BF16), rw_w_rkv[0, 2].astype(BF16),
              rw_w0[0].reshape(1, 2 * d), w1c, w2c, rw_a0[0].reshape(1, 2 * d), a1c, a2c, g1p, g2p,
              row2(rw_k_k[0]), row2(rw_k_a[0]), rw_r_k[0].reshape(1, d), sg, sgt]
    mu8 = jnp.zeros((8, d), F32).at[:6].set(rw_mu[0])
    consts[0] = mu8
    wf, wb, kf, kb, bf, bb, r, v, kk, g, bonus = _rwkv_prep(u, consts, nb)
    yf, yb = _rwkv_scan(wf, kf, bf, wb, kb, bb, r, v, kk, nbatch)
    xs, f, ft = _rwkv_out(xs, yf, yb, bonus, g, mod1, row2(rw_ln_g[0]), row2(rw_ln_b[0]),
                          rw_w_o[0].astype(BF16), row2(norm2_g[1]), sg, sgt, nb, nbatch)
    p = _peer(f, ft, peer_w_q[1], peer_k1[1], peer_k2[1], peer_u[1], peer_v[1])
    _, y = _res(xs, p, mod1, mod1, row2(final_g), nb, nbatch, final=True)
    return y.reshape(nbatch, t_lat, d)
```

```python
import functools

import numpy as np
import jax
import jax.numpy as jnp
from jax import lax
from jax.experimental import pallas as pl
from jax.experimental.pallas import tpu as pltpu

F32 = jnp.float32
BF16 = jnp.bfloat16
HI = lax.Precision.HIGHEST

NORM_EPS = 1e-6
GLA_HEADS, GLA_DK, GLA_DV, GLA_LOWRANK, GLA_TAU, GLA_CHUNK = 4, 64, 128, 16, 16.0, 64
SWA_HEADS, SWA_KV_HEADS, SWA_HEAD_DIM, WINDOW = 8, 2, 64, 128
ROPE_BASE = 10000.0
GRID_W = 64
RW_HEAD_DIM = 64
RW_GN_EPS = 64e-5
PEER_HEADS, PEER_NKEYS, PEER_TOPK = 8, 128, 16
NEG = -1e30

TM = 256
VMEM_LIMIT = 56 * 1024 * 1024


def _cparams(sem, vmem=None):
    return pltpu.CompilerParams(dimension_semantics=sem, vmem_limit_bytes=vmem)


def _bdot(a, b):
    return jnp.dot(a.astype(BF16), b.astype(BF16), preferred_element_type=F32)


def _hdot(a, b):
    return jnp.dot(a, b, precision=HI, preferred_element_type=F32)


def _sigmoid(x):
    return 1.0 / (1.0 + jnp.exp(-x))


def _rms(x):
    return x * lax.rsqrt(jnp.mean(x * x, axis=-1, keepdims=True) + NORM_EPS)


def _mod_rows(mod_ref, k, d):
    return mod_ref[0, :, k * d:(k + 1) * d]


def _mod_spec(mod, nb, nbatch):
    return pl.BlockSpec((1, 1, mod.shape[2]), lambda i: (jnp.where(i % nb == 0, nbatch, i // nb), 0, 0))


def _mod_spec_lat(mod, nbl):
    return pl.BlockSpec((1, 1, mod.shape[2]), lambda i: (i // nbl, 0, 0))


def _lat_tok_spec(n, nb):
    return pl.BlockSpec((TM, n), lambda i: ((i // (nb - 1)) * nb + 1 + i % (nb - 1), 0))


def _full(shape):
    n = len(shape)
    return pl.BlockSpec(shape, lambda *_: (0,) * n)


def _ada_kernel(c_ref, w_ref, b_ref, o_ref):
    c = c_ref[...]
    s = c * _sigmoid(c)
    o_ref[...] = _hdot(s, w_ref[...]) + b_ref[...]


def _ada_table(cc, w, b):
    rows, d = cc.shape
    n = w.shape[1]
    tn = 512
    return pl.pallas_call(
        _ada_kernel,
        grid=(n // tn,),
        in_specs=[_full((rows, d)), pl.BlockSpec((d, tn), lambda j: (0, j)),
                  pl.BlockSpec((1, tn), lambda j: (0, j))],
        out_specs=pl.BlockSpec((rows, tn), lambda j: (0, j)),
        out_shape=jax.ShapeDtypeStruct((rows, n), F32),
        compiler_params=_cparams(("arbitrary",)),
        name="ada_table",
    )(cc, w, b.reshape(1, n))


_P_GQ, _P_GK, _P_GV, _P_GG, _P_SQ, _P_SK, _P_SV, _P_LR, _P_END = 0, 256, 512, 1024, 1536, 2048, 2176, 2304, 2432


def _rope(x, cos, sin):
    lane = lax.broadcasted_iota(jnp.int32, x.shape, 1)
    up = pltpu.roll(x, 112, axis=1)
    dn = pltpu.roll(x, 16, axis=1)
    sw = jnp.where((lane % 32) < 16, up, dn)
    return x * cos + sw * sin


def _proj0_kernel(x_ref, mod_ref, g_ref, w_ref, w2_ref, db_ref, cos_ref, sin_ref,
                  gq_ref, gk_ref, gv_ref, gg_ref, laf_ref, lab_ref, sq_ref, sk_ref, sv_ref,
                  *, d):
    u = (_rms(x_ref[...]) * g_ref[...] * (1.0 + _mod_rows(mod_ref, 1, d))
         + _mod_rows(mod_ref, 0, d)).astype(BF16)

    def seg(a, b):
        return jnp.dot(u, w_ref[:, a:b], preferred_element_type=F32)

    gq_ref[...] = seg(_P_GQ, _P_GK) * (GLA_DK ** -0.5)
    gk_ref[...] = seg(_P_GK, _P_GV)
    gv_ref[...] = seg(_P_GV, _P_GG)
    gg_ref[...] = seg(_P_GG, _P_SQ)
    lr = seg(_P_LR, _P_END)
    z = _hdot(lr, w2_ref[...]) + db_ref[...]
    la = (jnp.minimum(z, 0.0) - jnp.log(1.0 + jnp.exp(-jnp.abs(z)))) * (1.0 / GLA_TAU)
    hk = GLA_HEADS * GLA_DK
    laf_ref[...] = la[:, :hk]
    lab_ref[...] = la[:, hk:]
    cos = cos_ref[...]
    sin = sin_ref[...]
    sq = seg(_P_SQ, _P_SK) * (SWA_HEAD_DIM ** -0.5)
    for c in range(4):
        sq_ref[:, 128 * c:128 * (c + 1)] = _rope(sq[:, 128 * c:128 * (c + 1)], cos, sin)
    sk_ref[...] = _rope(seg(_P_SK, _P_SV), cos, sin)
    sv_ref[...] = seg(_P_SV, _P_LR)


def _proj0(xs, mod, g, w, w2, db, cos, sin, nb, nbatch):
    m, d = xs.shape
    widths = (256, 256, 512, 512, 256, 256, 512, 128, 128)
    return pl.pallas_call(
        functools.partial(_proj0_kernel, d=d),
        grid=(m // TM,),
        in_specs=[pl.BlockSpec((TM, d), lambda i: (i, 0)), _mod_spec(mod, nb, nbatch), _full((1, d)),
                  _full(w.shape), _full(w2.shape), _full(db.shape),
                  pl.BlockSpec((TM, 128), lambda i: (i % nb, 0)),
                  pl.BlockSpec((TM, 128), lambda i: (i % nb, 0))],
        out_specs=[pl.BlockSpec((TM, n), lambda i: (i, 0)) for n in widths],
        out_shape=[jax.ShapeDtypeStruct((m, n), F32) for n in widths],
        compiler_params=_cparams(("arbitrary",), VMEM_LIMIT),
        name="proj0",
    )(xs, mod, g, w, w2, db, cos, sin)


def _gla_dir(q_ref, k_ref, v_ref, la_ref, o_ref, st, cum, mask, tot_row):
    c = GLA_CHUNK
    la = la_ref[...]
    bc = _hdot(cum, la)
    tot = bc[tot_row:tot_row + 1, :]
    q_in = q_ref[...] * jnp.exp(bc)
    k_in = (k_ref[...] * jnp.exp(-bc)).astype(BF16)
    k_out = (k_ref[...] * jnp.exp(tot - bc)).astype(BF16)
    ones = jnp.ones((c, GLA_DV), F32)
    dcol = jnp.exp(lax.dot_general(la, ones, (((0,), (0,)), ((), ())), precision=HI,
                                   preferred_element_type=F32))
    lane = lax.broadcasted_iota(jnp.int32, q_in.shape, 1)
    s_prev = st[...].astype(BF16)
    for h in range(GLA_HEADS):
        qh = jnp.where((lane // GLA_DK) == h, q_in, 0.0).astype(BF16)
        att = lax.dot_general(qh, k_in, (((1,), (1,)), ((), ())), preferred_element_type=F32)
        att = jnp.where(mask, att, 0.0).astype(BF16)
        vh = v_ref[:, h * GLA_DV:(h + 1) * GLA_DV].astype(BF16)
        o = (jnp.dot(att, vh, preferred_element_type=F32)
             + jnp.dot(qh, s_prev, preferred_element_type=F32))
        o_ref[:, h * GLA_DV:(h + 1) * GLA_DV] = o
        upd = lax.dot_general(k_out, vh, (((0,), (0,)), ((), ())), preferred_element_type=F32)
        r0, r1 = h * GLA_DK, (h + 1) * GLA_DK
        st[r0:r1, :] = st[r0:r1, :] * dcol[r0:r1, :] + upd[r0:r1, :]


def _gla_kernel(qf, kf, vf, laf, qb, kb, vb, lab, of_ref, ob_ref, sf, sb):
    s = pl.program_id(1)

    @pl.when(s == 0)
    def _():
        sf[...] = jnp.zeros_like(sf)
        sb[...] = jnp.zeros_like(sb)

    c = GLA_CHUNK
    ri = lax.broadcasted_iota(jnp.int32, (c, c), 0)
    ci = lax.broadcasted_iota(jnp.int32, (c, c), 1)
    _gla_dir(qf, kf, vf, laf, of_ref, sf, (ri >= ci).astype(F32), ri >= ci, c - 1)
    _gla_dir(qb, kb, vb, lab, ob_ref, sb, (ri <= ci).astype(F32), ci > ri, 0)


def _gla(gq, gk, gv, laf, lab, nbatch, nctx_chunks):
    m = gq.shape[0]
    c = GLA_CHUNK
    nch = m // nbatch // c
    hk, hv = GLA_HEADS * GLA_DK, GLA_HEADS * GLA_DV

    def fi(b, s):
        return (b * nch + s, 0)

    def bi(b, s):
        cb = jnp.where(s < nctx_chunks, nctx_chunks - 1 - s, nch - 1 + nctx_chunks - s)
        return (b * nch + cb, 0)

    sk = lambda im: pl.BlockSpec((c, hk), im)
    sv = lambda im: pl.BlockSpec((c, hv), im)
    return pl.pallas_call(
        _gla_kernel,
        grid=(nbatch, nch),
        in_specs=[sk(fi), sk(fi), sv(fi), sk(fi), sk(bi), sk(bi), sv(bi), sk(bi)],
        out_specs=[sv(fi), sv(bi)],
        out_shape=[jax.ShapeDtypeStruct((m, hv), F32)] * 2,
        scratch_shapes=[pltpu.VMEM((hk, GLA_DV), F32), pltpu.VMEM((hk, GLA_DV), F32)],
        compiler_params=_cparams(("arbitrary", "arbitrary")),
        name="gla_scan",
    )(gq, gk, gv, laf, gq, gk, gv, lab)


def _swa_kernel(q_ref, kc_ref, vc_ref, kp_ref, kcur_ref, kn_ref, vp_ref, vcur_ref, vn_ref, sink_ref,
                o_ref, *, nctx_blocks, nlat_blocks):
    j = pl.program_id(1)
    w = WINDOW
    n = j - nctx_blocks
    is_lat = j >= nctx_blocks
    nkc = kc_ref.shape[0]
    kall = jnp.concatenate([kc_ref[...], kp_ref[...], kcur_ref[...], kn_ref[...]], axis=0)
    vall = jnp.concatenate([vc_ref[...], vp_ref[...], vcur_ref[...], vn_ref[...]], axis=0)
    nk = kall.shape[0]
    lane = lax.broadcasted_iota(jnp.int32, kall.shape, 1)
    lo = lane < SWA_HEAD_DIM
    kroll = pltpu.roll(kall, SWA_HEAD_DIM, axis=1)
    vroll = pltpu.roll(vall, SWA_HEAD_DIM, axis=1)
    k2 = [jnp.where(lo, kall, kroll).astype(BF16), jnp.where(lo, kroll, kall).astype(BF16)]
    v2 = [jnp.where(lo, vall, vroll).astype(BF16), jnp.where(lo, vroll, vall).astype(BF16)]
    qi = lax.broadcasted_iota(jnp.int32, (w, nk), 0)
    kj = lax.broadcasted_iota(jnp.int32, (w, nk), 1)
    qpos = n * w + qi
    kpos = (n - 1) * w + (kj - nkc)
    loc_ok = (jnp.abs(qpos - kpos) <= WINDOW) & (kpos >= 0) & (kpos < nlat_blocks * w) & is_lat
    valid = (kj < nkc) | ((kj >= nkc) & loc_ok)
    qlane = lax.broadcasted_iota(jnp.int32, (w, 128), 1)
    qlo = qlane < SWA_HEAD_DIM
    rep = SWA_HEADS // SWA_KV_HEADS
    for pr in range(SWA_HEADS // 2):
        g = (2 * pr) // rep
        qp = q_ref[:, 128 * pr:128 * (pr + 1)]
        outs = []
        for half in range(2):
            h = 2 * pr + half
            qh = jnp.where(qlo if half == 0 else ~qlo, qp, 0.0).astype(BF16)
            s = lax.dot_general(qh, k2[g], (((1,), (1,)), ((), ())), preferred_element_type=F32)
            s = jnp.where(valid, s, NEG)
            sink = sink_ref[h:h + 1, 0:1]
            mx = jnp.maximum(jnp.max(s, axis=-1, keepdims=True), sink)
            p = jnp.exp(s - mx)
            den = jnp.sum(p, axis=-1, keepdims=True) + jnp.exp(sink - mx)
            o = jnp.dot(p.astype(BF16), v2[g], preferred_element_type=F32)
            outs.append(o / den)
        o_ref[:, 128 * pr:128 * (pr + 1)] = jnp.where(qlo, outs[0], outs[1])


def _swa(sq, sk, sv, sink, nbatch, nctx_blocks, nlat_blocks):
    m = sq.shape[0]
    w = WINDOW
    nblk = nctx_blocks + nlat_blocks
    nkc = nctx_blocks * w

    def qmap(b, j):
        return (b * nblk + j, 0)

    def cmap(b, j):
        return (b, 0)

    def nmap(off):
        def f(b, j):
            n = jnp.clip(j - nctx_blocks + off, 0, nlat_blocks - 1)
            return (b * nblk + nctx_blocks + n, 0)
        return f

    kvw = sk.shape[1]
    kcs = pl.BlockSpec((nkc, kvw), lambda b, j: (b * (nblk * w // nkc), 0))
    kvs = lambda off: pl.BlockSpec((w, kvw), nmap(off))
    return pl.pallas_call(
        functools.partial(_swa_kernel, nctx_blocks=nctx_blocks, nlat_blocks=nlat_blocks),
        grid=(nbatch, nblk),
        in_specs=[pl.BlockSpec((w, sq.shape[1]), qmap), kcs, kcs, kvs(-1), kvs(0), kvs(1),
                  kvs(-1), kvs(0), kvs(1), _full(sink.shape)],
        out_specs=pl.BlockSpec((w, sq.shape[1]), qmap),
        out_shape=jax.ShapeDtypeStruct(sq.shape, F32),
        compiler_params=_cparams(("arbitrary", "arbitrary")),
        name="swa_attn",
    )(sq, sk, sv, sk, sk, sk, sv, sv, sv, sink)


def _residual_epilogue(x, mix, mod_ref, n2_ref, xo_ref, f_ref, ft_ref, d):
    xn = x + _mod_rows(mod_ref, 2, d) * mix
    xo_ref[...] = xn
    f = _rms(xn) * n2_ref[...] * (1.0 + _mod_rows(mod_ref, 4, d)) + _mod_rows(mod_ref, 3, d)
    f_ref[...] = f
    ft_ref[...] = f.T.astype(BF16)


def _out0_kernel(x_ref, of_ref, ob_ref, gg_ref, a_ref, mod_ref, gn_ref, wo_ref, n2_ref,
                 xo_ref, f_ref, ft_ref, *, d):
    o = of_ref[...] + ob_ref[...]
    gate = gg_ref[...]
    gate = gate * _sigmoid(gate)
    parts = []
    for h in range(GLA_HEADS):
        oh = o[:, h * GLA_DV:(h + 1) * GLA_DV]
        parts.append(oh * lax.rsqrt(jnp.mean(oh * oh, axis=-1, keepdims=True) + NORM_EPS))
    on = jnp.concatenate(parts, axis=1) * gn_ref[...] * gate
    hv = GLA_HEADS * GLA_DV
    mix = _bdot(on, wo_ref[0:hv, :]) + _bdot(a_ref[...], wo_ref[hv:, :])
    _residual_epilogue(x_ref[...], mix, mod_ref, n2_ref, xo_ref, f_ref, ft_ref, d)


def _tok_spec(n):
    return pl.BlockSpec((TM, n), lambda i: (i, 0))


def _epilogue_specs(m, d):
    out_specs = [_tok_spec(d), _tok_spec(d), pl.BlockSpec((d, TM), lambda i: (0, i))]
    out_shape = [jax.ShapeDtypeStruct((m, d), F32), jax.ShapeDtypeStruct((m, d), F32),
                 jax.ShapeDtypeStruct((d, m), BF16)]
    return out_specs, out_shape


def _out0(xs, of, ob, gg, a, mod, gn, wo, n2, nb, nbatch):
    m, d = xs.shape
    out_specs, out_shape = _epilogue_specs(m, d)
    return pl.pallas_call(
        functools.partial(_out0_kernel, d=d),
        grid=(m // TM,),
        in_specs=[_tok_spec(d), _tok_spec(of.shape[1]), _tok_spec(ob.shape[1]), _tok_spec(gg.shape[1]),
                  _tok_spec(a.shape[1]), _mod_spec(mod, nb, nbatch), _full(gn.shape), _full(wo.shape),
                  _full(n2.shape)],
        out_specs=out_specs, out_shape=out_shape,
        compiler_params=_cparams(("arbitrary",), VMEM_LIMIT),
        name="out0",
    )(xs, of, ob, gg, a, mod, gn, wo, n2)


NO_RANK = 99.0
RANK_STEP = 1e27


def _top_vals(s, k, want_rank=False):
    lw = 128
    ng = s.shape[1] // lw
    rows = lax.broadcasted_iota(jnp.int32, (k, lw), 0)
    cur = [s[:, q * lw:(q + 1) * lw] for q in range(ng)]
    vals = [jnp.zeros((k, lw), F32) for _ in range(ng)]
    for t in range(k):
        for q in range(ng):
            mx = jnp.max(cur[q], axis=0, keepdims=True)
            vals[q] = jnp.where(rows == t, mx, vals[q])
            cur[q] = jnp.where(cur[q] == mx, NEG - t * RANK_STEP, cur[q])
    vals = jnp.concatenate(vals, axis=1)
    if not want_rank:
        return vals
    cur = jnp.concatenate(cur, axis=1)
    rank = jnp.where(cur < 0.5 * NEG, jnp.floor((NEG - cur) * (1.0 / RANK_STEP) + 0.5), NO_RANK)
    return vals, rank


def _peer_scores_kernel(f_ref, wh_ref, wl_ref, kh_ref, kl_ref, s1_ref, s2_ref):
    f = f_ref[...]
    fh = f.astype(BF16)
    fl = (f - fh.astype(F32)).astype(BF16)
    wh = wh_ref[...]
    q = (jnp.dot(fh, wh, preferred_element_type=F32) + jnp.dot(fl, wh, preferred_element_type=F32)
         + jnp.dot(fh, wl_ref[...], preferred_element_type=F32))
    qh = q.astype(BF16)
    ql = (q - qh.astype(F32)).astype(BF16)
    nt = (((1,), (1,)), ((), ()))
    for h in range(PEER_HEADS):
        for z, s_ref in enumerate((s1_ref, s2_ref)):
            c0 = (2 * h + z) * PEER_NKEYS
            kh = kh_ref[z, h]
            qhs = qh[:, c0:c0 + PEER_NKEYS]
            s_ref[h] = (lax.dot_general(kh, qhs, nt, preferred_element_type=F32)
                        + lax.dot_general(kh, ql[:, c0:c0 + PEER_NKEYS], nt, preferred_element_type=F32)
                        + lax.dot_general(kl_ref[z, h], qhs, nt, preferred_element_type=F32))


def _peer_q_kernel(s1_ref, s2_ref, r2_ref, cnt_ref, e1_ref, e2_ref):
    kk = PEER_TOPK
    for h in range(PEER_HEADS):
        s1 = s1_ref[h]
        s2 = s2_ref[h]
        v1 = _top_vals(s1, kk)
        v2, rank2 = _top_vals(s2, kk, want_rank=True)
        row8 = lax.broadcasted_iota(jnp.int32, (8, s1.shape[1]), 0)
        blocks = [v1[0:1, :] + v2]
        for r in range(1, 8):
            blocks.append(jnp.where(row8 < kk // (r + 1), v1[r:r + 1, :] + v2[0:8, :], NEG))
        blocks.append(v1[8:kk, :] + v2[0:1, :])
        top = _top_vals(jnp.concatenate(blocks, axis=0), kk)
        tau = top[kk - 1:kk, :]
        z = jnp.sum(jnp.exp(top - top[0:1, :]), axis=0, keepdims=True)
        for q in range(s1.shape[1] // 128):
            ln = slice(128 * q, 128 * (q + 1))
            v2b = [jnp.broadcast_to(v2[c:c + 1, ln], (8, 128)) for c in range(kk)]
            taub = jnp.broadcast_to(tau[:, ln], (8, 128))
            for rb in range(PEER_NKEYS // 8):
                x = s1[8 * rb:8 * (rb + 1), ln]
                cnt = jnp.zeros((8, 128), F32)
                for c in range(kk):
                    cnt = cnt + jnp.where(x + v2b[c] >= taub, 1.0, 0.0)
                cnt_ref[h, 8 * rb:8 * (rb + 1), ln] = cnt
        r2_ref[h] = rank2.astype(BF16)
        e1_ref[h] = jnp.exp(s1 - v1[0:1, :])
        e2_ref[h] = (jnp.exp(s2 - v2[0:1, :]) / z).astype(BF16)


def _peer_q(f, wh, wl, k1, k2):
    m, d = f.shape
    hh, nk = PEER_HEADS, PEER_NKEYS
    big = pl.BlockSpec((hh, nk, TM), lambda i: (0, 0, i))
    sh32 = jax.ShapeDtypeStruct((hh, nk, m), F32)
    sh16 = jax.ShapeDtypeStruct((hh, nk, m), BF16)
    kf = jnp.stack([k1, k2])
    kh = kf.astype(BF16)
    kl = (kf - kh.astype(F32)).astype(BF16)
    s1, s2 = pl.pallas_call(
        _peer_scores_kernel,
        grid=(m // TM,),
        in_specs=[_tok_spec(d), _full(wh.shape), _full(wl.shape), _full(kh.shape), _full(kl.shape)],
        out_specs=[big, big],
        out_shape=[sh32, sh32],
        compiler_params=_cparams(("arbitrary",), VMEM_LIMIT),
        name="peer_scores",
    )(f, wh, wl, kh, kl)
    return pl.pallas_call(
        _peer_q_kernel,
        grid=(m // TM,),
        in_specs=[big, big],
        out_specs=[big, big, big, big],
        out_shape=[sh16, sh32, sh32, sh16],
        compiler_params=_cparams(("arbitrary",), VMEM_LIMIT),
        name="peer_query",
    )(s1, s2)


PD_TM = 512
PD_TE = 2048
PD_SUB = 1024


def _gelu(x):
    return 0.5 * x * (1.0 + lax.erf(x * (2.0 ** -0.5)))


def _row_bf16(tile, r, rows):
    one = jnp.broadcast_to(tile[r:r + 1, :], (8, tile.shape[1]))
    one = jnp.concatenate([one, one], axis=0).astype(BF16)
    return jnp.concatenate([one] * (rows // 16), axis=0)


def _run_if(cond, fn):
    def body(_, carry):
        fn()
        return carry
    lax.fori_loop(0, cond.astype(jnp.int32), body, 0)


def _peer_dense_kernel(ft_ref, u_ref, vt_ref, r2_ref, cnt_ref, e1_ref, e2_ref, o_ref, acc_ref):
    k = pl.program_id(1)
    nk = PEER_NKEYS
    nslab = PD_TE // nk
    per_sub = PD_SUB // nk
    zero = jnp.zeros((nk, PD_TM), BF16)
    nsub = PD_TE // PD_SUB

    def first():
        acc_ref[...] = jnp.zeros_like(acc_ref)

    _run_if(k == 0, first)

    def hidden(sb):
        return jnp.dot(u_ref[sb * PD_SUB:(sb + 1) * PD_SUB, :], ft_ref[...], preferred_element_type=F32)

    ht_next = hidden(0)
    for sb in range(nsub):
        e0 = sb * PD_SUB
        ht = ht_next
        if sb + 1 < nsub:
            ht_next = hidden(sb + 1)
        gates = []
        for al in range(sb * per_sub, (sb + 1) * per_sub):
            wg = zero
            a0 = pl.multiple_of(k * nslab + 8 * (al // 8), 8)
            for h in range(PEER_HEADS):
                cnt = _row_bf16(cnt_ref[h, pl.ds(a0, 8), :], al % 8, nk)
                e1 = _row_bf16(e1_ref[h, pl.ds(a0, 8), :], al % 8, nk)
                wg = wg + jnp.where(r2_ref[h] < cnt, e2_ref[h], zero) * e1
            gates.append(wg)
        ct = _gelu(ht).astype(BF16) * jnp.concatenate(gates, axis=0)
        acc_ref[...] += jnp.dot(vt_ref[:, e0:e0 + PD_SUB], ct, preferred_element_type=F32)

    def last():
        o_ref[...] = acc_ref[...].T

    _run_if(k == pl.num_programs(1) - 1, last)


def _peer_dense(ft, u, vt, r2, cnt, e1, e2):
    d, m = ft.shape
    ne = u.shape[0]
    hh, nk = PEER_HEADS, PEER_NKEYS
    big = pl.BlockSpec((hh, nk, PD_TM), lambda i, k: (0, 0, i))
    return pl.pallas_call(
        _peer_dense_kernel,
        grid=(m // PD_TM, ne // PD_TE),
        in_specs=[pl.BlockSpec((d, PD_TM), lambda i, k: (0, i)),
                  pl.BlockSpec((PD_TE, d), lambda i, k: (k, 0)),
                  pl.BlockSpec((d, PD_TE), lambda i, k: (0, k)),
                  big, big, big, big],
        out_specs=pl.BlockSpec((PD_TM, d), lambda i, k: (i, 0)),
        out_shape=jax.ShapeDtypeStruct((m, d), F32),
        scratch_shapes=[pltpu.VMEM((d, PD_TM), F32)],
        compiler_params=_cparams(("arbitrary", "arbitrary"), VMEM_LIMIT),
        name="peer_dense",
    )(ft, u, vt, r2, cnt, e1, e2)


def _res_kernel(x_ref, p_ref, mod_ref, modn_ref, g_ref, xo_ref, u_ref, *, d, final):
    xn = x_ref[...] + _mod_rows(mod_ref, 5, d) * p_ref[...]
    xo_ref[...] = xn
    y = _rms(xn) * g_ref[...]
    if not final:
        y = y * (1.0 + _mod_rows(modn_ref, 1, d)) + _mod_rows(modn_ref, 0, d)
    u_ref[...] = y


def _res(xs, p, mod, modn, g, nb, nbatch, final):
    m, d = xs.shape
    mspec = (lambda t: _mod_spec_lat(t, nb - 1)) if final else (lambda t: _mod_spec(t, nb, nbatch))
    return pl.pallas_call(
        functools.partial(_res_kernel, d=d, final=final),
        grid=(m // TM,),
        in_specs=[_tok_spec(d), _tok_spec(d), mspec(mod), mspec(modn), _full(g.shape)],
        out_specs=[_tok_spec(d), _tok_spec(d)],
        out_shape=[jax.ShapeDtypeStruct((m, d), F32)] * 2,
        compiler_params=_cparams(("arbitrary",)),
        name="peer_residual",
    )(xs, p, mod, modn, g)


def _split_dot(x, w):
    hi = x.astype(BF16)
    lo = (x - hi.astype(F32)).astype(BF16)
    return jnp.dot(hi, w, preferred_element_type=F32) + jnp.dot(lo, w, preferred_element_type=F32)


def _seg_sum(x, g_ref, gt_ref):
    return _split_dot(_split_dot(x, g_ref[...]), gt_ref[...])


def _rwkv_prep_kernel(u_ref, up_ref, un_ref, mu_ref, wr_ref, wk_ref, wv_ref, w0_ref, w1_ref, w2_ref,
                      a0_ref, a1_ref, a2_ref, g1_ref, g2_ref, kk_ref_, ka_ref, rk_ref, sg_ref, sgt_ref,
                      wf_o, wb_o, kf_o, kb_o, bf_o, bb_o, r_o, v_o, kk_o, g_o, bonus_o, *, nb, d):
    i = pl.program_id(0)
    p = i % nb
    u = u_ref[...]
    has_prev = (p >= 2).astype(F32)
    has_next = ((p >= 1) & (p <= nb - 2)).astype(F32)
    prev_row = up_ref[7:8, :] * has_prev
    next_row = un_ref[0:1, :] * has_next
    rows = lax.broadcasted_iota(jnp.int32, u.shape, 0)
    up = jnp.where(rows == 0, prev_row, pltpu.roll(u, 1, axis=0))
    un = jnp.where(rows == TM - 1, next_row, pltpu.roll(u, TM - 1, axis=0))
    xx = 0.5 * (up + un) - u

    def mix(zi):
        return (u + xx * mu_ref[zi:zi + 1, :]).astype(BF16)

    r = jnp.dot(mix(0), wr_ref[...], preferred_element_type=F32)
    k = jnp.dot(mix(2), wk_ref[...], preferred_element_type=F32)
    v = jnp.dot(mix(3), wv_ref[...], preferred_element_type=F32)
    lw = w0_ref[...] + _bdot(jnp.tanh(jnp.dot(mix(1), w1_ref[...], preferred_element_type=F32)), w2_ref[...])
    decay = jnp.exp(-_sigmoid(lw) * float(np.exp(-0.5)))
    a = _sigmoid(a0_ref[...] + _bdot(jnp.dot(mix(4), a1_ref[...], preferred_element_type=F32), a2_ref[...]))
    g = _bdot(_sigmoid(jnp.dot(mix(5), g1_ref[...], preferred_element_type=F32)), g2_ref[...])
    kk = k * kk_ref_[...]
    kk = kk * lax.rsqrt(jnp.maximum(_seg_sum(kk * kk, sg_ref, sgt_ref), 1e-24))
    ka = ka_ref[...]
    kf = k * (1.0 + (a[:, :d] - 1.0) * ka)
    kb = k * (1.0 + (a[:, d:] - 1.0) * ka)
    bonus = _seg_sum(r * (kf + kb) * rk_ref[...], sg_ref, sgt_ref) * v
    wf_o[...] = decay[:, :d]
    wb_o[...] = decay[:, d:]
    kf_o[...] = kf
    kb_o[...] = kb
    bf_o[...] = kk * a[:, :d]
    bb_o[...] = kk * a[:, d:]
    r_o[...] = r
    v_o[...] = v
    kk_o[...] = kk
    g_o[...] = g
    bonus_o[...] = bonus


def _rwkv_prep(u, consts, nb):
    m, d = u.shape
    r8 = TM // 8
    nb8 = m // 8
    in_specs = [_tok_spec(d),
                pl.BlockSpec((8, d), lambda i: (jnp.maximum(i * r8 - 1, 0), 0)),
                pl.BlockSpec((8, d), lambda i: (jnp.minimum((i + 1) * r8, nb8 - 1), 0))]
    in_specs += [_full(c.shape) for c in consts]
    return pl.pallas_call(
        functools.partial(_rwkv_prep_kernel, nb=nb, d=d),
        grid=(m // TM,),
        in_specs=in_specs,
        out_specs=[_tok_spec(d)] * 11,
        out_shape=[jax.ShapeDtypeStruct((m, d), F32)] * 11,
        compiler_params=_cparams(("arbitrary",), VMEM_LIMIT),
        name="rwkv_prep",
    )(u, u, u, *consts)


RW_TC = 256


def _rwkv_scan_kernel(wf, kf, bf, rf, vf, kkf, wb, kb, bb, rb, vb, kkb, yf_ref, yb_ref,
                      st, vk_a, vk_b, ycol_scr, *, npair):
    s = pl.program_id(1)
    n = RW_HEAD_DIM
    grp = 8

    @pl.when(s == 0)
    def _():
        st[...] = jnp.zeros_like(st)
        ycol_scr[...] = jnp.zeros_like(ycol_scr)

    w2 = 2 * n
    ri = lax.broadcasted_iota(jnp.int32, (n, w2), 0)
    li = lax.broadcasted_iota(jnp.int32, (n, w2), 1)
    eye_a = (li == ri).astype(BF16)
    bi = lax.broadcasted_iota(jnp.int32, (w2, w2), 0)
    bj = lax.broadcasted_iota(jnp.int32, (w2, w2), 1)
    ones_blk = ((bi // n) == (bj // n)).astype(BF16)
    b4i = lax.broadcasted_iota(jnp.int32, (2 * w2, 2 * w2), 0)
    b4j = lax.broadcasted_iota(jnp.int32, (2 * w2, 2 * w2), 1)
    ones_blk2 = ((b4i // n) == (b4j // n)).astype(BF16)
    lane8 = lax.broadcasted_iota(jnp.int32, (grp, w2), 1)
    rows16 = lax.broadcasted_iota(jnp.int32, (2 * grp, w2), 0)
    lane16 = lax.broadcasted_iota(jnp.int32, (2 * grp, w2), 1)
    zpad = jnp.zeros((w2 - 2 * grp, w2), F32)
    untr = ((lane16 == rows16) | (lane16 == rows16 + n)).astype(BF16)
    dirs = ((wf, kf, bf, rf, vf, kkf, yf_ref), (wb, kb, bb, rb, vb, kkb, yb_ref))
    nch = 2 * npair
    ngrp = RW_TC // grp
    nt = (((1,), (1,)), ((), ()))

    def halves(tile):
        return jnp.concatenate([jnp.where(lane8 < n, tile, 0.0), jnp.where(lane8 >= n, tile, 0.0)], axis=0)

    assert nch == 2 * grp

    def group_base(g, dd):
        base = (g if dd == 0 else ngrp - 1 - g) * grp
        return base if isinstance(base, int) else pl.multiple_of(base, grp)

    def lanes(c):
        p = c % npair
        return slice(w2 * p, w2 * (p + 1))

    def precompute(g, chains, vk_dst):
        gq = jnp.minimum(g, ngrp - 1)
        v2, ks = [], []
        for c in chains:
            dd = c // npair
            base = group_base(gq, dd)
            v_t = dirs[dd][4][pl.ds(base, grp), lanes(c)]
            ks.append(dirs[dd][1][pl.ds(base, grp), lanes(c)])
            v2 += [jnp.where(lane8 < n, v_t, 0.0), jnp.where(lane8 < n, pltpu.roll(v_t, n, axis=1), 0.0)]
        vt = lax.dot_general(eye_a, jnp.concatenate(v2, axis=0).astype(BF16), nt,
                             preferred_element_type=F32).astype(BF16)
        for idx, c in enumerate(chains):
            k2 = halves(ks[idx])
            kblk = jnp.concatenate(
                [jnp.where((rows16 == j) | (rows16 == grp + j), k2, 0.0) for j in range(grp)], axis=1)
            vk_dst[c] = jnp.dot(vt[:, 2 * grp * idx:2 * grp * (idx + 1)], kblk.astype(BF16),
                                preferred_element_type=F32)

    def untranspose(g, chains):
        for c in chains:
            dd = c // npair
            yc = ycol_scr[c]
            ycs = jnp.concatenate([jnp.where(li < n, yc, 0.0), jnp.where(li >= n, yc, 0.0)], axis=0)
            yt = lax.dot_general(untr, ycs.astype(BF16), nt, preferred_element_type=F32)
            dirs[dd][6][pl.ds(group_base(g, dd), grp), lanes(c)] = yt[0:grp, :]

    def run_group(g, vk_cur, vk_next):
        tiles = []
        for c in range(nch):
            dd = c // npair
            base = group_base(g, dd)
            tiles.append([dirs[dd][q][pl.ds(base, grp), lanes(c)] for q in (0, 2, 3, 5)])
        gprev = jnp.maximum(g - 1, 0)
        states = [st[c] for c in range(nch)]
        ycol = [jnp.zeros((n, w2), F32) for _ in range(nch)]
        zero_half = jnp.zeros((n, w2), BF16)
        srp = [zero_half] * nch
        for j in range(grp + 1):
            lhs = []
            for c in range(nch):
                kk_t = tiles[c][3]
                jj = min(j, grp - 1) if c < npair else max(grp - 1 - j, 0)
                sk = (states[c] * jnp.broadcast_to(kk_t[jj:jj + 1, :], (n, w2))).astype(BF16)
                lhs.append(jnp.concatenate([sk, srp[c]], axis=1))
            red = jnp.dot(jnp.concatenate(lhs, axis=0), ones_blk2, preferred_element_type=F32)
            for c in range(nch):
                w_t, b_t, r_t, kk_t = tiles[c]
                jj = j if c < npair else grp - 1 - j
                jp = j - 1 if c < npair else grp - j
                row = lambda tl: jnp.broadcast_to(tl[jj:jj + 1, :], (n, w2))
                sa = red[c * n:(c + 1) * n, :w2]
                yb = red[c * n:(c + 1) * n, w2:]
                if j > 0:
                    ycol[c] = jnp.where((li % n) == jp, yb, ycol[c])
                if j < grp:
                    states[c] = states[c] * row(w_t) - sa * row(b_t) + vk_cur[c, :, jj * w2:(jj + 1) * w2]
                    srp[c] = (states[c] * row(r_t)).astype(BF16)
            if j < grp:
                pair = (2 * j, 2 * j + 1)
                precompute(g + 1, pair, vk_next)
                untranspose(gprev, pair)
        for c in range(nch):
            st[c] = states[c]
            ycol_scr[c] = ycol[c]

    precompute(0, tuple(range(nch)), vk_a)

    def two_groups(i, carry):
        run_group(2 * i, vk_a, vk_b)
        run_group(2 * i + 1, vk_b, vk_a)
        return carry

    lax.fori_loop(0, ngrp // 2, two_groups, 0)
    untranspose(ngrp - 1, tuple(range(nch)))


def _rwkv_scan(wf, kf, bf, wb, kb, bb, r, v, kk, nbatch):
    m, d = r.shape
    nch = m // nbatch // RW_TC
    npair = d // (2 * RW_HEAD_DIM)

    def fi(b, s):
        return (b * nch + s, 0)

    def bi(b, s):
        return (b * nch + jnp.where(s == 0, 0, nch - s), 0)

    sf = pl.BlockSpec((RW_TC, d), fi)
    sb = pl.BlockSpec((RW_TC, d), bi)
    return pl.pallas_call(
        functools.partial(_rwkv_scan_kernel, npair=npair),
        grid=(nbatch, nch),
        in_specs=[sf] * 6 + [sb] * 6,
        out_specs=[sf, sb],
        out_shape=[jax.ShapeDtypeStruct((m, d), F32)] * 2,
        scratch_shapes=[pltpu.VMEM((2 * npair, RW_HEAD_DIM, 2 * RW_HEAD_DIM), F32),
                        pltpu.VMEM((2 * npair, RW_HEAD_DIM, 8 * 2 * RW_HEAD_DIM), F32),
                        pltpu.VMEM((2 * npair, RW_HEAD_DIM, 8 * 2 * RW_HEAD_DIM), F32),
                        pltpu.VMEM((2 * npair, RW_HEAD_DIM, 2 * RW_HEAD_DIM), F32)],
        compiler_params=_cparams(("arbitrary", "arbitrary"), VMEM_LIMIT),
        name="rwkv_scan",
    )(wf, kf, bf, r, v, kk, wb, kb, bb, r, v, kk)


def _rwkv_out_kernel(x_ref, yf_ref, yb_ref, bonus_ref, g_ref, mod_ref, lg_ref, lb_ref, wo_ref, n2_ref,
                     sg_ref, sgt_ref, xo_ref, f_ref, ft_ref, *, d):
    y = yf_ref[...] + yb_ref[...]
    inv = 1.0 / RW_HEAD_DIM
    mean = _seg_sum(y, sg_ref, sgt_ref) * inv
    yc = y - mean
    var = _seg_sum(yc * yc, sg_ref, sgt_ref) * inv
    yn = yc * lax.rsqrt(var + RW_GN_EPS) * lg_ref[...] + lb_ref[...]
    out = (yn + bonus_ref[...]) * g_ref[...]
    mix = _bdot(out, wo_ref[...])
    _residual_epilogue(x_ref[...], mix, mod_ref, n2_ref, xo_ref, f_ref, ft_ref, d)


def _rwkv_out(xs, yf, yb, bonus, g, mod, lg, lb, wo, n2, sg, sgt, nb, nbatch):
    d = xs.shape[1]
    m = nbatch * (nb - 1) * TM
    out_specs, out_shape = _epilogue_specs(m, d)
    return pl.pallas_call(
        functools.partial(_rwkv_out_kernel, d=d),
        grid=(m // TM,),
        in_specs=[_lat_tok_spec(d, nb)] * 5 + [_mod_spec_lat(mod, nb - 1)]
        + [_full(a.shape) for a in (lg, lb, wo, n2, sg, sgt)],
        out_specs=out_specs, out_shape=out_shape,
        compiler_params=_cparams(("arbitrary",), VMEM_LIMIT),
        name="rwkv_out",
    )(xs, yf, yb, bonus, g, mod, lg, lb, wo, n2, sg, sgt)


def _rope_tables(t_lat, n_ctx):
    rows = t_lat // GRID_W
    row = jnp.broadcast_to(jnp.arange(rows, dtype=F32)[:, None], (rows, GRID_W)).reshape(-1)
    col = jnp.broadcast_to(jnp.arange(GRID_W, dtype=F32)[None, :], (rows, GRID_W)).reshape(-1)
    n_freq = SWA_HEAD_DIM // 4
    inv_freq = ROPE_BASE ** (-jnp.arange(n_freq, dtype=F32) / n_freq)
    ar = row[:, None] * inv_freq
    ac = col[:, None] * inv_freq
    cos = jnp.concatenate([jnp.cos(ar), jnp.cos(ar), jnp.cos(ac), jnp.cos(ac)], axis=1)
    sin = jnp.concatenate([-jnp.sin(ar), jnp.sin(ar), -jnp.sin(ac), jnp.sin(ac)], axis=1)
    cos = jnp.concatenate([jnp.ones((n_ctx, SWA_HEAD_DIM), F32), cos], axis=0)
    sin = jnp.concatenate([jnp.zeros((n_ctx, SWA_HEAD_DIM), F32), sin], axis=0)
    return jnp.tile(cos, (1, 2)), jnp.tile(sin, (1, 2))


def _peer(f, ft, w_q, k1, k2, u, v):
    wh = w_q.astype(BF16)
    wl = (w_q - wh.astype(F32)).astype(BF16)
    r2, cnt, e1, e2 = _peer_q(f, wh, wl, k1, k2)
    return _peer_dense(ft, u.astype(BF16), v.T.astype(BF16), r2, cnt, e1, e2)


def kernel(x, c, ctx, c_ctx, ada_w, ada_b, norm1_g, norm2_g, ab_w_in, gla_dec_w2, gla_dec_b, gla_norm_g,
           swa_sink, ab_w_out, rw_mu, rw_w_rkv, rw_w_o, rw_w0, rw_w1, rw_w2, rw_a0, rw_a1, rw_a2, rw_g1,
           rw_g2, rw_k_k, rw_k_a, rw_r_k, rw_ln_g, rw_ln_b, peer_w_q, peer_k1, peer_k2, peer_u, peer_v,
           final_g):
    nbatch, t_lat, d = x.shape
    n_ctx = ctx.shape[1]
    assert n_ctx == TM and t_lat % TM == 0 and d == 1024
    s_tok = n_ctx + t_lat
    nb = s_tok // TM
    m = nbatch * s_tok
    xs = jnp.concatenate([ctx, x], axis=1).reshape(m, d)

    cc = jnp.zeros((16, d), F32).at[:nbatch].set(c).at[nbatch].set(c_ctx)
    mod0 = _ada_table(cc, ada_w[0], ada_b[0]).reshape(16, 1, 6 * d)
    mod1 = _ada_table(cc, ada_w[1], ada_b[1]).reshape(16, 1, 6 * d)
    row2 = lambda a: a.reshape(1, -1)

    w_in = ab_w_in[0]
    cuts = np.cumsum((256, 256, 512, 512, 32, 512, 128, 128))[:-1].tolist()
    wgq, wgk, wgv, wgg, wlr, wsq, wsk, wsv = jnp.split(w_in, cuts, axis=1)
    wp = jnp.concatenate([wgq, wgk, wgv, wgg, wsq, wsk, wsv, wlr,
                          jnp.zeros((d, _P_END - _P_LR - 32), F32)], axis=1).astype(BF16)
    hk = GLA_HEADS * GLA_DK
    w2p = jnp.zeros((128, 2 * hk), F32)
    w2p = w2p.at[0:GLA_LOWRANK, 0:hk].set(gla_dec_w2[0, 0]).at[GLA_LOWRANK:2 * GLA_LOWRANK, hk:].set(gla_dec_w2[0, 1])
    db = gla_dec_b[0].reshape(1, 2 * hk)
    cos, sin = _rope_tables(t_lat, n_ctx)
    gq, gk, gv, gg, laf, lab, sq, sk, sv = _proj0(xs, mod0, row2(norm1_g[0]), wp, w2p, db, cos, sin, nb, nbatch)
    of, ob = _gla(gq, gk, gv, laf, lab, nbatch, n_ctx // GLA_CHUNK)
    sink = jnp.broadcast_to(swa_sink[0][:, None], (SWA_HEADS, 128))
    att = _swa(sq, sk, sv, sink, nbatch, n_ctx // WINDOW, t_lat // WINDOW)
    xs, f, ft = _out0(xs, of, ob, gg, att, mod0, row2(gla_norm_g[0]), ab_w_out[0].astype(BF16),
                      row2(norm2_g[0]), nb, nbatch)
    p = _peer(f, ft, peer_w_q[0], peer_k1[0], peer_k2[0], peer_u[0], peer_v[0])
    xs, u = _res(xs, p, mod0, mod1, row2(norm1_g[1]), nb, nbatch, final=False)

    lora = rw_w1.shape[-1]
    w1c = jnp.concatenate([rw_w1[0, 0], rw_w1[0, 1]], axis=1).astype(BF16)
    w2c = jnp.zeros((2 * lora, 2 * d), F32).at[:lora, :d].set(rw_w2[0, 0]).at[lora:, d:].set(rw_w2[0, 1]).astype(BF16)
    la = rw_a1.shape[-1]
    a1c = jnp.concatenate([rw_a1[0, 0], rw_a1[0, 1]], axis=1).astype(BF16)
    a2c = jnp.zeros((2 * la, 2 * d), F32).at[:la, :d].set(rw_a2[0, 0]).at[la:, d:].set(rw_a2[0, 1]).astype(BF16)
    lg = rw_g1.shape[-1]
    g1p = jnp.zeros((d, 256), F32).at[:, :lg].set(rw_g1[0]).astype(BF16)
    g2p = jnp.zeros((256, d), F32).at[:lg].set(rw_g2[0]).astype(BF16)
    nheads = d // RW_HEAD_DIM
    sg = (jnp.arange(d)[:, None] // RW_HEAD_DIM == jnp.arange(128)[None, :]).astype(BF16)
    sgt = sg.T
    consts = [rw_mu[0], rw_w_rkv[0, 0].astype(BF16), rw_w_rkv[0, 1].astype(BF16), rw_w_rkv[0, 2].astype(BF16),
              rw_w0[0].reshape(1, 2 * d), w1c, w2c, rw_a0[0].reshape(1, 2 * d), a1c, a2c, g1p, g2p,
              row2(rw_k_k[0]), row2(rw_k_a[0]), rw_r_k[0].reshape(1, d), sg, sgt]
    mu8 = jnp.zeros((8, d), F32).at[:6].set(rw_mu[0])
    consts[0] = mu8
    wf, wb, kf, kb, bf, bb, r, v, kk, g, bonus = _rwkv_prep(u, consts, nb)
    yf, yb = _rwkv_scan(wf, kf, bf, wb, kb, bb, r, v, kk, nbatch)
    xs, f, ft = _rwkv_out(xs, yf, yb, bonus, g, mod1, row2(rw_ln_g[0]), row2(rw_ln_b[0]),
                          rw_w_o[0].astype(BF16), row2(norm2_g[1]), sg, sgt, nb, nbatch)
    p = _peer(f, ft, peer_w_q[1], peer_k1[1], peer_k2[1], peer_u[1], peer_v[1])
    _, y = _res(xs, p, mod1, mod1, row2(final_g), nb, nbatch, final=True)
    return y.reshape(nbatch, t_lat, d)
```

```python
import functools

import numpy as np
import jax
import jax.numpy as jnp
from jax import lax
from jax.experimental import pallas as pl
from jax.experimental.pallas import tpu as pltpu

F32 = jnp.float32
BF16 = jnp.bfloat16
HI = lax.Precision.HIGHEST

NORM_EPS = 1e-6
GLA_HEADS, GLA_DK, GLA_DV, GLA_LOWRANK, GLA_TAU, GLA_CHUNK = 4, 64, 128, 16, 16.0, 64
SWA_HEADS, SWA_KV_HEADS, SWA_HEAD_DIM, WINDOW = 8, 2, 64, 128
ROPE_BASE = 10000.0
GRID_W = 64
RW_HEAD_DIM = 64
RW_GN_EPS = 64e-5
PEER_HEADS, PEER_NKEYS, PEER_TOPK = 8, 128, 16
NEG = -1e30

TM = 256
VMEM_LIMIT = 56 * 1024 * 1024


def _cparams(sem, vmem=None):
    return pltpu.CompilerParams(dimension_semantics=sem, vmem_limit_bytes=vmem)


def _bdot(a, b):
    return jnp.dot(a.astype(BF16), b.astype(BF16), preferred_element_type=F32)


def _hdot(a, b):
    return jnp.dot(a, b, precision=HI, preferred_element_type=F32)


def _sigmoid(x):
    return 1.0 / (1.0 + jnp.exp(-x))


def _rms(x):
    return x * lax.rsqrt(jnp.mean(x * x, axis=-1, keepdims=True) + NORM_EPS)


def _mod_rows(mod_ref, k, d):
    return mod_ref[0, :, k * d:(k + 1) * d]


def _mod_spec(mod, nb, nbatch):
    return pl.BlockSpec((1, 1, mod.shape[2]), lambda i: (jnp.where(i % nb == 0, nbatch, i // nb), 0, 0))


def _mod_spec_lat(mod, nbl):
    return pl.BlockSpec((1, 1, mod.shape[2]), lambda i: (i // nbl, 0, 0))


def _lat_tok_spec(n, nb):
    return pl.BlockSpec((TM, n), lambda i: ((i // (nb - 1)) * nb + 1 + i % (nb - 1), 0))


def _full(shape):
    n = len(shape)
    return pl.BlockSpec(shape, lambda *_: (0,) * n)


def _ada_kernel(c_ref, w_ref, b_ref, o_ref):
    c = c_ref[...]
    s = c * _sigmoid(c)
    o_ref[...] = _hdot(s, w_ref[...]) + b_ref[...]


def _ada_table(cc, w, b):
    rows, d = cc.shape
    n = w.shape[1]
    tn = 512
    return pl.pallas_call(
        _ada_kernel,
        grid=(n // tn,),
        in_specs=[_full((rows, d)), pl.BlockSpec((d, tn), lambda j: (0, j)),
                  pl.BlockSpec((1, tn), lambda j: (0, j))],
        out_specs=pl.BlockSpec((rows, tn), lambda j: (0, j)),
        out_shape=jax.ShapeDtypeStruct((rows, n), F32),
        compiler_params=_cparams(("arbitrary",)),
        name="ada_table",
    )(cc, w, b.reshape(1, n))


_P_GQ, _P_GK, _P_GV, _P_GG, _P_SQ, _P_SK, _P_SV, _P_LR, _P_END = 0, 256, 512, 1024, 1536, 2048, 2176, 2304, 2432


def _rope(x, cos, sin):
    lane = lax.broadcasted_iota(jnp.int32, x.shape, 1)
    up = pltpu.roll(x, 112, axis=1)
    dn = pltpu.roll(x, 16, axis=1)
    sw = jnp.where((lane % 32) < 16, up, dn)
    return x * cos + sw * sin


def _proj0_kernel(x_ref, mod_ref, g_ref, w_ref, w2_ref, db_ref, cos_ref, sin_ref,
                  gq_ref, gk_ref, gv_ref, gg_ref, laf_ref, lab_ref, sq_ref, sk_ref, sv_ref,
                  *, d):
    u = (_rms(x_ref[...]) * g_ref[...] * (1.0 + _mod_rows(mod_ref, 1, d))
         + _mod_rows(mod_ref, 0, d)).astype(BF16)

    def seg(a, b):
        return jnp.dot(u, w_ref[:, a:b], preferred_element_type=F32)

    gq_ref[...] = seg(_P_GQ, _P_GK) * (GLA_DK ** -0.5)
    gk_ref[...] = seg(_P_GK, _P_GV)
    gv_ref[...] = seg(_P_GV, _P_GG)
    gg_ref[...] = seg(_P_GG, _P_SQ)
    lr = seg(_P_LR, _P_END)
    z = _hdot(lr, w2_ref[...]) + db_ref[...]
    la = (jnp.minimum(z, 0.0) - jnp.log(1.0 + jnp.exp(-jnp.abs(z)))) * (1.0 / GLA_TAU)
    hk = GLA_HEADS * GLA_DK
    laf_ref[...] = la[:, :hk]
    lab_ref[...] = la[:, hk:]
    cos = cos_ref[...]
    sin = sin_ref[...]
    sq = seg(_P_SQ, _P_SK) * (SWA_HEAD_DIM ** -0.5)
    for c in range(4):
        sq_ref[:, 128 * c:128 * (c + 1)] = _rope(sq[:, 128 * c:128 * (c + 1)], cos, sin)
    sk_ref[...] = _rope(seg(_P_SK, _P_SV), cos, sin)
    sv_ref[...] = seg(_P_SV, _P_LR)


def _proj0(xs, mod, g, w, w2, db, cos, sin, nb, nbatch):
    m, d = xs.shape
    widths = (256, 256, 512, 512, 256, 256, 512, 128, 128)
    return pl.pallas_call(
        functools.partial(_proj0_kernel, d=d),
        grid=(m // TM,),
        in_specs=[pl.BlockSpec((TM, d), lambda i: (i, 0)), _mod_spec(mod, nb, nbatch), _full((1, d)),
                  _full(w.shape), _full(w2.shape), _full(db.shape),
                  pl.BlockSpec((TM, 128), lambda i: (i % nb, 0)),
                  pl.BlockSpec((TM, 128), lambda i: (i % nb, 0))],
        out_specs=[pl.BlockSpec((TM, n), lambda i: (i, 0)) for n in widths],
        out_shape=[jax.ShapeDtypeStruct((m, n), F32) for n in widths],
        compiler_params=_cparams(("arbitrary",), VMEM_LIMIT),
        name="proj0",
    )(xs, mod, g, w, w2, db, cos, sin)


def _gla_dir(q_ref, k_ref, v_ref, la_ref, o_ref, st, cum, mask, tot_row):
    c = GLA_CHUNK
    la = la_ref[...]
    la_hi = la.astype(BF16)
    la_lo = (la - la_hi.astype(F32)).astype(BF16)
    bc = (jnp.dot(cum, la_hi, preferred_element_type=F32)
          + jnp.dot(cum, la_lo, preferred_element_type=F32))
    tot = bc[tot_row:tot_row + 1, :]
    q_in = q_ref[...] * jnp.exp(bc)
    k_in = (k_ref[...] * jnp.exp(-bc)).astype(BF16)
    k_out = (k_ref[...] * jnp.exp(tot - bc)).astype(BF16)
    ones = jnp.ones((c, GLA_DV), BF16)
    tn = (((0,), (0,)), ((), ()))
    dcol = jnp.exp(lax.dot_general(la_hi, ones, tn, preferred_element_type=F32)
                   + lax.dot_general(la_lo, ones, tn, preferred_element_type=F32))
    lane = lax.broadcasted_iota(jnp.int32, q_in.shape, 1)
    s_prev = st[...].astype(BF16)
    for h in range(GLA_HEADS):
        qh = jnp.where((lane // GLA_DK) == h, q_in, 0.0).astype(BF16)
        att = lax.dot_general(qh, k_in, (((1,), (1,)), ((), ())), preferred_element_type=F32)
        att = jnp.where(mask, att, 0.0).astype(BF16)
        vh = v_ref[:, h * GLA_DV:(h + 1) * GLA_DV].astype(BF16)
        o = (jnp.dot(att, vh, preferred_element_type=F32)
             + jnp.dot(qh, s_prev, preferred_element_type=F32))
        o_ref[:, h * GLA_DV:(h + 1) * GLA_DV] = o
        upd = lax.dot_general(k_out, vh, (((0,), (0,)), ((), ())), preferred_element_type=F32)
        r0, r1 = h * GLA_DK, (h + 1) * GLA_DK
        st[r0:r1, :] = st[r0:r1, :] * dcol[r0:r1, :] + upd[r0:r1, :]


def _gla_kernel(qf, kf, vf, laf, qb, kb, vb, lab, of_ref, ob_ref, sf, sb):
    s = pl.program_id(1)

    @pl.when(s == 0)
    def _():
        sf[...] = jnp.zeros_like(sf)
        sb[...] = jnp.zeros_like(sb)

    c = GLA_CHUNK
    ri = lax.broadcasted_iota(jnp.int32, (c, c), 0)
    ci = lax.broadcasted_iota(jnp.int32, (c, c), 1)
    _gla_dir(qf, kf, vf, laf, of_ref, sf, (ri >= ci).astype(BF16), ri >= ci, c - 1)
    _gla_dir(qb, kb, vb, lab, ob_ref, sb, (ri <= ci).astype(BF16), ci > ri, 0)


def _gla(gq, gk, gv, laf, lab, nbatch, nctx_chunks):
    m = gq.shape[0]
    c = GLA_CHUNK
    nch = m // nbatch // c
    hk, hv = GLA_HEADS * GLA_DK, GLA_HEADS * GLA_DV

    def fi(b, s):
        return (b * nch + s, 0)

    def bi(b, s):
        cb = jnp.where(s < nctx_chunks, nctx_chunks - 1 - s, nch - 1 + nctx_chunks - s)
        return (b * nch + cb, 0)

    sk = lambda im: pl.BlockSpec((c, hk), im)
    sv = lambda im: pl.BlockSpec((c, hv), im)
    return pl.pallas_call(
        _gla_kernel,
        grid=(nbatch, nch),
        in_specs=[sk(fi), sk(fi), sv(fi), sk(fi), sk(bi), sk(bi), sv(bi), sk(bi)],
        out_specs=[sv(fi), sv(bi)],
        out_shape=[jax.ShapeDtypeStruct((m, hv), F32)] * 2,
        scratch_shapes=[pltpu.VMEM((hk, GLA_DV), F32), pltpu.VMEM((hk, GLA_DV), F32)],
        compiler_params=_cparams(("arbitrary", "arbitrary")),
        name="gla_scan",
    )(gq, gk, gv, laf, gq, gk, gv, lab)


def _swa_kernel(q_ref, kc_ref, vc_ref, kp_ref, kcur_ref, kn_ref, vp_ref, vcur_ref, vn_ref, sink_ref,
                o_ref, *, nctx_blocks, nlat_blocks):
    j = pl.program_id(1)
    w = WINDOW
    n = j - nctx_blocks
    is_lat = j >= nctx_blocks
    nkc = kc_ref.shape[0]
    kall = jnp.concatenate([kc_ref[...], kp_ref[...], kcur_ref[...], kn_ref[...]], axis=0)
    vall = jnp.concatenate([vc_ref[...], vp_ref[...], vcur_ref[...], vn_ref[...]], axis=0)
    nk = kall.shape[0]
    lane = lax.broadcasted_iota(jnp.int32, kall.shape, 1)
    lo = lane < SWA_HEAD_DIM
    kroll = pltpu.roll(kall, SWA_HEAD_DIM, axis=1)
    vroll = pltpu.roll(vall, SWA_HEAD_DIM, axis=1)
    k2 = [jnp.where(lo, kall, kroll).astype(BF16), jnp.where(lo, kroll, kall).astype(BF16)]
    v2 = [jnp.where(lo, vall, vroll).astype(BF16), jnp.where(lo, vroll, vall).astype(BF16)]
    rep = SWA_HEADS // SWA_KV_HEADS
    qi = lax.broadcasted_iota(jnp.int32, (rep * w, nk), 0) % w
    kj = lax.broadcasted_iota(jnp.int32, (rep * w, nk), 1)
    qpos = n * w + qi
    kpos = (n - 1) * w + (kj - nkc)
    loc_ok = (jnp.abs(qpos - kpos) <= WINDOW) & (kpos >= 0) & (kpos < nlat_blocks * w) & is_lat
    valid = (kj < nkc) | ((kj >= nkc) & loc_ok)
    qlane = lax.broadcasted_iota(jnp.int32, (w, 128), 1)
    qlo = qlane < SWA_HEAD_DIM
    outs = []
    for g in range(SWA_KV_HEADS):
        qs, sinks = [], []
        for r in range(rep):
            pr, half = divmod(g * rep + r, 2)
            qp = q_ref[:, 128 * pr:128 * (pr + 1)]
            qs.append(jnp.where(qlo if half == 0 else ~qlo, qp, 0.0).astype(BF16))
            sinks.append(jnp.broadcast_to(sink_ref[g * rep + r:g * rep + r + 1, 0:1], (w, 1)))
        sink = jnp.concatenate(sinks, axis=0)
        s = lax.dot_general(jnp.concatenate(qs, axis=0), k2[g], (((1,), (1,)), ((), ())),
                            preferred_element_type=F32)
        s = jnp.where(valid, s, NEG)
        mx = jnp.maximum(jnp.max(s, axis=-1, keepdims=True), sink)
        p = jnp.exp(s - mx)
        den = jnp.sum(p, axis=-1, keepdims=True) + jnp.exp(sink - mx)
        o = jnp.dot(p.astype(BF16), v2[g], preferred_element_type=F32) / den
        outs += [o[r * w:(r + 1) * w, :] for r in range(rep)]
    for pr in range(SWA_HEADS // 2):
        o_ref[:, 128 * pr:128 * (pr + 1)] = jnp.where(qlo, outs[2 * pr], outs[2 * pr + 1])


def _swa(sq, sk, sv, sink, nbatch, nctx_blocks, nlat_blocks):
    m = sq.shape[0]
    w = WINDOW
    nblk = nctx_blocks + nlat_blocks
    nkc = nctx_blocks * w

    def qmap(b, j):
        return (b * nblk + j, 0)

    def cmap(b, j):
        return (b, 0)

    def nmap(off):
        def f(b, j):
            n = jnp.clip(j - nctx_blocks + off, 0, nlat_blocks - 1)
            return (b * nblk + nctx_blocks + n, 0)
        return f

    kvw = sk.shape[1]
    kcs = pl.BlockSpec((nkc, kvw), lambda b, j: (b * (nblk * w // nkc), 0))
    kvs = lambda off: pl.BlockSpec((w, kvw), nmap(off))
    return pl.pallas_call(
        functools.partial(_swa_kernel, nctx_blocks=nctx_blocks, nlat_blocks=nlat_blocks),
        grid=(nbatch, nblk),
        in_specs=[pl.BlockSpec((w, sq.shape[1]), qmap), kcs, kcs, kvs(-1), kvs(0), kvs(1),
                  kvs(-1), kvs(0), kvs(1), _full(sink.shape)],
        out_specs=pl.BlockSpec((w, sq.shape[1]), qmap),
        out_shape=jax.ShapeDtypeStruct(sq.shape, F32),
        compiler_params=_cparams(("arbitrary", "arbitrary")),
        name="swa_attn",
    )(sq, sk, sv, sk, sk, sk, sv, sv, sv, sink)


def _residual_epilogue(x, mix, mod_ref, n2_ref, xo_ref, f_ref, ft_ref, d):
    xn = x + _mod_rows(mod_ref, 2, d) * mix
    xo_ref[...] = xn
    f = _rms(xn) * n2_ref[...] * (1.0 + _mod_rows(mod_ref, 4, d)) + _mod_rows(mod_ref, 3, d)
    f_ref[...] = f
    ft_ref[...] = f.T.astype(BF16)


def _out0_kernel(x_ref, of_ref, ob_ref, gg_ref, a_ref, mod_ref, gn_ref, wo_ref, n2_ref,
                 xo_ref, f_ref, ft_ref, *, d):
    o = of_ref[...] + ob_ref[...]
    gate = gg_ref[...]
    gate = gate * _sigmoid(gate)
    parts = []
    for h in range(GLA_HEADS):
        oh = o[:, h * GLA_DV:(h + 1) * GLA_DV]
        parts.append(oh * lax.rsqrt(jnp.mean(oh * oh, axis=-1, keepdims=True) + NORM_EPS))
    on = jnp.concatenate(parts, axis=1) * gn_ref[...] * gate
    hv = GLA_HEADS * GLA_DV
    mix = _bdot(on, wo_ref[0:hv, :]) + _bdot(a_ref[...], wo_ref[hv:, :])
    _residual_epilogue(x_ref[...], mix, mod_ref, n2_ref, xo_ref, f_ref, ft_ref, d)


def _tok_spec(n):
    return pl.BlockSpec((TM, n), lambda i: (i, 0))


def _epilogue_specs(m, d):
    out_specs = [_tok_spec(d), _tok_spec(d), pl.BlockSpec((d, TM), lambda i: (0, i))]
    out_shape = [jax.ShapeDtypeStruct((m, d), F32), jax.ShapeDtypeStruct((m, d), F32),
                 jax.ShapeDtypeStruct((d, m), BF16)]
    return out_specs, out_shape


def _out0(xs, of, ob, gg, a, mod, gn, wo, n2, nb, nbatch):
    m, d = xs.shape
    out_specs, out_shape = _epilogue_specs(m, d)
    return pl.pallas_call(
        functools.partial(_out0_kernel, d=d),
        grid=(m // TM,),
        in_specs=[_tok_spec(d), _tok_spec(of.shape[1]), _tok_spec(ob.shape[1]), _tok_spec(gg.shape[1]),
                  _tok_spec(a.shape[1]), _mod_spec(mod, nb, nbatch), _full(gn.shape), _full(wo.shape),
                  _full(n2.shape)],
        out_specs=out_specs, out_shape=out_shape,
        compiler_params=_cparams(("arbitrary",), VMEM_LIMIT),
        name="out0",
    )(xs, of, ob, gg, a, mod, gn, wo, n2)


NO_RANK = 99.0
RANK_STEP = 1e27


def _top_vals(s, k, want_rank=False):
    lw = 128
    ng = s.shape[1] // lw
    rows = lax.broadcasted_iota(jnp.int32, (k, lw), 0)
    cur = [s[:, q * lw:(q + 1) * lw] for q in range(ng)]
    vals = [jnp.zeros((k, lw), F32) for _ in range(ng)]
    for t in range(k):
        for q in range(ng):
            mx = jnp.max(cur[q], axis=0, keepdims=True)
            vals[q] = jnp.where(rows == t, mx, vals[q])
            cur[q] = jnp.where(cur[q] == mx, NEG - t * RANK_STEP, cur[q])
    vals = jnp.concatenate(vals, axis=1)
    if not want_rank:
        return vals
    cur = jnp.concatenate(cur, axis=1)
    rank = jnp.where(cur < 0.5 * NEG, jnp.floor((NEG - cur) * (1.0 / RANK_STEP) + 0.5), NO_RANK)
    return vals, rank


def _peer_scores_kernel(f_ref, wh_ref, wl_ref, kh_ref, kl_ref, s1_ref, s2_ref):
    f = f_ref[...]
    fh = f.astype(BF16)
    fl = (f - fh.astype(F32)).astype(BF16)
    wh = wh_ref[...]
    q = (jnp.dot(fh, wh, preferred_element_type=F32) + jnp.dot(fl, wh, preferred_element_type=F32)
         + jnp.dot(fh, wl_ref[...], preferred_element_type=F32))
    qh = q.astype(BF16)
    ql = (q - qh.astype(F32)).astype(BF16)
    nt = (((1,), (1,)), ((), ()))
    for h in range(PEER_HEADS):
        for z, s_ref in enumerate((s1_ref, s2_ref)):
            c0 = (2 * h + z) * PEER_NKEYS
            kh = kh_ref[z, h]
            qhs = qh[:, c0:c0 + PEER_NKEYS]
            s_ref[h] = (lax.dot_general(kh, qhs, nt, preferred_element_type=F32)
                        + lax.dot_general(kh, ql[:, c0:c0 + PEER_NKEYS], nt, preferred_element_type=F32)
                        + lax.dot_general(kl_ref[z, h], qhs, nt, preferred_element_type=F32))


def _peer_q_kernel(s1_ref, s2_ref, r2_ref, cnt_ref, e1_ref, e2_ref):
    kk = PEER_TOPK
    for h in range(PEER_HEADS):
        s1 = s1_ref[h]
        s2 = s2_ref[h]
        v1 = _top_vals(s1, kk)
        v2, rank2 = _top_vals(s2, kk, want_rank=True)
        row8 = lax.broadcasted_iota(jnp.int32, (8, s1.shape[1]), 0)
        blocks = [v1[0:1, :] + v2]
        for r in range(1, 8):
            blocks.append(jnp.where(row8 < kk // (r + 1), v1[r:r + 1, :] + v2[0:8, :], NEG))
        blocks.append(v1[8:kk, :] + v2[0:1, :])
        top = _top_vals(jnp.concatenate(blocks, axis=0), kk)
        tau = top[kk - 1:kk, :]
        z = jnp.sum(jnp.exp(top - top[0:1, :]), axis=0, keepdims=True)
        for q in range(s1.shape[1] // 128):
            ln = slice(128 * q, 128 * (q + 1))
            v2b = [jnp.broadcast_to(v2[c:c + 1, ln], (8, 128)) for c in range(kk)]
            taub = jnp.broadcast_to(tau[:, ln], (8, 128))
            for rb in range(PEER_NKEYS // 8):
                x = s1[8 * rb:8 * (rb + 1), ln]
                cnt = jnp.zeros((8, 128), F32)
                for c in range(kk):
                    cnt = cnt + jnp.where(x + v2b[c] >= taub, 1.0, 0.0)
                cnt_ref[h, 8 * rb:8 * (rb + 1), ln] = cnt
        r2_ref[h] = rank2.astype(BF16)
        e1_ref[h] = jnp.exp(s1 - v1[0:1, :])
        e2_ref[h] = (jnp.exp(s2 - v2[0:1, :]) / z).astype(BF16)


def _peer_q(f, wh, wl, k1, k2):
    m, d = f.shape
    hh, nk = PEER_HEADS, PEER_NKEYS
    big = pl.BlockSpec((hh, nk, TM), lambda i: (0, 0, i))
    sh32 = jax.ShapeDtypeStruct((hh, nk, m), F32)
    sh16 = jax.ShapeDtypeStruct((hh, nk, m), BF16)
    kf = jnp.stack([k1, k2])
    kh = kf.astype(BF16)
    kl = (kf - kh.astype(F32)).astype(BF16)
    s1, s2 = pl.pallas_call(
        _peer_scores_kernel,
        grid=(m // TM,),
        in_specs=[_tok_spec(d), _full(wh.shape), _full(wl.shape), _full(kh.shape), _full(kl.shape)],
        out_specs=[big, big],
        out_shape=[sh32, sh32],
        compiler_params=_cparams(("arbitrary",), VMEM_LIMIT),
        name="peer_scores",
    )(f, wh, wl, kh, kl)
    return pl.pallas_call(
        _peer_q_kernel,
        grid=(m // TM,),
        in_specs=[big, big],
        out_specs=[big, big, big, big],
        out_shape=[sh16, sh32, sh32, sh16],
        compiler_params=_cparams(("arbitrary",), VMEM_LIMIT),
        name="peer_query",
    )(s1, s2)


PD_TM = 512
PD_TE = 2048
PD_SUB = 1024


def _gelu(x):
    return 0.5 * x * (1.0 + lax.erf(x * (2.0 ** -0.5)))


def _row_bf16(tile, r, rows):
    one = jnp.broadcast_to(tile[r:r + 1, :], (8, tile.shape[1]))
    one = jnp.concatenate([one, one], axis=0).astype(BF16)
    return jnp.concatenate([one] * (rows // 16), axis=0)


def _run_if(cond, fn):
    def body(_, carry):
        fn()
        return carry
    lax.fori_loop(0, cond.astype(jnp.int32), body, 0)


def _peer_dense_kernel(ft_ref, u_ref, vt_ref, r2_ref, cnt_ref, e1_ref, e2_ref, o_ref, acc_ref):
    k = pl.program_id(1)
    nk = PEER_NKEYS
    nslab = PD_TE // nk
    per_sub = PD_SUB // nk
    zero = jnp.zeros((nk, PD_TM), BF16)
    nsub = PD_TE // PD_SUB

    def first():
        acc_ref[...] = jnp.zeros_like(acc_ref)

    _run_if(k == 0, first)

    def hidden(sb):
        return jnp.dot(u_ref[sb * PD_SUB:(sb + 1) * PD_SUB, :], ft_ref[...], preferred_element_type=F32)

    ht_next = hidden(0)
    for sb in range(nsub):
        e0 = sb * PD_SUB
        ht = ht_next
        if sb + 1 < nsub:
            ht_next = hidden(sb + 1)
        gates = []
        for al in range(sb * per_sub, (sb + 1) * per_sub):
            wg = zero
            a0 = pl.multiple_of(k * nslab + 8 * (al // 8), 8)
            for h in range(PEER_HEADS):
                cnt = _row_bf16(cnt_ref[h, pl.ds(a0, 8), :], al % 8, nk)
                e1 = _row_bf16(e1_ref[h, pl.ds(a0, 8), :], al % 8, nk)
                wg = wg + jnp.where(r2_ref[h] < cnt, e2_ref[h], zero) * e1
            gates.append(wg)
        ct = _gelu(ht).astype(BF16) * jnp.concatenate(gates, axis=0)
        acc_ref[...] += jnp.dot(vt_ref[:, e0:e0 + PD_SUB], ct, preferred_element_type=F32)

    def last():
        o_ref[...] = acc_ref[...].T

    _run_if(k == pl.num_programs(1) - 1, last)


def _peer_dense(ft, u, vt, layer, r2, cnt, e1, e2):
    d, m = ft.shape
    ne = u.shape[1]
    hh, nk = PEER_HEADS, PEER_NKEYS
    big = pl.BlockSpec((hh, nk, PD_TM), lambda i, k: (0, 0, i))
    return pl.pallas_call(
        _peer_dense_kernel,
        grid=(m // PD_TM, ne // PD_TE),
        in_specs=[pl.BlockSpec((d, PD_TM), lambda i, k: (0, i)),
                  pl.BlockSpec((None, PD_TE, d), lambda i, k: (layer, k, 0)),
                  pl.BlockSpec((None, d, PD_TE), lambda i, k: (layer, 0, k)),
                  big, big, big, big],
        out_specs=pl.BlockSpec((PD_TM, d), lambda i, k: (i, 0)),
        out_shape=jax.ShapeDtypeStruct((m, d), F32),
        scratch_shapes=[pltpu.VMEM((d, PD_TM), F32)],
        compiler_params=_cparams(("arbitrary", "arbitrary"), VMEM_LIMIT),
        name="peer_dense",
    )(ft, u, vt, r2, cnt, e1, e2)


def _res_kernel(x_ref, p_ref, mod_ref, modn_ref, g_ref, xo_ref, u_ref, *, d, final):
    xn = x_ref[...] + _mod_rows(mod_ref, 5, d) * p_ref[...]
    xo_ref[...] = xn
    y = _rms(xn) * g_ref[...]
    if not final:
        y = y * (1.0 + _mod_rows(modn_ref, 1, d)) + _mod_rows(modn_ref, 0, d)
    u_ref[...] = y


def _res(xs, p, mod, modn, g, nb, nbatch, final):
    m, d = xs.shape
    mspec = (lambda t: _mod_spec_lat(t, nb - 1)) if final else (lambda t: _mod_spec(t, nb, nbatch))
    return pl.pallas_call(
        functools.partial(_res_kernel, d=d, final=final),
        grid=(m // TM,),
        in_specs=[_tok_spec(d), _tok_spec(d), mspec(mod), mspec(modn), _full(g.shape)],
        out_specs=[_tok_spec(d), _tok_spec(d)],
        out_shape=[jax.ShapeDtypeStruct((m, d), F32)] * 2,
        compiler_params=_cparams(("arbitrary",)),
        name="peer_residual",
    )(xs, p, mod, modn, g)


def _split_dot(x, w):
    hi = x.astype(BF16)
    lo = (x - hi.astype(F32)).astype(BF16)
    return jnp.dot(hi, w, preferred_element_type=F32) + jnp.dot(lo, w, preferred_element_type=F32)


def _seg_sum(x, g_ref, gt_ref):
    return _split_dot(_split_dot(x, g_ref[...]), gt_ref[...])


def _rwkv_prep_kernel(u_ref, up_ref, un_ref, mu_ref, wr_ref, wk_ref, wv_ref, w0_ref, w1_ref, w2_ref,
                      a0_ref, a1_ref, a2_ref, g1_ref, g2_ref, kk_ref_, ka_ref, rk_ref, sg_ref, sgt_ref,
                      wf_o, wb_o, kf_o, kb_o, bf_o, bb_o, r_o, v_o, kk_o, g_o, bonus_o, *, nb, d):
    i = pl.program_id(0)
    p = i % nb
    u = u_ref[...]
    has_prev = (p >= 2).astype(F32)
    has_next = ((p >= 1) & (p <= nb - 2)).astype(F32)
    prev_row = up_ref[7:8, :] * has_prev
    next_row = un_ref[0:1, :] * has_next
    rows = lax.broadcasted_iota(jnp.int32, u.shape, 0)
    up = jnp.where(rows == 0, prev_row, pltpu.roll(u, 1, axis=0))
    un = jnp.where(rows == TM - 1, next_row, pltpu.roll(u, TM - 1, axis=0))
    xx = 0.5 * (up + un) - u

    def mix(zi):
        return (u + xx * mu_ref[zi:zi + 1, :]).astype(BF16)

    r = jnp.dot(mix(0), wr_ref[...], preferred_element_type=F32)
    k = jnp.dot(mix(2), wk_ref[...], preferred_element_type=F32)
    v = jnp.dot(mix(3), wv_ref[...], preferred_element_type=F32)
    lw = w0_ref[...] + _bdot(jnp.tanh(jnp.dot(mix(1), w1_ref[...], preferred_element_type=F32)), w2_ref[...])
    decay = jnp.exp(-_sigmoid(lw) * float(np.exp(-0.5)))
    a = _sigmoid(a0_ref[...] + _bdot(jnp.dot(mix(4), a1_ref[...], preferred_element_type=F32), a2_ref[...]))
    g = _bdot(_sigmoid(jnp.dot(mix(5), g1_ref[...], preferred_element_type=F32)), g2_ref[...])
    kk = k * kk_ref_[...]
    kk = kk * lax.rsqrt(jnp.maximum(_seg_sum(kk * kk, sg_ref, sgt_ref), 1e-24))
    ka = ka_ref[...]
    kf = k * (1.0 + (a[:, :d] - 1.0) * ka)
    kb = k * (1.0 + (a[:, d:] - 1.0) * ka)
    bonus = _seg_sum(r * (kf + kb) * rk_ref[...], sg_ref, sgt_ref) * v
    wf_o[...] = decay[:, :d]
    wb_o[...] = decay[:, d:]
    kf_o[...] = kf
    kb_o[...] = kb
    bf_o[...] = kk * a[:, :d]
    bb_o[...] = kk * a[:, d:]
    r_o[...] = r
    v_o[...] = v
    kk_o[...] = kk
    g_o[...] = g
    bonus_o[...] = bonus


def _rwkv_prep(u, consts, nb):
    m, d = u.shape
    r8 = TM // 8
    nb8 = m // 8
    in_specs = [_tok_spec(d),
                pl.BlockSpec((8, d), lambda i: (jnp.maximum(i * r8 - 1, 0), 0)),
                pl.BlockSpec((8, d), lambda i: (jnp.minimum((i + 1) * r8, nb8 - 1), 0))]
    in_specs += [_full(c.shape) for c in consts]
    return pl.pallas_call(
        functools.partial(_rwkv_prep_kernel, nb=nb, d=d),
        grid=(m // TM,),
        in_specs=in_specs,
        out_specs=[_tok_spec(d)] * 11,
        out_shape=[jax.ShapeDtypeStruct((m, d), F32)] * 11,
        compiler_params=_cparams(("arbitrary",), VMEM_LIMIT),
        name="rwkv_prep",
    )(u, u, u, *consts)


RW_TC = 256


def _rwkv_scan_kernel(wf, kf, bf, rf, vf, kkf, wb, kb, bb, rb, vb, kkb, yf_ref, yb_ref,
                      st, vk_a, vk_b, ycol_scr, sa_scr, *, npair):
    s = pl.program_id(1)
    n = RW_HEAD_DIM
    grp = 8

    @pl.when(s == 0)
    def _():
        st[...] = jnp.zeros_like(st)
        ycol_scr[...] = jnp.zeros_like(ycol_scr)

    w2 = 2 * n
    ri = lax.broadcasted_iota(jnp.int32, (n, w2), 0)
    li = lax.broadcasted_iota(jnp.int32, (n, w2), 1)
    eye_a = (li == ri).astype(BF16)
    bi = lax.broadcasted_iota(jnp.int32, (w2, w2), 0)
    bj = lax.broadcasted_iota(jnp.int32, (w2, w2), 1)
    ones_blk = ((bi // n) == (bj // n)).astype(BF16)
    b4i = lax.broadcasted_iota(jnp.int32, (2 * w2, 2 * w2), 0)
    b4j = lax.broadcasted_iota(jnp.int32, (2 * w2, 2 * w2), 1)
    ones_blk2 = ((b4i // n) == (b4j // n)).astype(BF16)
    lane8 = lax.broadcasted_iota(jnp.int32, (grp, w2), 1)
    rows16 = lax.broadcasted_iota(jnp.int32, (2 * grp, w2), 0)
    lane16 = lax.broadcasted_iota(jnp.int32, (2 * grp, w2), 1)
    zpad = jnp.zeros((w2 - 2 * grp, w2), F32)
    untr = ((lane16 == rows16) | (lane16 == rows16 + n)).astype(BF16)
    dirs = ((wf, kf, bf, rf, vf, kkf, yf_ref), (wb, kb, bb, rb, vb, kkb, yb_ref))
    nch = 2 * npair
    ngrp = RW_TC // grp
    nt = (((1,), (1,)), ((), ()))

    def halves(tile):
        return jnp.concatenate([jnp.where(lane8 < n, tile, 0.0), jnp.where(lane8 >= n, tile, 0.0)], axis=0)

    assert nch == 2 * grp

    def group_base(g, dd):
        base = (g if dd == 0 else ngrp - 1 - g) * grp
        return base if isinstance(base, int) else pl.multiple_of(base, grp)

    def lanes(c):
        p = c % npair
        return slice(w2 * p, w2 * (p + 1))

    def precompute(g, chains, vk_dst):
        gq = jnp.minimum(g, ngrp - 1)
        v2, ks = [], []
        for c in chains:
            dd = c // npair
            base = group_base(gq, dd)
            v_t = dirs[dd][4][pl.ds(base, grp), lanes(c)]
            ks.append(dirs[dd][1][pl.ds(base, grp), lanes(c)])
            v2 += [jnp.where(lane8 < n, v_t, 0.0), jnp.where(lane8 < n, pltpu.roll(v_t, n, axis=1), 0.0)]
        vt = lax.dot_general(eye_a, jnp.concatenate(v2, axis=0).astype(BF16), nt,
                             preferred_element_type=F32).astype(BF16)
        for idx, c in enumerate(chains):
            k2 = halves(ks[idx])
            kblk = jnp.concatenate(
                [jnp.where((rows16 == j) | (rows16 == grp + j), k2, 0.0) for j in range(grp)], axis=1)
            vk_dst[c] = jnp.dot(vt[:, 2 * grp * idx:2 * grp * (idx + 1)], kblk.astype(BF16),
                                preferred_element_type=F32)

    def untranspose(g, chains):
        for c in chains:
            dd = c // npair
            yc = ycol_scr[c]
            ycs = jnp.concatenate([jnp.where(li < n, yc, 0.0), jnp.where(li >= n, yc, 0.0)], axis=0)
            yt = lax.dot_general(untr, ycs.astype(BF16), nt, preferred_element_type=F32)
            dirs[dd][6][pl.ds(group_base(g, dd), grp), lanes(c)] = yt[0:grp, :]

    def run_group(g, vk_cur, vk_next):
        tiles = []
        for c in range(nch):
            dd = c // npair
            base = group_base(g, dd)
            tiles.append([dirs[dd][q][pl.ds(base, grp), lanes(c)] for q in (0, 2, 3, 5)])
        gprev = jnp.maximum(g - 1, 0)
        gnext = jnp.minimum(g + 1, ngrp - 1)
        kap_next = []
        for c in range(nch):
            dd = c // npair
            kap_next.append(dirs[dd][5][pl.ds(group_base(gnext, dd), grp), lanes(c)])
        states = [st[c] for c in range(nch)]
        sa = [sa_scr[c] for c in range(nch)]
        ycol = [jnp.zeros((n, w2), F32) for _ in range(nch)]
        for j in range(grp):
            lhs = []
            for c in range(nch):
                w_t, b_t, r_t, kk_t = tiles[c]
                fwd = c < npair
                jj = j if fwd else grp - 1 - j
                row = lambda tl, q: jnp.broadcast_to(tl[q:q + 1, :], (n, w2))
                states[c] = (states[c] * row(w_t, jj) - sa[c] * row(b_t, jj)
                             + vk_cur[c, :, jj * w2:(jj + 1) * w2])
                if j + 1 < grp:
                    kap = row(kk_t, jj + 1 if fwd else jj - 1)
                else:
                    kap = row(kap_next[c], 0 if fwd else grp - 1)
                lhs.append(jnp.concatenate([(states[c] * kap).astype(BF16),
                                            (states[c] * row(r_t, jj)).astype(BF16)], axis=1))
            red = jnp.dot(jnp.concatenate(lhs, axis=0), ones_blk2, preferred_element_type=F32)
            for c in range(nch):
                jj = j if c < npair else grp - 1 - j
                sa[c] = red[c * n:(c + 1) * n, :w2]
                ycol[c] = jnp.where((li % n) == jj, red[c * n:(c + 1) * n, w2:], ycol[c])
            pair = (2 * j, 2 * j + 1)
            precompute(g + 1, pair, vk_next)
            untranspose(gprev, pair)
        for c in range(nch):
            st[c] = states[c]
            sa_scr[c] = sa[c]
            ycol_scr[c] = ycol[c]

    precompute(0, tuple(range(nch)), vk_a)
    first = []
    for c in range(nch):
        dd = c // npair
        kk_t = dirs[dd][5][pl.ds(group_base(0, dd), grp), lanes(c)]
        q0 = 0 if dd == 0 else grp - 1
        first.append((st[c] * jnp.broadcast_to(kk_t[q0:q0 + 1, :], (n, w2))).astype(BF16))
    sa0 = jnp.dot(jnp.concatenate(first, axis=0), ones_blk, preferred_element_type=F32)
    for c in range(nch):
        sa_scr[c] = sa0[c * n:(c + 1) * n, :]

    def two_groups(i, carry):
        run_group(2 * i, vk_a, vk_b)
        run_group(2 * i + 1, vk_b, vk_a)
        return carry

    lax.fori_loop(0, ngrp // 2, two_groups, 0)
    untranspose(ngrp - 1, tuple(range(nch)))


def _rwkv_scan(wf, kf, bf, wb, kb, bb, r, v, kk, nbatch):
    m, d = r.shape
    nch = m // nbatch // RW_TC
    npair = d // (2 * RW_HEAD_DIM)

    def fi(b, s):
        return (b * nch + s, 0)

    def bi(b, s):
        return (b * nch + jnp.where(s == 0, 0, nch - s), 0)

    sf = pl.BlockSpec((RW_TC, d), fi)
    sb = pl.BlockSpec((RW_TC, d), bi)
    return pl.pallas_call(
        functools.partial(_rwkv_scan_kernel, npair=npair),
        grid=(nbatch, nch),
        in_specs=[sf] * 6 + [sb] * 6,
        out_specs=[sf, sb],
        out_shape=[jax.ShapeDtypeStruct((m, d), F32)] * 2,
        scratch_shapes=[pltpu.VMEM((2 * npair, RW_HEAD_DIM, 2 * RW_HEAD_DIM), F32),
                        pltpu.VMEM((2 * npair, RW_HEAD_DIM, 8 * 2 * RW_HEAD_DIM), F32),
                        pltpu.VMEM((2 * npair, RW_HEAD_DIM, 8 * 2 * RW_HEAD_DIM), F32),
                        pltpu.VMEM((2 * npair, RW_HEAD_DIM, 2 * RW_HEAD_DIM), F32),
                        pltpu.VMEM((2 * npair, RW_HEAD_DIM, 2 * RW_HEAD_DIM), F32)],
        compiler_params=_cparams(("arbitrary", "arbitrary"), VMEM_LIMIT),
        name="rwkv_scan",
    )(wf, kf, bf, r, v, kk, wb, kb, bb, r, v, kk)


def _rwkv_out_kernel(x_ref, yf_ref, yb_ref, bonus_ref, g_ref, mod_ref, lg_ref, lb_ref, wo_ref, n2_ref,
                     sg_ref, sgt_ref, xo_ref, f_ref, ft_ref, *, d):
    y = yf_ref[...] + yb_ref[...]
    inv = 1.0 / RW_HEAD_DIM
    mean = _seg_sum(y, sg_ref, sgt_ref) * inv
    yc = y - mean
    var = _seg_sum(yc * yc, sg_ref, sgt_ref) * inv
    yn = yc * lax.rsqrt(var + RW_GN_EPS) * lg_ref[...] + lb_ref[...]
    out = (yn + bonus_ref[...]) * g_ref[...]
    mix = _bdot(out, wo_ref[...])
    _residual_epilogue(x_ref[...], mix, mod_ref, n2_ref, xo_ref, f_ref, ft_ref, d)


def _rwkv_out(xs, yf, yb, bonus, g, mod, lg, lb, wo, n2, sg, sgt, nb, nbatch):
    d = xs.shape[1]
    m = nbatch * (nb - 1) * TM
    out_specs, out_shape = _epilogue_specs(m, d)
    return pl.pallas_call(
        functools.partial(_rwkv_out_kernel, d=d),
        grid=(m // TM,),
        in_specs=[_lat_tok_spec(d, nb)] * 5 + [_mod_spec_lat(mod, nb - 1)]
        + [_full(a.shape) for a in (lg, lb, wo, n2, sg, sgt)],
        out_specs=out_specs, out_shape=out_shape,
        compiler_params=_cparams(("arbitrary",), VMEM_LIMIT),
        name="rwkv_out",
    )(xs, yf, yb, bonus, g, mod, lg, lb, wo, n2, sg, sgt)


def _rope_tables(t_lat, n_ctx):
    rows = t_lat // GRID_W
    row = jnp.broadcast_to(jnp.arange(rows, dtype=F32)[:, None], (rows, GRID_W)).reshape(-1)
    col = jnp.broadcast_to(jnp.arange(GRID_W, dtype=F32)[None, :], (rows, GRID_W)).reshape(-1)
    n_freq = SWA_HEAD_DIM // 4
    inv_freq = ROPE_BASE ** (-jnp.arange(n_freq, dtype=F32) / n_freq)
    ar = row[:, None] * inv_freq
    ac = col[:, None] * inv_freq
    cos = jnp.concatenate([jnp.cos(ar), jnp.cos(ar), jnp.cos(ac), jnp.cos(ac)], axis=1)
    sin = jnp.concatenate([-jnp.sin(ar), jnp.sin(ar), -jnp.sin(ac), jnp.sin(ac)], axis=1)
    cos = jnp.concatenate([jnp.ones((n_ctx, SWA_HEAD_DIM), F32), cos], axis=0)
    sin = jnp.concatenate([jnp.zeros((n_ctx, SWA_HEAD_DIM), F32), sin], axis=0)
    return jnp.tile(cos, (1, 2)), jnp.tile(sin, (1, 2))


def _peer(f, ft, w_q, k1, k2, u_all, vt_all, layer):
    wh = w_q.astype(BF16)
    wl = (w_q - wh.astype(F32)).astype(BF16)
    r2, cnt, e1, e2 = _peer_q(f, wh, wl, k1, k2)
    return _peer_dense(ft, u_all, vt_all, layer, r2, cnt, e1, e2)


def kernel(x, c, ctx, c_ctx, ada_w, ada_b, norm1_g, norm2_g, ab_w_in, gla_dec_w2, gla_dec_b, gla_norm_g,
           swa_sink, ab_w_out, rw_mu, rw_w_rkv, rw_w_o, rw_w0, rw_w1, rw_w2, rw_a0, rw_a1, rw_a2, rw_g1,
           rw_g2, rw_k_k, rw_k_a, rw_r_k, rw_ln_g, rw_ln_b, peer_w_q, peer_k1, peer_k2, peer_u, peer_v,
           final_g):
    nbatch, t_lat, d = x.shape
    n_ctx = ctx.shape[1]
    assert n_ctx == TM and t_lat % TM == 0 and d == 1024
    s_tok = n_ctx + t_lat
    nb = s_tok // TM
    m = nbatch * s_tok
    xs = jnp.concatenate([ctx, x], axis=1).reshape(m, d)

    cc = jnp.zeros((16, d), F32).at[:nbatch].set(c).at[nbatch].set(c_ctx)
    mod0 = _ada_table(cc, ada_w[0], ada_b[0]).reshape(16, 1, 6 * d)
    mod1 = _ada_table(cc, ada_w[1], ada_b[1]).reshape(16, 1, 6 * d)
    row2 = lambda a: a.reshape(1, -1)

    w_in = ab_w_in[0]
    cuts = np.cumsum((256, 256, 512, 512, 32, 512, 128, 128))[:-1].tolist()
    wgq, wgk, wgv, wgg, wlr, wsq, wsk, wsv = jnp.split(w_in, cuts, axis=1)
    wp = jnp.concatenate([wgq, wgk, wgv, wgg, wsq, wsk, wsv, wlr,
                          jnp.zeros((d, _P_END - _P_LR - 32), F32)], axis=1).astype(BF16)
    hk = GLA_HEADS * GLA_DK
    w2p = jnp.zeros((128, 2 * hk), F32)
    w2p = w2p.at[0:GLA_LOWRANK, 0:hk].set(gla_dec_w2[0, 0]).at[GLA_LOWRANK:2 * GLA_LOWRANK, hk:].set(gla_dec_w2[0, 1])
    db = gla_dec_b[0].reshape(1, 2 * hk)
    cos, sin = _rope_tables(t_lat, n_ctx)
    gq, gk, gv, gg, laf, lab, sq, sk, sv = _proj0(xs, mod0, row2(norm1_g[0]), wp, w2p, db, cos, sin, nb, nbatch)
    of, ob = _gla(gq, gk, gv, laf, lab, nbatch, n_ctx // GLA_CHUNK)
    sink = jnp.broadcast_to(swa_sink[0][:, None], (SWA_HEADS, 128))
    att = _swa(sq, sk, sv, sink, nbatch, n_ctx // WINDOW, t_lat // WINDOW)
    xs, f, ft = _out0(xs, of, ob, gg, att, mod0, row2(gla_norm_g[0]), ab_w_out[0].astype(BF16),
                      row2(norm2_g[0]), nb, nbatch)
    u_all = peer_u.astype(BF16)
    vt_all = jnp.swapaxes(peer_v, 1, 2).astype(BF16)
    p = _peer(f, ft, peer_w_q[0], peer_k1[0], peer_k2[0], u_all, vt_all, 0)
    xs, u = _res(xs, p, mod0, mod1, row2(norm1_g[1]), nb, nbatch, final=False)

    lora = rw_w1.shape[-1]
    w1c = jnp.concatenate([rw_w1[0, 0], rw_w1[0, 1]], axis=1).astype(BF16)
    w2c = jnp.zeros((2 * lora, 2 * d), F32).at[:lora, :d].set(rw_w2[0, 0]).at[lora:, d:].set(rw_w2[0, 1]).astype(BF16)
    la = rw_a1.shape[-1]
    a1c = jnp.concatenate([rw_a1[0, 0], rw_a1[0, 1]], axis=1).astype(BF16)
    a2c = jnp.zeros((2 * la, 2 * d), F32).at[:la, :d].set(rw_a2[0, 0]).at[la:, d:].set(rw_a2[0, 1]).astype(BF16)
    lg = rw_g1.shape[-1]
    g1p = jnp.zeros((d, 256), F32).at[:, :lg].set(rw_g1[0]).astype(BF16)
    g2p = jnp.zeros((256, d), F32).at[:lg].set(rw_g2[0]).astype(BF16)
    nheads = d // RW_HEAD_DIM
    sg = (jnp.arange(d)[:, None] // RW_HEAD_DIM == jnp.arange(128)[None, :]).astype(BF16)
    sgt = sg.T
    consts = [rw_mu[0], rw_w_rkv[0, 0].astype(BF16), rw_w_rkv[0, 1].astype(BF16), rw_w_rkv[0, 2].astype(BF16),
              rw_w0[0].reshape(1, 2 * d), w1c, w2c, rw_a0[0].reshape(1, 2 * d), a1c, a2c, g1p, g2p,
              row2(rw_k_k[0]), row2(rw_k_a[0]), rw_r_k[0].reshape(1, d), sg, sgt]
    mu8 = jnp.zeros((8, d), F32).at[:6].set(rw_mu[0])
    consts[0] = mu8
    wf, wb, kf, kb, bf, bb, r, v, kk, g, bonus = _rwkv_prep(u, consts, nb)
    yf, yb = _rwkv_scan(wf, kf, bf, wb, kb, bb, r, v, kk, nbatch)
    xs, f, ft = _rwkv_out(xs, yf, yb, bonus, g, mod1, row2(rw_ln_g[0]), row2(rw_ln_b[0]),
                          rw_w_o[0].astype(BF16), row2(norm2_g[1]), sg, sgt, nb, nbatch)
    p = _peer(f, ft, peer_w_q[1], peer_k1[1], peer_k2[1], u_all, vt_all, 1)
    _, y = _res(xs, p, mod1, mod1, row2(final_g), nb, nbatch, final=True)
    return y.reshape(nbatch, t_lat, d)
```

```python
import functools

import numpy as np
import jax
import jax.numpy as jnp
from jax import lax
from jax.experimental import pallas as pl
from jax.experimental.pallas import tpu as pltpu

F32 = jnp.float32
BF16 = jnp.bfloat16
HI = lax.Precision.HIGHEST

NORM_EPS = 1e-6
GLA_HEADS, GLA_DK, GLA_DV, GLA_LOWRANK, GLA_TAU, GLA_CHUNK = 4, 64, 128, 16, 16.0, 64
SWA_HEADS, SWA_KV_HEADS, SWA_HEAD_DIM, WINDOW = 8, 2, 64, 128
ROPE_BASE = 10000.0
GRID_W = 64
RW_HEAD_DIM = 64
RW_GN_EPS = 64e-5
PEER_HEADS, PEER_NKEYS, PEER_TOPK = 8, 128, 16
NEG = -1e30

TM = 256
VMEM_LIMIT = 56 * 1024 * 1024


def _cparams(sem, vmem=None):
    return pltpu.CompilerParams(dimension_semantics=sem, vmem_limit_bytes=vmem)


def _bdot(a, b):
    return jnp.dot(a.astype(BF16), b.astype(BF16), preferred_element_type=F32)


def _hdot(a, b):
    return jnp.dot(a, b, precision=HI, preferred_element_type=F32)


def _sigmoid(x):
    return 1.0 / (1.0 + jnp.exp(-x))


def _rms(x):
    return x * lax.rsqrt(jnp.mean(x * x, axis=-1, keepdims=True) + NORM_EPS)


def _mod_rows(mod_ref, k, d):
    return mod_ref[0, :, k * d:(k + 1) * d]


def _mod_spec(mod, nb, nbatch):
    return pl.BlockSpec((1, 1, mod.shape[2]), lambda i: (jnp.where(i % nb == 0, nbatch, i // nb), 0, 0))


def _mod_spec_lat(mod, nbl):
    return pl.BlockSpec((1, 1, mod.shape[2]), lambda i: (i // nbl, 0, 0))


def _lat_tok_spec(n, nb):
    return pl.BlockSpec((TM, n), lambda i: ((i // (nb - 1)) * nb + 1 + i % (nb - 1), 0))


def _full(shape):
    n = len(shape)
    return pl.BlockSpec(shape, lambda *_: (0,) * n)


def _ada_kernel(c_ref, w_ref, b_ref, o_ref):
    c = c_ref[...]
    s = c * _sigmoid(c)
    o_ref[...] = _hdot(s, w_ref[...]) + b_ref[...]


def _ada_table(cc, w, b):
    rows, d = cc.shape
    n = w.shape[1]
    tn = 512
    return pl.pallas_call(
        _ada_kernel,
        grid=(n // tn,),
        in_specs=[_full((rows, d)), pl.BlockSpec((d, tn), lambda j: (0, j)),
                  pl.BlockSpec((1, tn), lambda j: (0, j))],
        out_specs=pl.BlockSpec((rows, tn), lambda j: (0, j)),
        out_shape=jax.ShapeDtypeStruct((rows, n), F32),
        compiler_params=_cparams(("arbitrary",)),
        name="ada_table",
    )(cc, w, b.reshape(1, n))


_P_GQ, _P_GK, _P_GV, _P_GG, _P_SQ, _P_SK, _P_SV, _P_LR, _P_END = 0, 256, 512, 1024, 1536, 2048, 2176, 2304, 2432


def _rope(x, cos, sin):
    lane = lax.broadcasted_iota(jnp.int32, x.shape, 1)
    up = pltpu.roll(x, 112, axis=1)
    dn = pltpu.roll(x, 16, axis=1)
    sw = jnp.where((lane % 32) < 16, up, dn)
    return x * cos + sw * sin


def _proj0_kernel(x_ref, mod_ref, g_ref, w_ref, w2_ref, db_ref, cos_ref, sin_ref,
                  gq_ref, gk_ref, gv_ref, gg_ref, laf_ref, lab_ref, sq_ref, sk_ref, sv_ref,
                  *, d):
    u = (_rms(x_ref[...]) * g_ref[...] * (1.0 + _mod_rows(mod_ref, 1, d))
         + _mod_rows(mod_ref, 0, d)).astype(BF16)

    def seg(a, b):
        return jnp.dot(u, w_ref[:, a:b], preferred_element_type=F32)

    gq_ref[...] = seg(_P_GQ, _P_GK) * (GLA_DK ** -0.5)
    gk_ref[...] = seg(_P_GK, _P_GV)
    gv_ref[...] = seg(_P_GV, _P_GG)
    gg_ref[...] = seg(_P_GG, _P_SQ)
    lr = seg(_P_LR, _P_END)
    z = _hdot(lr, w2_ref[...]) + db_ref[...]
    la = (jnp.minimum(z, 0.0) - jnp.log(1.0 + jnp.exp(-jnp.abs(z)))) * (1.0 / GLA_TAU)
    hk = GLA_HEADS * GLA_DK
    laf_ref[...] = la[:, :hk]
    lab_ref[...] = la[:, hk:]
    cos = cos_ref[...]
    sin = sin_ref[...]
    sq = seg(_P_SQ, _P_SK) * (SWA_HEAD_DIM ** -0.5)
    for c in range(4):
        sq_ref[:, 128 * c:128 * (c + 1)] = _rope(sq[:, 128 * c:128 * (c + 1)], cos, sin)
    sk_ref[...] = _rope(seg(_P_SK, _P_SV), cos, sin)
    sv_ref[...] = seg(_P_SV, _P_LR)


def _proj0(xs, mod, g, w, w2, db, cos, sin, nb, nbatch):
    m, d = xs.shape
    widths = (256, 256, 512, 512, 256, 256, 512, 128, 128)
    return pl.pallas_call(
        functools.partial(_proj0_kernel, d=d),
        grid=(m // TM,),
        in_specs=[pl.BlockSpec((TM, d), lambda i: (i, 0)), _mod_spec(mod, nb, nbatch), _full((1, d)),
                  _full(w.shape), _full(w2.shape), _full(db.shape),
                  pl.BlockSpec((TM, 128), lambda i: (i % nb, 0)),
                  pl.BlockSpec((TM, 128), lambda i: (i % nb, 0))],
        out_specs=[pl.BlockSpec((TM, n), lambda i: (i, 0)) for n in widths],
        out_shape=[jax.ShapeDtypeStruct((m, n), F32) for n in widths],
        compiler_params=_cparams(("arbitrary",), VMEM_LIMIT),
        name="proj0",
    )(xs, mod, g, w, w2, db, cos, sin)


def _gla_prepare(q_ref, k_ref, la_ref, st, cum, tot_row):
    c = GLA_CHUNK
    la = la_ref[...]
    la_hi = la.astype(BF16)
    la_lo = (la - la_hi.astype(F32)).astype(BF16)
    bc = (jnp.dot(cum, la_hi, preferred_element_type=F32)
          + jnp.dot(cum, la_lo, preferred_element_type=F32))
    tot = bc[tot_row:tot_row + 1, :]
    q_in = q_ref[...] * jnp.exp(bc)
    k_in = (k_ref[...] * jnp.exp(-bc)).astype(BF16)
    k_out = (k_ref[...] * jnp.exp(tot - bc)).astype(BF16)
    ones = jnp.ones((c, GLA_DV), BF16)
    tn = (((0,), (0,)), ((), ()))
    dcol = jnp.exp(lax.dot_general(la_hi, ones, tn, preferred_element_type=F32)
                   + lax.dot_general(la_lo, ones, tn, preferred_element_type=F32))
    return q_in, k_in, k_out, dcol, st[...].astype(BF16)


def _gla_head(h, prep, v_ref, o_ref, st, mask):
    q_in, k_in, k_out, dcol, s_prev = prep
    lane = lax.broadcasted_iota(jnp.int32, q_in.shape, 1)
    qh = jnp.where((lane // GLA_DK) == h, q_in, 0.0).astype(BF16)
    att = lax.dot_general(qh, k_in, (((1,), (1,)), ((), ())), preferred_element_type=F32)
    att = jnp.where(mask, att, 0.0).astype(BF16)
    vh = v_ref[:, h * GLA_DV:(h + 1) * GLA_DV].astype(BF16)
    o = (jnp.dot(att, vh, preferred_element_type=F32)
         + jnp.dot(qh, s_prev, preferred_element_type=F32))
    o_ref[:, h * GLA_DV:(h + 1) * GLA_DV] = o
    upd = lax.dot_general(k_out, vh, (((0,), (0,)), ((), ())), preferred_element_type=F32)
    r0, r1 = h * GLA_DK, (h + 1) * GLA_DK
    st[r0:r1, :] = st[r0:r1, :] * dcol[r0:r1, :] + upd[r0:r1, :]


def _gla_kernel(qf, kf, vf, laf, qb, kb, vb, lab, of_ref, ob_ref, sf, sb):
    s = pl.program_id(1)

    @pl.when(s == 0)
    def _():
        sf[...] = jnp.zeros_like(sf)
        sb[...] = jnp.zeros_like(sb)

    c = GLA_CHUNK
    ri = lax.broadcasted_iota(jnp.int32, (c, c), 0)
    ci = lax.broadcasted_iota(jnp.int32, (c, c), 1)
    pf = _gla_prepare(qf, kf, laf, sf, (ri >= ci).astype(BF16), c - 1)
    pb = _gla_prepare(qb, kb, lab, sb, (ri <= ci).astype(BF16), 0)
    for h in range(GLA_HEADS):
        _gla_head(h, pf, vf, of_ref, sf, ri >= ci)
        _gla_head(h, pb, vb, ob_ref, sb, ci > ri)


def _gla(gq, gk, gv, laf, lab, nbatch, nctx_chunks):
    m = gq.shape[0]
    c = GLA_CHUNK
    nch = m // nbatch // c
    hk, hv = GLA_HEADS * GLA_DK, GLA_HEADS * GLA_DV

    def fi(b, s):
        return (b * nch + s, 0)

    def bi(b, s):
        cb = jnp.where(s < nctx_chunks, nctx_chunks - 1 - s, nch - 1 + nctx_chunks - s)
        return (b * nch + cb, 0)

    sk = lambda im: pl.BlockSpec((c, hk), im)
    sv = lambda im: pl.BlockSpec((c, hv), im)
    return pl.pallas_call(
        _gla_kernel,
        grid=(nbatch, nch),
        in_specs=[sk(fi), sk(fi), sv(fi), sk(fi), sk(bi), sk(bi), sv(bi), sk(bi)],
        out_specs=[sv(fi), sv(bi)],
        out_shape=[jax.ShapeDtypeStruct((m, hv), F32)] * 2,
        scratch_shapes=[pltpu.VMEM((hk, GLA_DV), F32), pltpu.VMEM((hk, GLA_DV), F32)],
        compiler_params=_cparams(("arbitrary", "arbitrary")),
        name="gla_scan",
    )(gq, gk, gv, laf, gq, gk, gv, lab)


def _swa_kernel(q_ref, kc_ref, vc_ref, kp_ref, kcur_ref, kn_ref, vp_ref, vcur_ref, vn_ref, sink_ref,
                o_ref, *, nctx_blocks, nlat_blocks):
    j = pl.program_id(1)
    w = WINDOW
    n = j - nctx_blocks
    is_lat = j >= nctx_blocks
    nkc = kc_ref.shape[0]
    kall = jnp.concatenate([kc_ref[...], kp_ref[...], kcur_ref[...], kn_ref[...]], axis=0)
    vall = jnp.concatenate([vc_ref[...], vp_ref[...], vcur_ref[...], vn_ref[...]], axis=0)
    nk = kall.shape[0]
    lane = lax.broadcasted_iota(jnp.int32, kall.shape, 1)
    lo = lane < SWA_HEAD_DIM
    kroll = pltpu.roll(kall, SWA_HEAD_DIM, axis=1)
    vroll = pltpu.roll(vall, SWA_HEAD_DIM, axis=1)
    k2 = [jnp.where(lo, kall, kroll).astype(BF16), jnp.where(lo, kroll, kall).astype(BF16)]
    v2 = [jnp.where(lo, vall, vroll).astype(BF16), jnp.where(lo, vroll, vall).astype(BF16)]
    rep = SWA_HEADS // SWA_KV_HEADS
    qi = lax.broadcasted_iota(jnp.int32, (rep * w, nk), 0) % w
    kj = lax.broadcasted_iota(jnp.int32, (rep * w, nk), 1)
    qpos = n * w + qi
    kpos = (n - 1) * w + (kj - nkc)
    loc_ok = (jnp.abs(qpos - kpos) <= WINDOW) & (kpos >= 0) & (kpos < nlat_blocks * w) & is_lat
    valid = (kj < nkc) | ((kj >= nkc) & loc_ok)
    qlane = lax.broadcasted_iota(jnp.int32, (w, 128), 1)
    qlo = qlane < SWA_HEAD_DIM
    outs = []
    for g in range(SWA_KV_HEADS):
        qs, sinks = [], []
        for r in range(rep):
            pr, half = divmod(g * rep + r, 2)
            qp = q_ref[:, 128 * pr:128 * (pr + 1)]
            qs.append(jnp.where(qlo if half == 0 else ~qlo, qp, 0.0).astype(BF16))
            sinks.append(jnp.broadcast_to(sink_ref[g * rep + r:g * rep + r + 1, 0:1], (w, 1)))
        sink = jnp.concatenate(sinks, axis=0)
        s = lax.dot_general(jnp.concatenate(qs, axis=0), k2[g], (((1,), (1,)), ((), ())),
                            preferred_element_type=F32)
        s = jnp.where(valid, s, NEG)
        mx = jnp.maximum(jnp.max(s, axis=-1, keepdims=True), sink)
        p = jnp.exp(s - mx)
        den = jnp.sum(p, axis=-1, keepdims=True) + jnp.exp(sink - mx)
        o = jnp.dot(p.astype(BF16), v2[g], preferred_element_type=F32) / den
        outs += [o[r * w:(r + 1) * w, :] for r in range(rep)]
    for pr in range(SWA_HEADS // 2):
        o_ref[:, 128 * pr:128 * (pr + 1)] = jnp.where(qlo, outs[2 * pr], outs[2 * pr + 1])


def _swa(sq, sk, sv, sink, nbatch, nctx_blocks, nlat_blocks):
    m = sq.shape[0]
    w = WINDOW
    nblk = nctx_blocks + nlat_blocks
    nkc = nctx_blocks * w

    def qmap(b, j):
        return (b * nblk + j, 0)

    def nmap(off):
        def f(b, j):
            n = jnp.clip(j - nctx_blocks + off, 0, nlat_blocks - 1)
            return (b * nblk + nctx_blocks + n, 0)
        return f

    kvw = sk.shape[1]
    kcs = pl.BlockSpec((nkc, kvw), lambda b, j: (b * (nblk * w // nkc), 0))
    kvs = lambda off: pl.BlockSpec((w, kvw), nmap(off))
    return pl.pallas_call(
        functools.partial(_swa_kernel, nctx_blocks=nctx_blocks, nlat_blocks=nlat_blocks),
        grid=(nbatch, nblk),
        in_specs=[pl.BlockSpec((w, sq.shape[1]), qmap), kcs, kcs, kvs(-1), kvs(0), kvs(1),
                  kvs(-1), kvs(0), kvs(1), _full(sink.shape)],
        out_specs=pl.BlockSpec((w, sq.shape[1]), qmap),
        out_shape=jax.ShapeDtypeStruct(sq.shape, F32),
        compiler_params=_cparams(("arbitrary", "arbitrary")),
        name="swa_attn",
    )(sq, sk, sv, sk, sk, sk, sv, sv, sv, sink)


def _residual_epilogue(x, mix, mod_ref, n2_ref, xo_ref, f_ref, ft_ref, d):
    xn = x + _mod_rows(mod_ref, 2, d) * mix
    xo_ref[...] = xn
    f = _rms(xn) * n2_ref[...] * (1.0 + _mod_rows(mod_ref, 4, d)) + _mod_rows(mod_ref, 3, d)
    f_ref[...] = f
    ft_ref[...] = f.T.astype(BF16)


def _out0_kernel(x_ref, of_ref, ob_ref, gg_ref, a_ref, mod_ref, gn_ref, wo_ref, n2_ref,
                 xo_ref, f_ref, ft_ref, *, d):
    o = of_ref[...] + ob_ref[...]
    gate = gg_ref[...]
    gate = gate * _sigmoid(gate)
    parts = []
    for h in range(GLA_HEADS):
        oh = o[:, h * GLA_DV:(h + 1) * GLA_DV]
        parts.append(oh * lax.rsqrt(jnp.mean(oh * oh, axis=-1, keepdims=True) + NORM_EPS))
    on = jnp.concatenate(parts, axis=1) * gn_ref[...] * gate
    hv = GLA_HEADS * GLA_DV
    mix = _bdot(on, wo_ref[0:hv, :]) + _bdot(a_ref[...], wo_ref[hv:, :])
    _residual_epilogue(x_ref[...], mix, mod_ref, n2_ref, xo_ref, f_ref, ft_ref, d)


def _tok_spec(n):
    return pl.BlockSpec((TM, n), lambda i: (i, 0))


def _epilogue_specs(m, d):
    out_specs = [_tok_spec(d), _tok_spec(d), pl.BlockSpec((d, TM), lambda i: (0, i))]
    out_shape = [jax.ShapeDtypeStruct((m, d), F32), jax.ShapeDtypeStruct((m, d), F32),
                 jax.ShapeDtypeStruct((d, m), BF16)]
    return out_specs, out_shape


def _out0(xs, of, ob, gg, a, mod, gn, wo, n2, nb, nbatch):
    m, d = xs.shape
    out_specs, out_shape = _epilogue_specs(m, d)
    return pl.pallas_call(
        functools.partial(_out0_kernel, d=d),
        grid=(m // TM,),
        in_specs=[_tok_spec(d), _tok_spec(of.shape[1]), _tok_spec(ob.shape[1]), _tok_spec(gg.shape[1]),
                  _tok_spec(a.shape[1]), _mod_spec(mod, nb, nbatch), _full(gn.shape), _full(wo.shape),
                  _full(n2.shape)],
        out_specs=out_specs, out_shape=out_shape,
        compiler_params=_cparams(("arbitrary",), VMEM_LIMIT),
        name="out0",
    )(xs, of, ob, gg, a, mod, gn, wo, n2)


NO_RANK = 99.0
RANK_STEP = 1e27


def _top_vals(s, k, want_rank=False):
    lw = 128
    ng = s.shape[1] // lw
    rows = lax.broadcasted_iota(jnp.int32, (k, lw), 0)
    cur = [s[:, q * lw:(q + 1) * lw] for q in range(ng)]
    vals = [jnp.zeros((k, lw), F32) for _ in range(ng)]
    for t in range(k):
        for q in range(ng):
            mx = jnp.max(cur[q], axis=0, keepdims=True)
            vals[q] = jnp.where(rows == t, mx, vals[q])
            cur[q] = jnp.where(cur[q] == mx, NEG - t * RANK_STEP, cur[q])
    vals = jnp.concatenate(vals, axis=1)
    if not want_rank:
        return vals
    cur = jnp.concatenate(cur, axis=1)
    rank = jnp.where(cur < 0.5 * NEG, jnp.floor((NEG - cur) * (1.0 / RANK_STEP) + 0.5), NO_RANK)
    return vals, rank


def _peer_scores_kernel(f_ref, wh_ref, wl_ref, kh_ref, kl_ref, s1_ref, s2_ref):
    f = f_ref[...]
    fh = f.astype(BF16)
    fl = (f - fh.astype(F32)).astype(BF16)
    wh = wh_ref[...]
    q = (jnp.dot(fh, wh, preferred_element_type=F32) + jnp.dot(fl, wh, preferred_element_type=F32)
         + jnp.dot(fh, wl_ref[...], preferred_element_type=F32))
    qh = q.astype(BF16)
    ql = (q - qh.astype(F32)).astype(BF16)
    nt = (((1,), (1,)), ((), ()))
    for h in range(PEER_HEADS):
        for z, s_ref in enumerate((s1_ref, s2_ref)):
            c0 = (2 * h + z) * PEER_NKEYS
            kh = kh_ref[z, h]
            qhs = qh[:, c0:c0 + PEER_NKEYS]
            s_ref[h] = (lax.dot_general(kh, qhs, nt, preferred_element_type=F32)
                        + lax.dot_general(kh, ql[:, c0:c0 + PEER_NKEYS], nt, preferred_element_type=F32)
                        + lax.dot_general(kl_ref[z, h], qhs, nt, preferred_element_type=F32))


def _peer_q_kernel(s1_ref, s2_ref, r2_ref, cnt_ref, e1_ref, e2_ref):
    kk = PEER_TOPK
    for h in range(PEER_HEADS):
        s1 = s1_ref[h]
        s2 = s2_ref[h]
        v1 = _top_vals(s1, kk)
        v2, rank2 = _top_vals(s2, kk, want_rank=True)
        row8 = lax.broadcasted_iota(jnp.int32, (8, s1.shape[1]), 0)
        blocks = [v1[0:1, :] + v2]
        for r in range(1, 8):
            blocks.append(jnp.where(row8 < kk // (r + 1), v1[r:r + 1, :] + v2[0:8, :], NEG))
        blocks.append(v1[8:kk, :] + v2[0:1, :])
        top = _top_vals(jnp.concatenate(blocks, axis=0), kk)
        tau = top[kk - 1:kk, :]
        z = jnp.sum(jnp.exp(top - top[0:1, :]), axis=0, keepdims=True)
        for q in range(s1.shape[1] // 128):
            ln = slice(128 * q, 128 * (q + 1))
            v2b = [jnp.broadcast_to(v2[c:c + 1, ln], (8, 128)) for c in range(kk)]
            taub = jnp.broadcast_to(tau[:, ln], (8, 128))
            for rb in range(PEER_NKEYS // 8):
                x = s1[8 * rb:8 * (rb + 1), ln]
                cnt = jnp.zeros((8, 128), F32)
                for c in range(kk):
                    cnt = cnt + jnp.where(x + v2b[c] >= taub, 1.0, 0.0)
                cnt_ref[h, 8 * rb:8 * (rb + 1), ln] = cnt
        r2_ref[h] = rank2.astype(BF16)
        e1_ref[h] = jnp.exp(s1 - v1[0:1, :])
        e2_ref[h] = (jnp.exp(s2 - v2[0:1, :]) / z).astype(BF16)


def _peer_q(f, wh, wl, k1, k2):
    m, d = f.shape
    hh, nk = PEER_HEADS, PEER_NKEYS
    big = pl.BlockSpec((hh, nk, TM), lambda i: (0, 0, i))
    sh32 = jax.ShapeDtypeStruct((hh, nk, m), F32)
    sh16 = jax.ShapeDtypeStruct((hh, nk, m), BF16)
    kf = jnp.stack([k1, k2])
    kh = kf.astype(BF16)
    kl = (kf - kh.astype(F32)).astype(BF16)
    s1, s2 = pl.pallas_call(
        _peer_scores_kernel,
        grid=(m // TM,),
        in_specs=[_tok_spec(d), _full(wh.shape), _full(wl.shape), _full(kh.shape), _full(kl.shape)],
        out_specs=[big, big],
        out_shape=[sh32, sh32],
        compiler_params=_cparams(("arbitrary",), VMEM_LIMIT),
        name="peer_scores",
    )(f, wh, wl, kh, kl)
    return pl.pallas_call(
        _peer_q_kernel,
        grid=(m // TM,),
        in_specs=[big, big],
        out_specs=[big, big, big, big],
        out_shape=[sh16, sh32, sh32, sh16],
        compiler_params=_cparams(("arbitrary",), VMEM_LIMIT),
        name="peer_query",
    )(s1, s2)


PD_TM = 512
PD_TE = 2048
PD_SUB = 1024


def _gelu(x):
    return 0.5 * x * (1.0 + lax.erf(x * (2.0 ** -0.5)))


def _row_bf16(tile, r, rows):
    one = jnp.broadcast_to(tile[r:r + 1, :], (8, tile.shape[1]))
    one = jnp.concatenate([one, one], axis=0).astype(BF16)
    return jnp.concatenate([one] * (rows // 16), axis=0)


def _run_if(cond, fn):
    def body(_, carry):
        fn()
        return carry
    lax.fori_loop(0, cond.astype(jnp.int32), body, 0)


def _peer_dense_kernel(ft_ref, u_ref, vt_ref, r2_ref, cnt_ref, e1_ref, e2_ref, o_ref, acc_ref):
    k = pl.program_id(1)
    nk = PEER_NKEYS
    nslab = PD_TE // nk
    per_sub = PD_SUB // nk
    zero = jnp.zeros((nk, PD_TM), BF16)
    nsub = PD_TE // PD_SUB

    def first():
        acc_ref[...] = jnp.zeros_like(acc_ref)

    _run_if(k == 0, first)

    def hidden(sb):
        return jnp.dot(u_ref[sb * PD_SUB:(sb + 1) * PD_SUB, :], ft_ref[...], preferred_element_type=F32)

    ht_next = hidden(0)
    for sb in range(nsub):
        e0 = sb * PD_SUB
        ht = ht_next
        if sb + 1 < nsub:
            ht_next = hidden(sb + 1)
        gates = []
        for al in range(sb * per_sub, (sb + 1) * per_sub):
            wg = zero
            a0 = pl.multiple_of(k * nslab + 8 * (al // 8), 8)
            for h in range(PEER_HEADS):
                cnt = _row_bf16(cnt_ref[h, pl.ds(a0, 8), :], al % 8, nk)
                e1 = _row_bf16(e1_ref[h, pl.ds(a0, 8), :], al % 8, nk)
                wg = wg + jnp.where(r2_ref[h] < cnt, e2_ref[h], zero) * e1
            gates.append(wg)
        ct = _gelu(ht).astype(BF16) * jnp.concatenate(gates, axis=0)
        acc_ref[...] += jnp.dot(vt_ref[:, e0:e0 + PD_SUB], ct, preferred_element_type=F32)

    def last():
        o_ref[...] = acc_ref[...].T

    _run_if(k == pl.num_programs(1) - 1, last)


def _peer_dense(ft, u, vt, layer, r2, cnt, e1, e2):
    d, m = ft.shape
    ne = u.shape[1]
    hh, nk = PEER_HEADS, PEER_NKEYS
    big = pl.BlockSpec((hh, nk, PD_TM), lambda i, k: (0, 0, i))
    return pl.pallas_call(
        _peer_dense_kernel,
        grid=(m // PD_TM, ne // PD_TE),
        in_specs=[pl.BlockSpec((d, PD_TM), lambda i, k: (0, i)),
                  pl.BlockSpec((None, PD_TE, d), lambda i, k: (layer, k, 0)),
                  pl.BlockSpec((None, None, d, PD_TE), lambda i, k: (layer, k, 0, 0)),
                  big, big, big, big],
        out_specs=pl.BlockSpec((PD_TM, d), lambda i, k: (i, 0)),
        out_shape=jax.ShapeDtypeStruct((m, d), F32),
        scratch_shapes=[pltpu.VMEM((d, PD_TM), F32)],
        compiler_params=_cparams(("arbitrary", "arbitrary"), VMEM_LIMIT),
        name="peer_dense",
    )(ft, u, vt, r2, cnt, e1, e2)


def _res_kernel(x_ref, p_ref, mod_ref, modn_ref, g_ref, xo_ref, u_ref, *, d):
    xn = x_ref[...] + _mod_rows(mod_ref, 5, d) * p_ref[...]
    xo_ref[...] = xn
    u_ref[...] = _rms(xn) * g_ref[...] * (1.0 + _mod_rows(modn_ref, 1, d)) + _mod_rows(modn_ref, 0, d)


def _res(xs, p, mod, modn, g, nb, nbatch):
    m, d = xs.shape
    return pl.pallas_call(
        functools.partial(_res_kernel, d=d),
        grid=(m // TM,),
        in_specs=[_tok_spec(d), _tok_spec(d), _mod_spec(mod, nb, nbatch), _mod_spec(modn, nb, nbatch),
                  _full(g.shape)],
        out_specs=[_tok_spec(d), _tok_spec(d)],
        out_shape=[jax.ShapeDtypeStruct((m, d), F32)] * 2,
        compiler_params=_cparams(("arbitrary",)),
        name="peer_residual",
    )(xs, p, mod, modn, g)


def _final_kernel(x_ref, p_ref, mod_ref, g_ref, y_ref, *, d):
    y_ref[...] = _rms(x_ref[...] + _mod_rows(mod_ref, 5, d) * p_ref[...]) * g_ref[...]


def _final(xs, p, mod, g, nb):
    m, d = xs.shape
    return pl.pallas_call(
        functools.partial(_final_kernel, d=d),
        grid=(m // TM,),
        in_specs=[_tok_spec(d), _tok_spec(d), _mod_spec_lat(mod, nb - 1), _full(g.shape)],
        out_specs=_tok_spec(d),
        out_shape=jax.ShapeDtypeStruct((m, d), F32),
        compiler_params=_cparams(("arbitrary",)),
        name="final_norm",
    )(xs, p, mod, g)


def _split_dot(x, w):
    hi = x.astype(BF16)
    lo = (x - hi.astype(F32)).astype(BF16)
    return jnp.dot(hi, w, preferred_element_type=F32) + jnp.dot(lo, w, preferred_element_type=F32)


def _seg_sum(x, g_ref, gt_ref):
    return _split_dot(_split_dot(x, g_ref[...]), gt_ref[...])


def _rwkv_prep_kernel(u_ref, up_ref, un_ref, mu_ref, wr_ref, wk_ref, wv_ref, w0_ref, w1_ref, w2_ref,
                      a0_ref, a1_ref, a2_ref, g1_ref, g2_ref, kk_ref_, ka_ref, rk_ref, sg_ref, sgt_ref,
                      wf_o, wb_o, kf_o, kb_o, bf_o, bb_o, r_o, v_o, kk_o, g_o, bonus_o, *, nb, d):
    i = pl.program_id(0)
    p = i % nb
    u = u_ref[...]
    has_prev = (p >= 2).astype(F32)
    has_next = ((p >= 1) & (p <= nb - 2)).astype(F32)
    prev_row = up_ref[7:8, :] * has_prev
    next_row = un_ref[0:1, :] * has_next
    rows = lax.broadcasted_iota(jnp.int32, u.shape, 0)
    up = jnp.where(rows == 0, prev_row, pltpu.roll(u, 1, axis=0))
    un = jnp.where(rows == TM - 1, next_row, pltpu.roll(u, TM - 1, axis=0))
    xx = 0.5 * (up + un) - u

    def mix(zi):
        return (u + xx * mu_ref[zi:zi + 1, :]).astype(BF16)

    r = jnp.dot(mix(0), wr_ref[...], preferred_element_type=F32)
    k = jnp.dot(mix(2), wk_ref[...], preferred_element_type=F32)
    v = jnp.dot(mix(3), wv_ref[...], preferred_element_type=F32)
    lw = w0_ref[...] + _bdot(jnp.tanh(jnp.dot(mix(1), w1_ref[...], preferred_element_type=F32)), w2_ref[...])
    decay = jnp.exp(-_sigmoid(lw) * float(np.exp(-0.5)))
    a = _sigmoid(a0_ref[...] + _bdot(jnp.dot(mix(4), a1_ref[...], preferred_element_type=F32), a2_ref[...]))
    g = _bdot(_sigmoid(jnp.dot(mix(5), g1_ref[...], preferred_element_type=F32)), g2_ref[...])
    kk = k * kk_ref_[...]
    kk = kk * lax.rsqrt(jnp.maximum(_seg_sum(kk * kk, sg_ref, sgt_ref), 1e-24))
    ka = ka_ref[...]
    kf = k * (1.0 + (a[:, :d] - 1.0) * ka)
    kb = k * (1.0 + (a[:, d:] - 1.0) * ka)
    bonus = _seg_sum(r * (kf + kb) * rk_ref[...], sg_ref, sgt_ref) * v
    wf_o[...] = decay[:, :d]
    wb_o[...] = decay[:, d:]
    kf_o[...] = kf
    kb_o[...] = kb
    bf_o[...] = kk * a[:, :d]
    bb_o[...] = kk * a[:, d:]
    r_o[...] = r
    v_o[...] = v
    kk_o[...] = kk
    g_o[...] = g
    bonus_o[...] = bonus


def _rwkv_prep(u, consts, nb):
    m, d = u.shape
    r8 = TM // 8
    nb8 = m // 8
    in_specs = [_tok_spec(d),
                pl.BlockSpec((8, d), lambda i: (jnp.maximum(i * r8 - 1, 0), 0)),
                pl.BlockSpec((8, d), lambda i: (jnp.minimum((i + 1) * r8, nb8 - 1), 0))]
    in_specs += [_full(c.shape) for c in consts]
    return pl.pallas_call(
        functools.partial(_rwkv_prep_kernel, nb=nb, d=d),
        grid=(m // TM,),
        in_specs=in_specs,
        out_specs=[_tok_spec(d)] * 11,
        out_shape=[jax.ShapeDtypeStruct((m, d), F32)] * 11,
        compiler_params=_cparams(("arbitrary",), VMEM_LIMIT),
        name="rwkv_prep",
    )(u, u, u, *consts)


RW_TC = 256


def _rwkv_scan_kernel(wf, kf, bf, rf, vf, kkf, wb, kb, bb, rb, vb, kkb, yf_ref, yb_ref,
                      st, vk_a, vk_b, ycol_scr, sa_scr, *, npair):
    s = pl.program_id(1)
    n = RW_HEAD_DIM
    grp = 8

    @pl.when(s == 0)
    def _():
        st[...] = jnp.zeros_like(st)
        ycol_scr[...] = jnp.zeros_like(ycol_scr)

    w2 = 2 * n
    ri = lax.broadcasted_iota(jnp.int32, (n, w2), 0)
    li = lax.broadcasted_iota(jnp.int32, (n, w2), 1)
    eye_a = (li == ri).astype(BF16)
    bi = lax.broadcasted_iota(jnp.int32, (w2, w2), 0)
    bj = lax.broadcasted_iota(jnp.int32, (w2, w2), 1)
    ones_blk = ((bi // n) == (bj // n)).astype(BF16)
    b4i = lax.broadcasted_iota(jnp.int32, (2 * w2, 2 * w2), 0)
    b4j = lax.broadcasted_iota(jnp.int32, (2 * w2, 2 * w2), 1)
    ones_blk2 = ((b4i // n) == (b4j // n)).astype(BF16)
    lane8 = lax.broadcasted_iota(jnp.int32, (grp, w2), 1)
    rows16 = lax.broadcasted_iota(jnp.int32, (2 * grp, w2), 0)
    lane16 = lax.broadcasted_iota(jnp.int32, (2 * grp, w2), 1)
    untr = ((lane16 == rows16) | (lane16 == rows16 + n)).astype(BF16)
    dirs = ((wf, kf, bf, rf, vf, kkf, yf_ref), (wb, kb, bb, rb, vb, kkb, yb_ref))
    nch = 2 * npair
    ngrp = RW_TC // grp
    nt = (((1,), (1,)), ((), ()))

    def halves(tile):
        return jnp.concatenate([jnp.where(lane8 < n, tile, 0.0), jnp.where(lane8 >= n, tile, 0.0)], axis=0)

    assert nch == 2 * grp

    def group_base(g, dd):
        base = (g if dd == 0 else ngrp - 1 - g) * grp
        return base if isinstance(base, int) else pl.multiple_of(base, grp)

    def lanes(c):
        p = c % npair
        return slice(w2 * p, w2 * (p + 1))

    def precompute(g, chains, vk_dst):
        gq = jnp.minimum(g, ngrp - 1)
        v2, ks = [], []
        for c in chains:
            dd = c // npair
            base = group_base(gq, dd)
            v_t = dirs[dd][4][pl.ds(base, grp), lanes(c)]
            ks.append(dirs[dd][1][pl.ds(base, grp), lanes(c)])
            v2 += [jnp.where(lane8 < n, v_t, 0.0), jnp.where(lane8 < n, pltpu.roll(v_t, n, axis=1), 0.0)]
        vt = lax.dot_general(eye_a, jnp.concatenate(v2, axis=0).astype(BF16), nt,
                             preferred_element_type=F32).astype(BF16)
        for idx, c in enumerate(chains):
            k2 = halves(ks[idx])
            kblk = jnp.concatenate(
                [jnp.where((rows16 == j) | (rows16 == grp + j), k2, 0.0) for j in range(grp)], axis=1)
            vk_dst[c] = jnp.dot(vt[:, 2 * grp * idx:2 * grp * (idx + 1)], kblk.astype(BF16),
                                preferred_element_type=F32)

    def untranspose(g, chains, ycol_prev):
        for c in chains:
            dd = c // npair
            yc = ycol_prev[c]
            ycs = jnp.concatenate([jnp.where(li < n, yc, 0.0), jnp.where(li >= n, yc, 0.0)], axis=0)
            yt = lax.dot_general(untr, ycs.astype(BF16), nt, preferred_element_type=F32)
            dirs[dd][6][pl.ds(group_base(g, dd), grp), lanes(c)] = yt[0:grp, :]

    def run_group(g, vk_cur, vk_next, states, sa, ycol_prev):
        tiles = []
        for c in range(nch):
            dd = c // npair
            base = group_base(g, dd)
            tiles.append([dirs[dd][q][pl.ds(base, grp), lanes(c)] for q in (0, 2, 3, 5)])
        gprev = jnp.maximum(g - 1, 0)
        gnext = jnp.minimum(g + 1, ngrp - 1)
        kap_next = []
        for c in range(nch):
            dd = c // npair
            kap_next.append(dirs[dd][5][pl.ds(group_base(gnext, dd), grp), lanes(c)])
        states, sa = list(states), list(sa)
        ycol = [jnp.zeros((n, w2), F32) for _ in range(nch)]
        for j in range(grp):
            lhs = []
            for c in range(nch):
                w_t, b_t, r_t, kk_t = tiles[c]
                fwd = c < npair
                jj = j if fwd else grp - 1 - j
                row = lambda tl, q: jnp.broadcast_to(tl[q:q + 1, :], (n, w2))
                states[c] = (states[c] * row(w_t, jj) - sa[c] * row(b_t, jj)
                             + vk_cur[c, :, jj * w2:(jj + 1) * w2])
                if j + 1 < grp:
                    kap = row(kk_t, jj + 1 if fwd else jj - 1)
                else:
                    kap = row(kap_next[c], 0 if fwd else grp - 1)
                lhs.append(jnp.concatenate([(states[c] * kap).astype(BF16),
                                            (states[c] * row(r_t, jj)).astype(BF16)], axis=1))
            red = jnp.dot(jnp.concatenate(lhs, axis=0), ones_blk2, preferred_element_type=F32)
            for c in range(nch):
                jj = j if c < npair else grp - 1 - j
                sa[c] = red[c * n:(c + 1) * n, :w2]
                ycol[c] = jnp.where((li % n) == jj, red[c * n:(c + 1) * n, w2:], ycol[c])
            pair = (2 * j, 2 * j + 1)
            precompute(g + 1, pair, vk_next)
            untranspose(gprev, pair, ycol_prev)
        return states, sa, ycol

    precompute(0, tuple(range(nch)), vk_a)
    first = []
    for c in range(nch):
        dd = c // npair
        kk_t = dirs[dd][5][pl.ds(group_base(0, dd), grp), lanes(c)]
        q0 = 0 if dd == 0 else grp - 1
        first.append((st[c] * jnp.broadcast_to(kk_t[q0:q0 + 1, :], (n, w2))).astype(BF16))
    sa0 = jnp.dot(jnp.concatenate(first, axis=0), ones_blk, preferred_element_type=F32)
    for c in range(nch):
        sa_scr[c] = sa0[c * n:(c + 1) * n, :]

    def two_groups(i, carry):
        states = [st[c] for c in range(nch)]
        sa = [sa_scr[c] for c in range(nch)]
        ycol = [ycol_scr[c] for c in range(nch)]
        states, sa, ycol = run_group(2 * i, vk_a, vk_b, states, sa, ycol)
        states, sa, ycol = run_group(2 * i + 1, vk_b, vk_a, states, sa, ycol)
        for c in range(nch):
            st[c] = states[c]
            sa_scr[c] = sa[c]
            ycol_scr[c] = ycol[c]
        return carry

    lax.fori_loop(0, ngrp // 2, two_groups, 0)
    untranspose(ngrp - 1, tuple(range(nch)), [ycol_scr[c] for c in range(nch)])


def _rwkv_scan(wf, kf, bf, wb, kb, bb, r, v, kk, nbatch):
    m, d = r.shape
    nch = m // nbatch // RW_TC
    npair = d // (2 * RW_HEAD_DIM)

    def fi(b, s):
        return (b * nch + s, 0)

    def bi(b, s):
        return (b * nch + jnp.where(s == 0, 0, nch - s), 0)

    sf = pl.BlockSpec((RW_TC, d), fi)
    sb = pl.BlockSpec((RW_TC, d), bi)
    return pl.pallas_call(
        functools.partial(_rwkv_scan_kernel, npair=npair),
        grid=(nbatch, nch),
        in_specs=[sf] * 6 + [sb] * 6,
        out_specs=[sf, sb],
        out_shape=[jax.ShapeDtypeStruct((m, d), F32)] * 2,
        scratch_shapes=[pltpu.VMEM((2 * npair, RW_HEAD_DIM, 2 * RW_HEAD_DIM), F32),
                        pltpu.VMEM((2 * npair, RW_HEAD_DIM, 8 * 2 * RW_HEAD_DIM), F32),
                        pltpu.VMEM((2 * npair, RW_HEAD_DIM, 8 * 2 * RW_HEAD_DIM), F32),
                        pltpu.VMEM((2 * npair, RW_HEAD_DIM, 2 * RW_HEAD_DIM), F32),
                        pltpu.VMEM((2 * npair, RW_HEAD_DIM, 2 * RW_HEAD_DIM), F32)],
        compiler_params=_cparams(("arbitrary", "arbitrary"), VMEM_LIMIT),
        name="rwkv_scan",
    )(wf, kf, bf, r, v, kk, wb, kb, bb, r, v, kk)


def _rwkv_out_kernel(x_ref, yf_ref, yb_ref, bonus_ref, g_ref, mod_ref, lg_ref, lb_ref, wo_ref, n2_ref,
                     sg_ref, sgt_ref, xo_ref, f_ref, ft_ref, *, d):
    y = yf_ref[...] + yb_ref[...]
    inv = 1.0 / RW_HEAD_DIM
    mean = _seg_sum(y, sg_ref, sgt_ref) * inv
    yc = y - mean
    var = _seg_sum(yc * yc, sg_ref, sgt_ref) * inv
    yn = yc * lax.rsqrt(var + RW_GN_EPS) * lg_ref[...] + lb_ref[...]
    out = (yn + bonus_ref[...]) * g_ref[...]
    mix = _bdot(out, wo_ref[...])
    _residual_epilogue(x_ref[...], mix, mod_ref, n2_ref, xo_ref, f_ref, ft_ref, d)


def _rwkv_out(xs, yf, yb, bonus, g, mod, lg, lb, wo, n2, sg, sgt, nb, nbatch):
    d = xs.shape[1]
    m = nbatch * (nb - 1) * TM
    out_specs, out_shape = _epilogue_specs(m, d)
    return pl.pallas_call(
        functools.partial(_rwkv_out_kernel, d=d),
        grid=(m // TM,),
        in_specs=[_lat_tok_spec(d, nb)] * 5 + [_mod_spec_lat(mod, nb - 1)]
        + [_full(a.shape) for a in (lg, lb, wo, n2, sg, sgt)],
        out_specs=out_specs, out_shape=out_shape,
        compiler_params=_cparams(("arbitrary",), VMEM_LIMIT),
        name="rwkv_out",
    )(xs, yf, yb, bonus, g, mod, lg, lb, wo, n2, sg, sgt)


def _rope_tables(t_lat, n_ctx):
    rows = t_lat // GRID_W
    row = jnp.broadcast_to(jnp.arange(rows, dtype=F32)[:, None], (rows, GRID_W)).reshape(-1)
    col = jnp.broadcast_to(jnp.arange(GRID_W, dtype=F32)[None, :], (rows, GRID_W)).reshape(-1)
    n_freq = SWA_HEAD_DIM // 4
    inv_freq = ROPE_BASE ** (-jnp.arange(n_freq, dtype=F32) / n_freq)
    ar = row[:, None] * inv_freq
    ac = col[:, None] * inv_freq
    cos = jnp.concatenate([jnp.cos(ar), jnp.cos(ar), jnp.cos(ac), jnp.cos(ac)], axis=1)
    sin = jnp.concatenate([-jnp.sin(ar), jnp.sin(ar), -jnp.sin(ac), jnp.sin(ac)], axis=1)
    cos = jnp.concatenate([jnp.ones((n_ctx, SWA_HEAD_DIM), F32), cos], axis=0)
    sin = jnp.concatenate([jnp.zeros((n_ctx, SWA_HEAD_DIM), F32), sin], axis=0)
    return jnp.tile(cos, (1, 2)), jnp.tile(sin, (1, 2))


def _peer(f, ft, w_q, k1, k2, u_all, vt_all, layer):
    wh = w_q.astype(BF16)
    wl = (w_q - wh.astype(F32)).astype(BF16)
    r2, cnt, e1, e2 = _peer_q(f, wh, wl, k1, k2)
    return _peer_dense(ft, u_all, vt_all, layer, r2, cnt, e1, e2)


def kernel(x, c, ctx, c_ctx, ada_w, ada_b, norm1_g, norm2_g, ab_w_in, gla_dec_w2, gla_dec_b, gla_norm_g,
           swa_sink, ab_w_out, rw_mu, rw_w_rkv, rw_w_o, rw_w0, rw_w1, rw_w2, rw_a0, rw_a1, rw_a2, rw_g1,
           rw_g2, rw_k_k, rw_k_a, rw_r_k, rw_ln_g, rw_ln_b, peer_w_q, peer_k1, peer_k2, peer_u, peer_v,
           final_g):
    nbatch, t_lat, d = x.shape
    n_ctx = ctx.shape[1]
    assert n_ctx == TM and t_lat % TM == 0 and d == 1024
    s_tok = n_ctx + t_lat
    nb = s_tok // TM
    m = nbatch * s_tok
    xs = jnp.concatenate([ctx, x], axis=1).reshape(m, d)

    cc = jnp.zeros((16, d), F32).at[:nbatch].set(c).at[nbatch].set(c_ctx)
    mod0 = _ada_table(cc, ada_w[0], ada_b[0]).reshape(16, 1, 6 * d)
    mod1 = _ada_table(cc, ada_w[1], ada_b[1]).reshape(16, 1, 6 * d)
    row2 = lambda a: a.reshape(1, -1)

    w_in = ab_w_in[0]
    cuts = np.cumsum((256, 256, 512, 512, 32, 512, 128, 128))[:-1].tolist()
    wgq, wgk, wgv, wgg, wlr, wsq, wsk, wsv = jnp.split(w_in, cuts, axis=1)
    wp = jnp.concatenate([wgq, wgk, wgv, wgg, wsq, wsk, wsv, wlr,
                          jnp.zeros((d, _P_END - _P_LR - 32), F32)], axis=1).astype(BF16)
    hk = GLA_HEADS * GLA_DK
    w2p = jnp.zeros((128, 2 * hk), F32)
    w2p = w2p.at[0:GLA_LOWRANK, 0:hk].set(gla_dec_w2[0, 0]).at[GLA_LOWRANK:2 * GLA_LOWRANK, hk:].set(gla_dec_w2[0, 1])
    db = gla_dec_b[0].reshape(1, 2 * hk)
    cos, sin = _rope_tables(t_lat, n_ctx)
    gq, gk, gv, gg, laf, lab, sq, sk, sv = _proj0(xs, mod0, row2(norm1_g[0]), wp, w2p, db, cos, sin, nb, nbatch)
    of, ob = _gla(gq, gk, gv, laf, lab, nbatch, n_ctx // GLA_CHUNK)
    sink = jnp.broadcast_to(swa_sink[0][:, None], (SWA_HEADS, 128))
    att = _swa(sq, sk, sv, sink, nbatch, n_ctx // WINDOW, t_lat // WINDOW)
    xs, f, ft = _out0(xs, of, ob, gg, att, mod0, row2(gla_norm_g[0]), ab_w_out[0].astype(BF16),
                      row2(norm2_g[0]), nb, nbatch)
    u_all = peer_u.astype(BF16)
    vt_all = jnp.swapaxes(peer_v.reshape(peer_v.shape[0], -1, PD_TE, d), 2, 3).astype(BF16)
    p = _peer(f, ft, peer_w_q[0], peer_k1[0], peer_k2[0], u_all, vt_all, 0)
    xs, u = _res(xs, p, mod0, mod1, row2(norm1_g[1]), nb, nbatch)

    lora = rw_w1.shape[-1]
    w1c = jnp.concatenate([rw_w1[0, 0], rw_w1[0, 1]], axis=1).astype(BF16)
    w2c = jnp.zeros((2 * lora, 2 * d), F32).at[:lora, :d].set(rw_w2[0, 0]).at[lora:, d:].set(rw_w2[0, 1]).astype(BF16)
    la = rw_a1.shape[-1]
    a1c = jnp.concatenate([rw_a1[0, 0], rw_a1[0, 1]], axis=1).astype(BF16)
    a2c = jnp.zeros((2 * la, 2 * d), F32).at[:la, :d].set(rw_a2[0, 0]).at[la:, d:].set(rw_a2[0, 1]).astype(BF16)
    lg = rw_g1.shape[-1]
    g1p = jnp.zeros((d, 256), F32).at[:, :lg].set(rw_g1[0]).astype(BF16)
    g2p = jnp.zeros((256, d), F32).at[:lg].set(rw_g2[0]).astype(BF16)
    sg = (jnp.arange(d)[:, None] // RW_HEAD_DIM == jnp.arange(128)[None, :]).astype(BF16)
    sgt = sg.T
    consts = [rw_mu[0], rw_w_rkv[0, 0].astype(BF16), rw_w_rkv[0, 1].astype(BF16), rw_w_rkv[0, 2].astype(BF16),
              rw_w0[0].reshape(1, 2 * d), w1c, w2c, rw_a0[0].reshape(1, 2 * d), a1c, a2c, g1p, g2p,
              row2(rw_k_k[0]), row2(rw_k_a[0]), rw_r_k[0].reshape(1, d), sg, sgt]
    mu8 = jnp.zeros((8, d), F32).at[:6].set(rw_mu[0])
    consts[0] = mu8
    wf, wb, kf, kb, bf, bb, r, v, kk, g, bonus = _rwkv_prep(u, consts, nb)
    yf, yb = _rwkv_scan(wf, kf, bf, wb, kb, bb, r, v, kk, nbatch)
    xs, f, ft = _rwkv_out(xs, yf, yb, bonus, g, mod1, row2(rw_ln_g[0]), row2(rw_ln_b[0]),
                          rw_w_o[0].astype(BF16), row2(norm2_g[1]), sg, sgt, nb, nbatch)
    p = _peer(f, ft, peer_w_q[1], peer_k1[1], peer_k2[1], u_all, vt_all, 1)
    y = _final(xs, p, mod1, row2(final_g), nb)
    return y.reshape(nbatch, t_lat, d)
```

```python
import functools

import numpy as np
import jax
import jax.numpy as jnp
from jax import lax
from jax.experimental import pallas as pl
from jax.experimental.pallas import tpu as pltpu

F32 = jnp.float32
BF16 = jnp.bfloat16
HI = lax.Precision.HIGHEST

NORM_EPS = 1e-6
GLA_HEADS, GLA_DK, GLA_DV, GLA_LOWRANK, GLA_TAU, GLA_CHUNK = 4, 64, 128, 16, 16.0, 64
SWA_HEADS, SWA_KV_HEADS, SWA_HEAD_DIM, WINDOW = 8, 2, 64, 128
ROPE_BASE = 10000.0
GRID_W = 64
RW_HEAD_DIM = 64
RW_GN_EPS = 64e-5
PEER_HEADS, PEER_NKEYS, PEER_TOPK = 8, 128, 16
NEG = -1e30

TM = 256
VMEM_LIMIT = 56 * 1024 * 1024


def _cparams(sem, vmem=None):
    return pltpu.CompilerParams(dimension_semantics=sem, vmem_limit_bytes=vmem)


def _bdot(a, b):
    return jnp.dot(a.astype(BF16), b.astype(BF16), preferred_element_type=F32)


def _hdot(a, b):
    return jnp.dot(a, b, precision=HI, preferred_element_type=F32)


def _sigmoid(x):
    return 1.0 / (1.0 + jnp.exp(-x))


def _rms(x):
    return x * lax.rsqrt(jnp.mean(x * x, axis=-1, keepdims=True) + NORM_EPS)


def _mod_rows(mod_ref, k, d):
    return mod_ref[0, :, k * d:(k + 1) * d]


def _mod_spec(mod, nb, nbatch):
    return pl.BlockSpec((1, 1, mod.shape[2]), lambda i: (jnp.where(i % nb == 0, nbatch, i // nb), 0, 0))


def _mod_spec_lat(mod, nbl):
    return pl.BlockSpec((1, 1, mod.shape[2]), lambda i: (i // nbl, 0, 0))


def _lat_tok_spec(n, nb):
    return pl.BlockSpec((TM, n), lambda i: ((i // (nb - 1)) * nb + 1 + i % (nb - 1), 0))


def _full(shape):
    n = len(shape)
    return pl.BlockSpec(shape, lambda *_: (0,) * n)


def _ada_kernel(c_ref, w_ref, b_ref, o_ref):
    c = c_ref[...]
    s = c * _sigmoid(c)
    o_ref[...] = _hdot(s, w_ref[...]) + b_ref[...]


def _ada_table(cc, w, b):
    rows, d = cc.shape
    n = w.shape[1]
    tn = 512
    return pl.pallas_call(
        _ada_kernel,
        grid=(n // tn,),
        in_specs=[_full((rows, d)), pl.BlockSpec((d, tn), lambda j: (0, j)),
                  pl.BlockSpec((1, tn), lambda j: (0, j))],
        out_specs=pl.BlockSpec((rows, tn), lambda j: (0, j)),
        out_shape=jax.ShapeDtypeStruct((rows, n), F32),
        compiler_params=_cparams(("arbitrary",)),
        name="ada_table",
    )(cc, w, b.reshape(1, n))


_P_GQ, _P_GK, _P_GV, _P_GG, _P_SQ, _P_SK, _P_SV, _P_LR, _P_END = 0, 256, 512, 1024, 1536, 2048, 2176, 2304, 2432


def _rope(x, cos, sin):
    lane = lax.broadcasted_iota(jnp.int32, x.shape, 1)
    up = pltpu.roll(x, 112, axis=1)
    dn = pltpu.roll(x, 16, axis=1)
    sw = jnp.where((lane % 32) < 16, up, dn)
    return x * cos + sw * sin


def _proj0_kernel(x_ref, mod_ref, g_ref, w_ref, w2_ref, db_ref, cos_ref, sin_ref,
                  gq_ref, gk_ref, gv_ref, gg_ref, laf_ref, lab_ref, sq_ref, sk_ref, sv_ref,
                  *, d):
    u = (_rms(x_ref[...]) * g_ref[...] * (1.0 + _mod_rows(mod_ref, 1, d))
         + _mod_rows(mod_ref, 0, d)).astype(BF16)

    def seg(a, b):
        return jnp.dot(u, w_ref[:, a:b], preferred_element_type=F32)

    gq_ref[...] = seg(_P_GQ, _P_GK) * (GLA_DK ** -0.5)
    gk_ref[...] = seg(_P_GK, _P_GV)
    gv_ref[...] = seg(_P_GV, _P_GG)
    gg_ref[...] = seg(_P_GG, _P_SQ)
    lr = seg(_P_LR, _P_END)
    z = _hdot(lr, w2_ref[...]) + db_ref[...]
    la = (jnp.minimum(z, 0.0) - jnp.log(1.0 + jnp.exp(-jnp.abs(z)))) * (1.0 / GLA_TAU)
    hk = GLA_HEADS * GLA_DK
    laf_ref[...] = la[:, :hk]
    lab_ref[...] = la[:, hk:]
    cos = cos_ref[...]
    sin = sin_ref[...]
    sq = seg(_P_SQ, _P_SK) * (SWA_HEAD_DIM ** -0.5)
    for c in range(4):
        sq_ref[:, 128 * c:128 * (c + 1)] = _rope(sq[:, 128 * c:128 * (c + 1)], cos, sin)
    sk_ref[...] = _rope(seg(_P_SK, _P_SV), cos, sin)
    sv_ref[...] = seg(_P_SV, _P_LR)


def _proj0(xs, mod, g, w, w2, db, cos, sin, nb, nbatch):
    m, d = xs.shape
    widths = (256, 256, 512, 512, 256, 256, 512, 128, 128)
    return pl.pallas_call(
        functools.partial(_proj0_kernel, d=d),
        grid=(m // TM,),
        in_specs=[pl.BlockSpec((TM, d), lambda i: (i, 0)), _mod_spec(mod, nb, nbatch), _full((1, d)),
                  _full(w.shape), _full(w2.shape), _full(db.shape),
                  pl.BlockSpec((TM, 128), lambda i: (i % nb, 0)),
                  pl.BlockSpec((TM, 128), lambda i: (i % nb, 0))],
        out_specs=[pl.BlockSpec((TM, n), lambda i: (i, 0)) for n in widths],
        out_shape=[jax.ShapeDtypeStruct((m, n), F32) for n in widths],
        compiler_params=_cparams(("arbitrary",), VMEM_LIMIT),
        name="proj0",
    )(xs, mod, g, w, w2, db, cos, sin)


def _gla_prepare(q_ref, k_ref, la_ref, st, cum, tot_row):
    c = GLA_CHUNK
    la = la_ref[...]
    la_hi = la.astype(BF16)
    la_lo = (la - la_hi.astype(F32)).astype(BF16)
    bc = (jnp.dot(cum, la_hi, preferred_element_type=F32)
          + jnp.dot(cum, la_lo, preferred_element_type=F32))
    tot = bc[tot_row:tot_row + 1, :]
    q_in = q_ref[...] * jnp.exp(bc)
    k_in = (k_ref[...] * jnp.exp(-bc)).astype(BF16)
    k_out = (k_ref[...] * jnp.exp(tot - bc)).astype(BF16)
    ones = jnp.ones((c, GLA_DV), BF16)
    tn = (((0,), (0,)), ((), ()))
    dcol = jnp.exp(lax.dot_general(la_hi, ones, tn, preferred_element_type=F32)
                   + lax.dot_general(la_lo, ones, tn, preferred_element_type=F32))
    return q_in, k_in, k_out, dcol, st[...].astype(BF16)


def _gla_head(h, prep, v_ref, o_ref, st, mask):
    q_in, k_in, k_out, dcol, s_prev = prep
    lane = lax.broadcasted_iota(jnp.int32, q_in.shape, 1)
    qh = jnp.where((lane // GLA_DK) == h, q_in, 0.0).astype(BF16)
    att = lax.dot_general(qh, k_in, (((1,), (1,)), ((), ())), preferred_element_type=F32)
    att = jnp.where(mask, att, 0.0).astype(BF16)
    vh = v_ref[:, h * GLA_DV:(h + 1) * GLA_DV].astype(BF16)
    o = (jnp.dot(att, vh, preferred_element_type=F32)
         + jnp.dot(qh, s_prev, preferred_element_type=F32))
    o_ref[:, h * GLA_DV:(h + 1) * GLA_DV] = o
    upd = lax.dot_general(k_out, vh, (((0,), (0,)), ((), ())), preferred_element_type=F32)
    r0, r1 = h * GLA_DK, (h + 1) * GLA_DK
    st[r0:r1, :] = st[r0:r1, :] * dcol[r0:r1, :] + upd[r0:r1, :]


def _gla_kernel(qf, kf, vf, laf, qb, kb, vb, lab, of_ref, ob_ref, sf, sb):
    s = pl.program_id(1)

    @pl.when(s == 0)
    def _():
        sf[...] = jnp.zeros_like(sf)
        sb[...] = jnp.zeros_like(sb)

    c = GLA_CHUNK
    ri = lax.broadcasted_iota(jnp.int32, (c, c), 0)
    ci = lax.broadcasted_iota(jnp.int32, (c, c), 1)
    pf = _gla_prepare(qf, kf, laf, sf, (ri >= ci).astype(BF16), c - 1)
    pb = _gla_prepare(qb, kb, lab, sb, (ri <= ci).astype(BF16), 0)
    for h in range(GLA_HEADS):
        _gla_head(h, pf, vf, of_ref, sf, ri >= ci)
        _gla_head(h, pb, vb, ob_ref, sb, ci > ri)


def _gla(gq, gk, gv, laf, lab, nbatch, nctx_chunks):
    m = gq.shape[0]
    c = GLA_CHUNK
    nch = m // nbatch // c
    hk, hv = GLA_HEADS * GLA_DK, GLA_HEADS * GLA_DV

    def fi(b, s):
        return (b * nch + s, 0)

    def bi(b, s):
        cb = jnp.where(s < nctx_chunks, nctx_chunks - 1 - s, nch - 1 + nctx_chunks - s)
        return (b * nch + cb, 0)

    sk = lambda im: pl.BlockSpec((c, hk), im)
    sv = lambda im: pl.BlockSpec((c, hv), im)
    return pl.pallas_call(
        _gla_kernel,
        grid=(nbatch, nch),
        in_specs=[sk(fi), sk(fi), sv(fi), sk(fi), sk(bi), sk(bi), sv(bi), sk(bi)],
        out_specs=[sv(fi), sv(bi)],
        out_shape=[jax.ShapeDtypeStruct((m, hv), F32)] * 2,
        scratch_shapes=[pltpu.VMEM((hk, GLA_DV), F32), pltpu.VMEM((hk, GLA_DV), F32)],
        compiler_params=_cparams(("arbitrary", "arbitrary")),
        name="gla_scan",
    )(gq, gk, gv, laf, gq, gk, gv, lab)


def _swa_kernel(q_ref, kc_ref, vc_ref, kp_ref, kcur_ref, kn_ref, vp_ref, vcur_ref, vn_ref, sink_ref,
                o_ref, *, nctx_blocks, nlat_blocks):
    j = pl.program_id(1)
    w = WINDOW
    n = j - nctx_blocks
    is_lat = j >= nctx_blocks
    nkc = kc_ref.shape[0]
    kall = jnp.concatenate([kc_ref[...], kp_ref[...], kcur_ref[...], kn_ref[...]], axis=0)
    vall = jnp.concatenate([vc_ref[...], vp_ref[...], vcur_ref[...], vn_ref[...]], axis=0)
    nk = kall.shape[0]
    lane = lax.broadcasted_iota(jnp.int32, kall.shape, 1)
    lo = lane < SWA_HEAD_DIM
    kroll = pltpu.roll(kall, SWA_HEAD_DIM, axis=1)
    vroll = pltpu.roll(vall, SWA_HEAD_DIM, axis=1)
    k2 = [jnp.where(lo, kall, kroll).astype(BF16), jnp.where(lo, kroll, kall).astype(BF16)]
    v2 = [jnp.where(lo, vall, vroll).astype(BF16), jnp.where(lo, vroll, vall).astype(BF16)]
    rep = SWA_HEADS // SWA_KV_HEADS
    qi = lax.broadcasted_iota(jnp.int32, (rep * w, nk), 0) % w
    kj = lax.broadcasted_iota(jnp.int32, (rep * w, nk), 1)
    qpos = n * w + qi
    kpos = (n - 1) * w + (kj - nkc)
    loc_ok = (jnp.abs(qpos - kpos) <= WINDOW) & (kpos >= 0) & (kpos < nlat_blocks * w) & is_lat
    valid = (kj < nkc) | ((kj >= nkc) & loc_ok)
    qlane = lax.broadcasted_iota(jnp.int32, (w, 128), 1)
    qlo = qlane < SWA_HEAD_DIM
    outs = []
    for g in range(SWA_KV_HEADS):
        qs, sinks = [], []
        for r in range(rep):
            pr, half = divmod(g * rep + r, 2)
            qp = q_ref[:, 128 * pr:128 * (pr + 1)]
            qs.append(jnp.where(qlo if half == 0 else ~qlo, qp, 0.0).astype(BF16))
            sinks.append(jnp.broadcast_to(sink_ref[g * rep + r:g * rep + r + 1, 0:1], (w, 1)))
        sink = jnp.concatenate(sinks, axis=0)
        s = lax.dot_general(jnp.concatenate(qs, axis=0), k2[g], (((1,), (1,)), ((), ())),
                            preferred_element_type=F32)
        s = jnp.where(valid, s, NEG)
        mx = jnp.maximum(jnp.max(s, axis=-1, keepdims=True), sink)
        p = jnp.exp(s - mx)
        den = jnp.sum(p, axis=-1, keepdims=True) + jnp.exp(sink - mx)
        o = jnp.dot(p.astype(BF16), v2[g], preferred_element_type=F32) / den
        outs += [o[r * w:(r + 1) * w, :] for r in range(rep)]
    for pr in range(SWA_HEADS // 2):
        o_ref[:, 128 * pr:128 * (pr + 1)] = jnp.where(qlo, outs[2 * pr], outs[2 * pr + 1])


def _swa(sq, sk, sv, sink, nbatch, nctx_blocks, nlat_blocks):
    m = sq.shape[0]
    w = WINDOW
    nblk = nctx_blocks + nlat_blocks
    nkc = nctx_blocks * w

    def qmap(b, j):
        return (b * nblk + j, 0)

    def nmap(off):
        def f(b, j):
            n = jnp.clip(j - nctx_blocks + off, 0, nlat_blocks - 1)
            return (b * nblk + nctx_blocks + n, 0)
        return f

    kvw = sk.shape[1]
    kcs = pl.BlockSpec((nkc, kvw), lambda b, j: (b * (nblk * w // nkc), 0))
    kvs = lambda off: pl.BlockSpec((w, kvw), nmap(off))
    return pl.pallas_call(
        functools.partial(_swa_kernel, nctx_blocks=nctx_blocks, nlat_blocks=nlat_blocks),
        grid=(nbatch, nblk),
        in_specs=[pl.BlockSpec((w, sq.shape[1]), qmap), kcs, kcs, kvs(-1), kvs(0), kvs(1),
                  kvs(-1), kvs(0), kvs(1), _full(sink.shape)],
        out_specs=pl.BlockSpec((w, sq.shape[1]), qmap),
        out_shape=jax.ShapeDtypeStruct(sq.shape, F32),
        compiler_params=_cparams(("arbitrary", "arbitrary")),
        name="swa_attn",
    )(sq, sk, sv, sk, sk, sk, sv, sv, sv, sink)


def _residual_epilogue(x, mix, mod_ref, n2_ref, xo_ref, f_ref, ft_ref, d):
    xn = x + _mod_rows(mod_ref, 2, d) * mix
    xo_ref[...] = xn
    f = _rms(xn) * n2_ref[...] * (1.0 + _mod_rows(mod_ref, 4, d)) + _mod_rows(mod_ref, 3, d)
    f_ref[...] = f
    ft_ref[...] = f.T.astype(BF16)


def _out0_kernel(x_ref, of_ref, ob_ref, gg_ref, a_ref, mod_ref, gn_ref, wo_ref, n2_ref,
                 xo_ref, f_ref, ft_ref, *, d):
    o = of_ref[...] + ob_ref[...]
    gate = gg_ref[...]
    gate = gate * _sigmoid(gate)
    parts = []
    for h in range(GLA_HEADS):
        oh = o[:, h * GLA_DV:(h + 1) * GLA_DV]
        parts.append(oh * lax.rsqrt(jnp.mean(oh * oh, axis=-1, keepdims=True) + NORM_EPS))
    on = jnp.concatenate(parts, axis=1) * gn_ref[...] * gate
    hv = GLA_HEADS * GLA_DV
    mix = _bdot(on, wo_ref[0:hv, :]) + _bdot(a_ref[...], wo_ref[hv:, :])
    _residual_epilogue(x_ref[...], mix, mod_ref, n2_ref, xo_ref, f_ref, ft_ref, d)


def _tok_spec(n):
    return pl.BlockSpec((TM, n), lambda i: (i, 0))


def _epilogue_specs(m, d):
    out_specs = [_tok_spec(d), _tok_spec(d), pl.BlockSpec((d, TM), lambda i: (0, i))]
    out_shape = [jax.ShapeDtypeStruct((m, d), F32), jax.ShapeDtypeStruct((m, d), F32),
                 jax.ShapeDtypeStruct((d, m), BF16)]
    return out_specs, out_shape


def _out0(xs, of, ob, gg, a, mod, gn, wo, n2, nb, nbatch):
    m, d = xs.shape
    out_specs, out_shape = _epilogue_specs(m, d)
    return pl.pallas_call(
        functools.partial(_out0_kernel, d=d),
        grid=(m // TM,),
        in_specs=[_tok_spec(d), _tok_spec(of.shape[1]), _tok_spec(ob.shape[1]), _tok_spec(gg.shape[1]),
                  _tok_spec(a.shape[1]), _mod_spec(mod, nb, nbatch), _full(gn.shape), _full(wo.shape),
                  _full(n2.shape)],
        out_specs=out_specs, out_shape=out_shape,
        compiler_params=_cparams(("arbitrary",), VMEM_LIMIT),
        name="out0",
    )(xs, of, ob, gg, a, mod, gn, wo, n2)


NO_RANK = 99.0
RANK_STEP = 1e27


def _top_vals(s, k, want_rank=False):
    lw = 128
    ng = s.shape[1] // lw
    rows = lax.broadcasted_iota(jnp.int32, (k, lw), 0)
    cur = [s[:, q * lw:(q + 1) * lw] for q in range(ng)]
    vals = [jnp.zeros((k, lw), F32) for _ in range(ng)]
    for t in range(k):
        for q in range(ng):
            mx = jnp.max(cur[q], axis=0, keepdims=True)
            vals[q] = jnp.where(rows == t, mx, vals[q])
            cur[q] = jnp.where(cur[q] == mx, NEG - t * RANK_STEP, cur[q])
    vals = jnp.concatenate(vals, axis=1)
    if not want_rank:
        return vals
    cur = jnp.concatenate(cur, axis=1)
    rank = jnp.where(cur < 0.5 * NEG, jnp.floor((NEG - cur) * (1.0 / RANK_STEP) + 0.5), NO_RANK)
    return vals, rank


def _peer_scores_kernel(f_ref, wh_ref, wl_ref, kh_ref, kl_ref, s1_ref, s2_ref):
    f = f_ref[...]
    fh = f.astype(BF16)
    fl = (f - fh.astype(F32)).astype(BF16)
    wh = wh_ref[...]
    q = (jnp.dot(fh, wh, preferred_element_type=F32) + jnp.dot(fl, wh, preferred_element_type=F32)
         + jnp.dot(fh, wl_ref[...], preferred_element_type=F32))
    qh = q.astype(BF16)
    ql = (q - qh.astype(F32)).astype(BF16)
    nt = (((1,), (1,)), ((), ()))
    for h in range(PEER_HEADS):
        for z, s_ref in enumerate((s1_ref, s2_ref)):
            c0 = (2 * h + z) * PEER_NKEYS
            kh = kh_ref[z, h]
            qhs = qh[:, c0:c0 + PEER_NKEYS]
            s_ref[h] = (lax.dot_general(kh, qhs, nt, preferred_element_type=F32)
                        + lax.dot_general(kh, ql[:, c0:c0 + PEER_NKEYS], nt, preferred_element_type=F32)
                        + lax.dot_general(kl_ref[z, h], qhs, nt, preferred_element_type=F32))


def _peer_q_kernel(s1_ref, s2_ref, r2_ref, cnt_ref, e1_ref, e2_ref):
    kk = PEER_TOPK
    for h in range(PEER_HEADS):
        s1 = s1_ref[h]
        s2 = s2_ref[h]
        v1 = _top_vals(s1, kk)
        v2, rank2 = _top_vals(s2, kk, want_rank=True)
        row8 = lax.broadcasted_iota(jnp.int32, (8, s1.shape[1]), 0)
        blocks = [v1[0:1, :] + v2]
        for r in range(1, 8):
            blocks.append(jnp.where(row8 < kk // (r + 1), v1[r:r + 1, :] + v2[0:8, :], NEG))
        blocks.append(v1[8:kk, :] + v2[0:1, :])
        top = _top_vals(jnp.concatenate(blocks, axis=0), kk)
        tau = top[kk - 1:kk, :]
        z = jnp.sum(jnp.exp(top - top[0:1, :]), axis=0, keepdims=True)
        for q in range(s1.shape[1] // 128):
            ln = slice(128 * q, 128 * (q + 1))
            v2b = [jnp.broadcast_to(v2[c:c + 1, ln], (8, 128)) for c in range(kk)]
            taub = jnp.broadcast_to(tau[:, ln], (8, 128))
            for rb in range(PEER_NKEYS // 8):
                x = s1[8 * rb:8 * (rb + 1), ln]
                cnt = jnp.zeros((8, 128), F32)
                for c in range(kk):
                    cnt = cnt + jnp.where(x + v2b[c] >= taub, 1.0, 0.0)
                cnt_ref[h, 8 * rb:8 * (rb + 1), ln] = cnt
        r2_ref[h] = rank2.astype(BF16)
        e1_ref[h] = jnp.exp(s1 - v1[0:1, :])
        e2_ref[h] = (jnp.exp(s2 - v2[0:1, :]) / z).astype(BF16)


def _peer_q(f, wh, wl, k1, k2):
    m, d = f.shape
    hh, nk = PEER_HEADS, PEER_NKEYS
    big = pl.BlockSpec((hh, nk, TM), lambda i: (0, 0, i))
    sh32 = jax.ShapeDtypeStruct((hh, nk, m), F32)
    sh16 = jax.ShapeDtypeStruct((hh, nk, m), BF16)
    kf = jnp.stack([k1, k2])
    kh = kf.astype(BF16)
    kl = (kf - kh.astype(F32)).astype(BF16)
    s1, s2 = pl.pallas_call(
        _peer_scores_kernel,
        grid=(m // TM,),
        in_specs=[_tok_spec(d), _full(wh.shape), _full(wl.shape), _full(kh.shape), _full(kl.shape)],
        out_specs=[big, big],
        out_shape=[sh32, sh32],
        compiler_params=_cparams(("arbitrary",), VMEM_LIMIT),
        name="peer_scores",
    )(f, wh, wl, kh, kl)
    return pl.pallas_call(
        _peer_q_kernel,
        grid=(m // TM,),
        in_specs=[big, big],
        out_specs=[big, big, big, big],
        out_shape=[sh16, sh32, sh32, sh16],
        compiler_params=_cparams(("arbitrary",), VMEM_LIMIT),
        name="peer_query",
    )(s1, s2)


PD_TM = 512
PD_TE = 2048
PD_SUBS = (1024, 1024)


def _gelu(x):
    return 0.5 * x * (1.0 + lax.erf(x * (2.0 ** -0.5)))


def _row_bf16(tile, r, rows):
    one = jnp.broadcast_to(tile[r:r + 1, :], (8, tile.shape[1]))
    one = jnp.concatenate([one, one], axis=0).astype(BF16)
    return jnp.concatenate([one] * (rows // 16), axis=0)


def _run_if(cond, fn):
    def body(_, carry):
        fn()
        return carry
    lax.fori_loop(0, cond.astype(jnp.int32), body, 0)


def _peer_dense_kernel(ft_ref, u_ref, vt_ref, r2_ref, cnt_ref, e1_ref, e2_ref, o_ref, acc_ref):
    k = pl.program_id(1)
    nk = PEER_NKEYS
    nslab = PD_TE // nk
    zero = jnp.zeros((nk, PD_TM), BF16)
    nsub = len(PD_SUBS)
    starts = [sum(PD_SUBS[:i]) for i in range(nsub + 1)]
    assert starts[-1] == PD_TE

    def first():
        acc_ref[...] = jnp.zeros_like(acc_ref)

    _run_if(k == 0, first)

    def hidden(sb):
        return jnp.dot(u_ref[starts[sb]:starts[sb + 1], :], ft_ref[...], preferred_element_type=F32)

    ht_next = hidden(0)
    for sb in range(nsub):
        e0, e1x = starts[sb], starts[sb + 1]
        ht = ht_next
        if sb + 1 < nsub:
            ht_next = hidden(sb + 1)
        gates = []
        for al in range(e0 // nk, e1x // nk):
            wg = zero
            a0 = pl.multiple_of(k * nslab + 8 * (al // 8), 8)
            for h in range(PEER_HEADS):
                cnt = _row_bf16(cnt_ref[h, pl.ds(a0, 8), :], al % 8, nk)
                e1 = _row_bf16(e1_ref[h, pl.ds(a0, 8), :], al % 8, nk)
                wg = wg + jnp.where(r2_ref[h] < cnt, e2_ref[h], zero) * e1
            gates.append(wg)
        ct = _gelu(ht).astype(BF16) * jnp.concatenate(gates, axis=0)
        acc_ref[...] += jnp.dot(vt_ref[:, e0:e1x], ct, preferred_element_type=F32)

    def last():
        o_ref[...] = acc_ref[...].T

    _run_if(k == pl.num_programs(1) - 1, last)


def _peer_dense(ft, u, vt, layer, r2, cnt, e1, e2):
    d, m = ft.shape
    ne = u.shape[1]
    hh, nk = PEER_HEADS, PEER_NKEYS
    big = pl.BlockSpec((hh, nk, PD_TM), lambda i, k: (0, 0, i))
    return pl.pallas_call(
        _peer_dense_kernel,
        grid=(m // PD_TM, ne // PD_TE),
        in_specs=[pl.BlockSpec((d, PD_TM), lambda i, k: (0, i)),
                  pl.BlockSpec((None, PD_TE, d), lambda i, k: (layer, k, 0)),
                  pl.BlockSpec((None, None, d, PD_TE), lambda i, k: (layer, k, 0, 0)),
                  big, big, big, big],
        out_specs=pl.BlockSpec((PD_TM, d), lambda i, k: (i, 0)),
        out_shape=jax.ShapeDtypeStruct((m, d), F32),
        scratch_shapes=[pltpu.VMEM((d, PD_TM), F32)],
        compiler_params=_cparams(("arbitrary", "arbitrary"), VMEM_LIMIT),
        name="peer_dense",
    )(ft, u, vt, r2, cnt, e1, e2)


def _res_kernel(x_ref, p_ref, mod_ref, modn_ref, g_ref, xo_ref, u_ref, *, d):
    xn = x_ref[...] + _mod_rows(mod_ref, 5, d) * p_ref[...]
    xo_ref[...] = xn
    u_ref[...] = _rms(xn) * g_ref[...] * (1.0 + _mod_rows(modn_ref, 1, d)) + _mod_rows(modn_ref, 0, d)


def _res(xs, p, mod, modn, g, nb, nbatch):
    m, d = xs.shape
    return pl.pallas_call(
        functools.partial(_res_kernel, d=d),
        grid=(m // TM,),
        in_specs=[_tok_spec(d), _tok_spec(d), _mod_spec(mod, nb, nbatch), _mod_spec(modn, nb, nbatch),
                  _full(g.shape)],
        out_specs=[_tok_spec(d), _tok_spec(d)],
        out_shape=[jax.ShapeDtypeStruct((m, d), F32)] * 2,
        compiler_params=_cparams(("arbitrary",)),
        name="peer_residual",
    )(xs, p, mod, modn, g)


def _final_kernel(x_ref, p_ref, mod_ref, g_ref, y_ref, *, d):
    y_ref[...] = _rms(x_ref[...] + _mod_rows(mod_ref, 5, d) * p_ref[...]) * g_ref[...]


def _final(xs, p, mod, g, nb):
    m, d = xs.shape
    return pl.pallas_call(
        functools.partial(_final_kernel, d=d),
        grid=(m // TM,),
        in_specs=[_tok_spec(d), _tok_spec(d), _mod_spec_lat(mod, nb - 1), _full(g.shape)],
        out_specs=_tok_spec(d),
        out_shape=jax.ShapeDtypeStruct((m, d), F32),
        compiler_params=_cparams(("arbitrary",)),
        name="final_norm",
    )(xs, p, mod, g)


def _split_dot(x, w):
    hi = x.astype(BF16)
    lo = (x - hi.astype(F32)).astype(BF16)
    return jnp.dot(hi, w, preferred_element_type=F32) + jnp.dot(lo, w, preferred_element_type=F32)


def _seg_sum(x, g_ref, gt_ref):
    return _split_dot(_split_dot(x, g_ref[...]), gt_ref[...])


def _rwkv_prep_kernel(u_ref, up_ref, un_ref, mu_ref, wr_ref, wk_ref, wv_ref, w0_ref, w1_ref, w2_ref,
                      a0_ref, a1_ref, a2_ref, g1_ref, g2_ref, kk_ref_, ka_ref, rk_ref, sg_ref, sgt_ref,
                      wf_o, wb_o, kf_o, kb_o, bf_o, bb_o, r_o, v_o, kk_o, g_o, bonus_o, *, nb, d):
    i = pl.program_id(0)
    p = i % nb
    u = u_ref[...]
    has_prev = (p >= 2).astype(F32)
    has_next = ((p >= 1) & (p <= nb - 2)).astype(F32)
    prev_row = up_ref[7:8, :] * has_prev
    next_row = un_ref[0:1, :] * has_next
    rows = lax.broadcasted_iota(jnp.int32, u.shape, 0)
    up = jnp.where(rows == 0, prev_row, pltpu.roll(u, 1, axis=0))
    un = jnp.where(rows == TM - 1, next_row, pltpu.roll(u, TM - 1, axis=0))
    xx = 0.5 * (up + un) - u

    def mix(zi):
        return (u + xx * mu_ref[zi:zi + 1, :]).astype(BF16)

    r = jnp.dot(mix(0), wr_ref[...], preferred_element_type=F32)
    k = jnp.dot(mix(2), wk_ref[...], preferred_element_type=F32)
    v = jnp.dot(mix(3), wv_ref[...], preferred_element_type=F32)
    lw = w0_ref[...] + _bdot(jnp.tanh(jnp.dot(mix(1), w1_ref[...], preferred_element_type=F32)), w2_ref[...])
    decay = jnp.exp(-_sigmoid(lw) * float(np.exp(-0.5)))
    a = _sigmoid(a0_ref[...] + _bdot(jnp.dot(mix(4), a1_ref[...], preferred_element_type=F32), a2_ref[...]))
    g = _bdot(_sigmoid(jnp.dot(mix(5), g1_ref[...], preferred_element_type=F32)), g2_ref[...])
    kk = k * kk_ref_[...]
    kk = kk * lax.rsqrt(jnp.maximum(_seg_sum(kk * kk, sg_ref, sgt_ref), 1e-24))
    ka = ka_ref[...]
    kf = k * (1.0 + (a[:, :d] - 1.0) * ka)
    kb = k * (1.0 + (a[:, d:] - 1.0) * ka)
    bonus = _seg_sum(r * (kf + kb) * rk_ref[...], sg_ref, sgt_ref) * v
    wf_o[...] = decay[:, :d]
    wb_o[...] = decay[:, d:]
    kf_o[...] = kf
    kb_o[...] = kb
    bf_o[...] = kk * a[:, :d]
    bb_o[...] = kk * a[:, d:]
    r_o[...] = r
    v_o[...] = v
    kk_o[...] = kk
    g_o[...] = g
    bonus_o[...] = bonus


def _rwkv_prep(u, consts, nb):
    m, d = u.shape
    r8 = TM // 8
    nb8 = m // 8
    in_specs = [_tok_spec(d),
                pl.BlockSpec((8, d), lambda i: (jnp.maximum(i * r8 - 1, 0), 0)),
                pl.BlockSpec((8, d), lambda i: (jnp.minimum((i + 1) * r8, nb8 - 1), 0))]
    in_specs += [_full(c.shape) for c in consts]
    return pl.pallas_call(
        functools.partial(_rwkv_prep_kernel, nb=nb, d=d),
        grid=(m // TM,),
        in_specs=in_specs,
        out_specs=[_tok_spec(d)] * 11,
        out_shape=[jax.ShapeDtypeStruct((m, d), F32)] * 11,
        compiler_params=_cparams(("arbitrary",), VMEM_LIMIT),
        name="rwkv_prep",
    )(u, u, u, *consts)


RW_TC = 256
RW_UNROLL = 4


def _rwkv_scan_kernel(wf, kf, bf, rf, vf, kkf, wb, kb, bb, rb, vb, kkb, yf_ref, yb_ref,
                      st, vk_a, vk_b, ycol_scr, sa_scr, *, npair):
    s = pl.program_id(1)
    n = RW_HEAD_DIM
    grp = 8

    @pl.when(s == 0)
    def _():
        st[...] = jnp.zeros_like(st)
        ycol_scr[...] = jnp.zeros_like(ycol_scr)

    w2 = 2 * n
    ri = lax.broadcasted_iota(jnp.int32, (n, w2), 0)
    li = lax.broadcasted_iota(jnp.int32, (n, w2), 1)
    eye_a = (li == ri).astype(BF16)
    bi = lax.broadcasted_iota(jnp.int32, (w2, w2), 0)
    bj = lax.broadcasted_iota(jnp.int32, (w2, w2), 1)
    ones_blk = ((bi // n) == (bj // n)).astype(BF16)
    b4i = lax.broadcasted_iota(jnp.int32, (2 * w2, 2 * w2), 0)
    b4j = lax.broadcasted_iota(jnp.int32, (2 * w2, 2 * w2), 1)
    ones_blk2 = ((b4i // n) == (b4j // n)).astype(BF16)
    lane8 = lax.broadcasted_iota(jnp.int32, (grp, w2), 1)
    rows16 = lax.broadcasted_iota(jnp.int32, (2 * grp, w2), 0)
    lane16 = lax.broadcasted_iota(jnp.int32, (2 * grp, w2), 1)
    untr = ((lane16 == rows16) | (lane16 == rows16 + n)).astype(BF16)
    dirs = ((wf, kf, bf, rf, vf, kkf, yf_ref), (wb, kb, bb, rb, vb, kkb, yb_ref))
    nch = 2 * npair
    ngrp = RW_TC // grp
    nt = (((1,), (1,)), ((), ()))

    def halves(tile):
        return jnp.concatenate([jnp.where(lane8 < n, tile, 0.0), jnp.where(lane8 >= n, tile, 0.0)], axis=0)

    assert nch == 2 * grp

    def group_base(g, dd):
        base = (g if dd == 0 else ngrp - 1 - g) * grp
        return base if isinstance(base, int) else pl.multiple_of(base, grp)

    def lanes(c):
        p = c % npair
        return slice(w2 * p, w2 * (p + 1))

    def precompute(g, chains, vk_dst):
        gq = jnp.minimum(g, ngrp - 1)
        v2, ks = [], []
        for c in chains:
            dd = c // npair
            base = group_base(gq, dd)
            v_t = dirs[dd][4][pl.ds(base, grp), lanes(c)]
            ks.append(dirs[dd][1][pl.ds(base, grp), lanes(c)])
            v2 += [jnp.where(lane8 < n, v_t, 0.0), jnp.where(lane8 < n, pltpu.roll(v_t, n, axis=1), 0.0)]
        vt = lax.dot_general(eye_a, jnp.concatenate(v2, axis=0).astype(BF16), nt,
                             preferred_element_type=F32).astype(BF16)
        for idx, c in enumerate(chains):
            k2 = halves(ks[idx])
            kblk = jnp.concatenate(
                [jnp.where((rows16 == j) | (rows16 == grp + j), k2, 0.0) for j in range(grp)], axis=1)
            vk_dst[c] = jnp.dot(vt[:, 2 * grp * idx:2 * grp * (idx + 1)], kblk.astype(BF16),
                                preferred_element_type=F32)

    def untranspose(g, chains, ycol_prev):
        for c in chains:
            dd = c // npair
            yc = ycol_prev[c]
            ycs = jnp.concatenate([jnp.where(li < n, yc, 0.0), jnp.where(li >= n, yc, 0.0)], axis=0)
            yt = lax.dot_general(untr, ycs.astype(BF16), nt, preferred_element_type=F32)
            dirs[dd][6][pl.ds(group_base(g, dd), grp), lanes(c)] = yt[0:grp, :]

    def run_group(g, vk_cur, vk_next, states, sa, ycol_prev):
        tiles = []
        for c in range(nch):
            dd = c // npair
            base = group_base(g, dd)
            tiles.append([dirs[dd][q][pl.ds(base, grp), lanes(c)] for q in (0, 2, 3, 5)])
        gprev = jnp.maximum(g - 1, 0)
        gnext = jnp.minimum(g + 1, ngrp - 1)
        kap_next = []
        for c in range(nch):
            dd = c // npair
            kap_next.append(dirs[dd][5][pl.ds(group_base(gnext, dd), grp), lanes(c)])
        states, sa = list(states), list(sa)
        ycol = [jnp.zeros((n, w2), F32) for _ in range(nch)]
        for j in range(grp):
            lhs = []
            for c in range(nch):
                w_t, b_t, r_t, kk_t = tiles[c]
                fwd = c < npair
                jj = j if fwd else grp - 1 - j
                row = lambda tl, q: jnp.broadcast_to(tl[q:q + 1, :], (n, w2))
                states[c] = (states[c] * row(w_t, jj) - sa[c] * row(b_t, jj)
                             + vk_cur[c, :, jj * w2:(jj + 1) * w2])
                if j + 1 < grp:
                    kap = row(kk_t, jj + 1 if fwd else jj - 1)
                else:
                    kap = row(kap_next[c], 0 if fwd else grp - 1)
                lhs.append(jnp.concatenate([(states[c] * kap).astype(BF16),
                                            (states[c] * row(r_t, jj)).astype(BF16)], axis=1))
            red = jnp.dot(jnp.concatenate(lhs, axis=0), ones_blk2, preferred_element_type=F32)
            for c in range(nch):
                jj = j if c < npair else grp - 1 - j
                sa[c] = red[c * n:(c + 1) * n, :w2]
                ycol[c] = jnp.where((li % n) == jj, red[c * n:(c + 1) * n, w2:], ycol[c])
            pair = (2 * j, 2 * j + 1)
            precompute(g + 1, pair, vk_next)
            untranspose(gprev, pair, ycol_prev)
        return states, sa, ycol

    precompute(0, tuple(range(nch)), vk_a)
    first = []
    for c in range(nch):
        dd = c // npair
        kk_t = dirs[dd][5][pl.ds(group_base(0, dd), grp), lanes(c)]
        q0 = 0 if dd == 0 else grp - 1
        first.append((st[c] * jnp.broadcast_to(kk_t[q0:q0 + 1, :], (n, w2))).astype(BF16))
    sa0 = jnp.dot(jnp.concatenate(first, axis=0), ones_blk, preferred_element_type=F32)
    for c in range(nch):
        sa_scr[c] = sa0[c * n:(c + 1) * n, :]

    def two_groups(i, carry):
        states = [st[c] for c in range(nch)]
        sa = [sa_scr[c] for c in range(nch)]
        ycol = [ycol_scr[c] for c in range(nch)]
        for u in range(0, RW_UNROLL, 2):
            states, sa, ycol = run_group(RW_UNROLL * i + u, vk_a, vk_b, states, sa, ycol)
            states, sa, ycol = run_group(RW_UNROLL * i + u + 1, vk_b, vk_a, states, sa, ycol)
        for c in range(nch):
            st[c] = states[c]
            sa_scr[c] = sa[c]
            ycol_scr[c] = ycol[c]
        return carry

    lax.fori_loop(0, ngrp // RW_UNROLL, two_groups, 0)
    untranspose(ngrp - 1, tuple(range(nch)), [ycol_scr[c] for c in range(nch)])


def _rwkv_scan(wf, kf, bf, wb, kb, bb, r, v, kk, nbatch):
    m, d = r.shape
    nch = m // nbatch // RW_TC
    npair = d // (2 * RW_HEAD_DIM)

    def fi(b, s):
        return (b * nch + s, 0)

    def bi(b, s):
        return (b * nch + jnp.where(s == 0, 0, nch - s), 0)

    sf = pl.BlockSpec((RW_TC, d), fi)
    sb = pl.BlockSpec((RW_TC, d), bi)
    return pl.pallas_call(
        functools.partial(_rwkv_scan_kernel, npair=npair),
        grid=(nbatch, nch),
        in_specs=[sf] * 6 + [sb] * 6,
        out_specs=[sf, sb],
        out_shape=[jax.ShapeDtypeStruct((m, d), F32)] * 2,
        scratch_shapes=[pltpu.VMEM((2 * npair, RW_HEAD_DIM, 2 * RW_HEAD_DIM), F32),
                        pltpu.VMEM((2 * npair, RW_HEAD_DIM, 8 * 2 * RW_HEAD_DIM), F32),
                        pltpu.VMEM((2 * npair, RW_HEAD_DIM, 8 * 2 * RW_HEAD_DIM), F32),
                        pltpu.VMEM((2 * npair, RW_HEAD_DIM, 2 * RW_HEAD_DIM), F32),
                        pltpu.VMEM((2 * npair, RW_HEAD_DIM, 2 * RW_HEAD_DIM), F32)],
        compiler_params=_cparams(("arbitrary", "arbitrary"), VMEM_LIMIT),
        name="rwkv_scan",
    )(wf, kf, bf, r, v, kk, wb, kb, bb, r, v, kk)


def _rwkv_out_kernel(x_ref, yf_ref, yb_ref, bonus_ref, g_ref, mod_ref, lg_ref, lb_ref, wo_ref, n2_ref,
                     sg_ref, sgt_ref, xo_ref, f_ref, ft_ref, *, d):
    y = yf_ref[...] + yb_ref[...]
    inv = 1.0 / RW_HEAD_DIM
    mean = _seg_sum(y, sg_ref, sgt_ref) * inv
    yc = y - mean
    var = _seg_sum(yc * yc, sg_ref, sgt_ref) * inv
    yn = yc * lax.rsqrt(var + RW_GN_EPS) * lg_ref[...] + lb_ref[...]
    out = (yn + bonus_ref[...]) * g_ref[...]
    mix = _bdot(out, wo_ref[...])
    _residual_epilogue(x_ref[...], mix, mod_ref, n2_ref, xo_ref, f_ref, ft_ref, d)


def _rwkv_out(xs, yf, yb, bonus, g, mod, lg, lb, wo, n2, sg, sgt, nb, nbatch):
    d = xs.shape[1]
    m = nbatch * (nb - 1) * TM
    out_specs, out_shape = _epilogue_specs(m, d)
    return pl.pallas_call(
        functools.partial(_rwkv_out_kernel, d=d),
        grid=(m // TM,),
        in_specs=[_lat_tok_spec(d, nb)] * 5 + [_mod_spec_lat(mod, nb - 1)]
        + [_full(a.shape) for a in (lg, lb, wo, n2, sg, sgt)],
        out_specs=out_specs, out_shape=out_shape,
        compiler_params=_cparams(("arbitrary",), VMEM_LIMIT),
        name="rwkv_out",
    )(xs, yf, yb, bonus, g, mod, lg, lb, wo, n2, sg, sgt)


def _rope_tables(t_lat, n_ctx):
    rows = t_lat // GRID_W
    row = jnp.broadcast_to(jnp.arange(rows, dtype=F32)[:, None], (rows, GRID_W)).reshape(-1)
    col = jnp.broadcast_to(jnp.arange(GRID_W, dtype=F32)[None, :], (rows, GRID_W)).reshape(-1)
    n_freq = SWA_HEAD_DIM // 4
    inv_freq = ROPE_BASE ** (-jnp.arange(n_freq, dtype=F32) / n_freq)
    ar = row[:, None] * inv_freq
    ac = col[:, None] * inv_freq
    cos = jnp.concatenate([jnp.cos(ar), jnp.cos(ar), jnp.cos(ac), jnp.cos(ac)], axis=1)
    sin = jnp.concatenate([-jnp.sin(ar), jnp.sin(ar), -jnp.sin(ac), jnp.sin(ac)], axis=1)
    cos = jnp.concatenate([jnp.ones((n_ctx, SWA_HEAD_DIM), F32), cos], axis=0)
    sin = jnp.concatenate([jnp.zeros((n_ctx, SWA_HEAD_DIM), F32), sin], axis=0)
    return jnp.tile(cos, (1, 2)), jnp.tile(sin, (1, 2))


def _peer(f, ft, w_q, k1, k2, u_all, vt_all, layer):
    wh = w_q.astype(BF16)
    wl = (w_q - wh.astype(F32)).astype(BF16)
    r2, cnt, e1, e2 = _peer_q(f, wh, wl, k1, k2)
    return _peer_dense(ft, u_all, vt_all, layer, r2, cnt, e1, e2)


def kernel(x, c, ctx, c_ctx, ada_w, ada_b, norm1_g, norm2_g, ab_w_in, gla_dec_w2, gla_dec_b, gla_norm_g,
           swa_sink, ab_w_out, rw_mu, rw_w_rkv, rw_w_o, rw_w0, rw_w1, rw_w2, rw_a0, rw_a1, rw_a2, rw_g1,
           rw_g2, rw_k_k, rw_k_a, rw_r_k, rw_ln_g, rw_ln_b, peer_w_q, peer_k1, peer_k2, peer_u, peer_v,
           final_g):
    nbatch, t_lat, d = x.shape
    n_ctx = ctx.shape[1]
    assert n_ctx == TM and t_lat % TM == 0 and d == 1024
    s_tok = n_ctx + t_lat
    nb = s_tok // TM
    m = nbatch * s_tok
    xs = jnp.concatenate([ctx, x], axis=1).reshape(m, d)

    cc = jnp.zeros((16, d), F32).at[:nbatch].set(c).at[nbatch].set(c_ctx)
    mod0 = _ada_table(cc, ada_w[0], ada_b[0]).reshape(16, 1, 6 * d)
    mod1 = _ada_table(cc, ada_w[1], ada_b[1]).reshape(16, 1, 6 * d)
    row2 = lambda a: a.reshape(1, -1)

    w_in = ab_w_in[0]
    cuts = np.cumsum((256, 256, 512, 512, 32, 512, 128, 128))[:-1].tolist()
    wgq, wgk, wgv, wgg, wlr, wsq, wsk, wsv = jnp.split(w_in, cuts, axis=1)
    wp = jnp.concatenate([wgq, wgk, wgv, wgg, wsq, wsk, wsv, wlr,
                          jnp.zeros((d, _P_END - _P_LR - 32), F32)], axis=1).astype(BF16)
    hk = GLA_HEADS * GLA_DK
    w2p = jnp.zeros((128, 2 * hk), F32)
    w2p = w2p.at[0:GLA_LOWRANK, 0:hk].set(gla_dec_w2[0, 0]).at[GLA_LOWRANK:2 * GLA_LOWRANK, hk:].set(gla_dec_w2[0, 1])
    db = gla_dec_b[0].reshape(1, 2 * hk)
    cos, sin = _rope_tables(t_lat, n_ctx)
    gq, gk, gv, gg, laf, lab, sq, sk, sv = _proj0(xs, mod0, row2(norm1_g[0]), wp, w2p, db, cos, sin, nb, nbatch)
    of, ob = _gla(gq, gk, gv, laf, lab, nbatch, n_ctx // GLA_CHUNK)
    sink = jnp.broadcast_to(swa_sink[0][:, None], (SWA_HEADS, 128))
    att = _swa(sq, sk, sv, sink, nbatch, n_ctx // WINDOW, t_lat // WINDOW)
    xs, f, ft = _out0(xs, of, ob, gg, att, mod0, row2(gla_norm_g[0]), ab_w_out[0].astype(BF16),
                      row2(norm2_g[0]), nb, nbatch)
    u_all = peer_u.astype(BF16)
    vt_all = jnp.swapaxes(peer_v.reshape(peer_v.shape[0], -1, PD_TE, d), 2, 3).astype(BF16)
    p = _peer(f, ft, peer_w_q[0], peer_k1[0], peer_k2[0], u_all, vt_all, 0)
    xs, u = _res(xs, p, mod0, mod1, row2(norm1_g[1]), nb, nbatch)

    lora = rw_w1.shape[-1]
    w1c = jnp.concatenate([rw_w1[0, 0], rw_w1[0, 1]], axis=1).astype(BF16)
    w2c = jnp.zeros((2 * lora, 2 * d), F32).at[:lora, :d].set(rw_w2[0, 0]).at[lora:, d:].set(rw_w2[0, 1]).astype(BF16)
    la = rw_a1.shape[-1]
    a1c = jnp.concatenate([rw_a1[0, 0], rw_a1[0, 1]], axis=1).astype(BF16)
    a2c = jnp.zeros((2 * la, 2 * d), F32).at[:la, :d].set(rw_a2[0, 0]).at[la:, d:].set(rw_a2[0, 1]).astype(BF16)
    lg = rw_g1.shape[-1]
    g1p = jnp.zeros((d, 256), F32).at[:, :lg].set(rw_g1[0]).astype(BF16)
    g2p = jnp.zeros((256, d), F32).at[:lg].set(rw_g2[0]).astype(BF16)
    sg = (jnp.arange(d)[:, None] // RW_HEAD_DIM == jnp.arange(128)[None, :]).astype(BF16)
    sgt = sg.T
    consts = [rw_mu[0], rw_w_rkv[0, 0].astype(BF16), rw_w_rkv[0, 1].astype(BF16), rw_w_rkv[0, 2].astype(BF16),
              rw_w0[0].reshape(1, 2 * d), w1c, w2c, rw_a0[0].reshape(1, 2 * d), a1c, a2c, g1p, g2p,
              row2(rw_k_k[0]), row2(rw_k_a[0]), rw_r_k[0].reshape(1, d), sg, sgt]
    mu8 = jnp.zeros((8, d), F32).at[:6].set(rw_mu[0])
    consts[0] = mu8
    wf, wb, kf, kb, bf, bb, r, v, kk, g, bonus = _rwkv_prep(u, consts, nb)
    yf, yb = _rwkv_scan(wf, kf, bf, wb, kb, bb, r, v, kk, nbatch)
    xs, f, ft = _rwkv_out(xs, yf, yb, bonus, g, mod1, row2(rw_ln_g[0]), row2(rw_ln_b[0]),
                          rw_w_o[0].astype(BF16), row2(norm2_g[1]), sg, sgt, nb, nbatch)
    p = _peer(f, ft, peer_w_q[1], peer_k1[1], peer_k2[1], u_all, vt_all, 1)
    y = _final(xs, p, mod1, row2(final_g), nb)
    return y.reshape(nbatch, t_lat, d)
```

```python
import functools

import numpy as np
import jax
import jax.numpy as jnp
from jax import lax
from jax.experimental import pallas as pl
from jax.experimental.pallas import tpu as pltpu

F32 = jnp.float32
BF16 = jnp.bfloat16
HI = lax.Precision.HIGHEST

NORM_EPS = 1e-6
GLA_HEADS, GLA_DK, GLA_DV, GLA_LOWRANK, GLA_TAU, GLA_CHUNK = 4, 64, 128, 16, 16.0, 64
SWA_HEADS, SWA_KV_HEADS, SWA_HEAD_DIM, WINDOW = 8, 2, 64, 128
ROPE_BASE = 10000.0
GRID_W = 64
RW_HEAD_DIM = 64
RW_GN_EPS = 64e-5
PEER_HEADS, PEER_NKEYS, PEER_TOPK = 8, 128, 16
NEG = -1e30

TM = 256
VMEM_LIMIT = 56 * 1024 * 1024


def _cparams(sem, vmem=None):
    return pltpu.CompilerParams(dimension_semantics=sem, vmem_limit_bytes=vmem)


def _bdot(a, b):
    return jnp.dot(a.astype(BF16), b.astype(BF16), preferred_element_type=F32)


def _hdot(a, b):
    return jnp.dot(a, b, precision=HI, preferred_element_type=F32)


def _sigmoid(x):
    return 1.0 / (1.0 + jnp.exp(-x))


def _rms(x):
    return x * lax.rsqrt(jnp.mean(x * x, axis=-1, keepdims=True) + NORM_EPS)


def _mod_rows(mod_ref, k, d):
    return mod_ref[0, :, k * d:(k + 1) * d]


def _mod_spec(mod, nb, nbatch):
    return pl.BlockSpec((1, 1, mod.shape[2]), lambda i: (jnp.where(i % nb == 0, nbatch, i // nb), 0, 0))


def _mod_spec_lat(mod, nbl):
    return pl.BlockSpec((1, 1, mod.shape[2]), lambda i: (i // nbl, 0, 0))


def _lat_tok_spec(n, nb):
    return pl.BlockSpec((TM, n), lambda i: ((i // (nb - 1)) * nb + 1 + i % (nb - 1), 0))


def _full(shape):
    n = len(shape)
    return pl.BlockSpec(shape, lambda *_: (0,) * n)


def _ada_kernel(c_ref, w_ref, b_ref, o_ref):
    c = c_ref[...]
    s = c * _sigmoid(c)
    o_ref[...] = _hdot(s, w_ref[...]) + b_ref[...]


def _ada_table(cc, w, b):
    rows, d = cc.shape
    n = w.shape[1]
    tn = 512
    return pl.pallas_call(
        _ada_kernel,
        grid=(n // tn,),
        in_specs=[_full((rows, d)), pl.BlockSpec((d, tn), lambda j: (0, j)),
                  pl.BlockSpec((1, tn), lambda j: (0, j))],
        out_specs=pl.BlockSpec((rows, tn), lambda j: (0, j)),
        out_shape=jax.ShapeDtypeStruct((rows, n), F32),
        compiler_params=_cparams(("arbitrary",)),
        name="ada_table",
    )(cc, w, b.reshape(1, n))


_P_GQ, _P_GK, _P_GV, _P_GG, _P_SQ, _P_SK, _P_SV, _P_LR, _P_END = 0, 256, 512, 1024, 1536, 2048, 2176, 2304, 2432


def _rope(x, cos, sin):
    lane = lax.broadcasted_iota(jnp.int32, x.shape, 1)
    up = pltpu.roll(x, 112, axis=1)
    dn = pltpu.roll(x, 16, axis=1)
    sw = jnp.where((lane % 32) < 16, up, dn)
    return x * cos + sw * sin


def _proj0_kernel(x_ref, mod_ref, g_ref, w_ref, w2_ref, db_ref, cos_ref, sin_ref,
                  gq_ref, gk_ref, gv_ref, gg_ref, laf_ref, lab_ref, sq_ref, sk_ref, sv_ref,
                  *, d):
    u = (_rms(x_ref[...]) * g_ref[...] * (1.0 + _mod_rows(mod_ref, 1, d))
         + _mod_rows(mod_ref, 0, d)).astype(BF16)

    def seg(a, b):
        return jnp.dot(u, w_ref[:, a:b], preferred_element_type=F32)

    gq_ref[...] = seg(_P_GQ, _P_GK) * (GLA_DK ** -0.5)
    gk_ref[...] = seg(_P_GK, _P_GV)
    gv_ref[...] = seg(_P_GV, _P_GG)
    gg_ref[...] = seg(_P_GG, _P_SQ)
    lr = seg(_P_LR, _P_END)
    z = _hdot(lr, w2_ref[...]) + db_ref[...]
    la = (jnp.minimum(z, 0.0) - jnp.log(1.0 + jnp.exp(-jnp.abs(z)))) * (1.0 / GLA_TAU)
    hk = GLA_HEADS * GLA_DK
    laf_ref[...] = la[:, :hk]
    lab_ref[...] = la[:, hk:]
    cos = cos_ref[...]
    sin = sin_ref[...]
    sq = seg(_P_SQ, _P_SK) * (SWA_HEAD_DIM ** -0.5)
    for c in range(4):
        sq_ref[:, 128 * c:128 * (c + 1)] = _rope(sq[:, 128 * c:128 * (c + 1)], cos, sin)
    sk_ref[...] = _rope(seg(_P_SK, _P_SV), cos, sin)
    sv_ref[...] = seg(_P_SV, _P_LR)


def _proj0(xs, mod, g, w, w2, db, cos, sin, nb, nbatch):
    m, d = xs.shape
    widths = (256, 256, 512, 512, 256, 256, 512, 128, 128)
    return pl.pallas_call(
        functools.partial(_proj0_kernel, d=d),
        grid=(m // TM,),
        in_specs=[pl.BlockSpec((TM, d), lambda i: (i, 0)), _mod_spec(mod, nb, nbatch), _full((1, d)),
                  _full(w.shape), _full(w2.shape), _full(db.shape),
                  pl.BlockSpec((TM, 128), lambda i: (i % nb, 0)),
                  pl.BlockSpec((TM, 128), lambda i: (i % nb, 0))],
        out_specs=[pl.BlockSpec((TM, n), lambda i: (i, 0)) for n in widths],
        out_shape=[jax.ShapeDtypeStruct((m, n), F32) for n in widths],
        compiler_params=_cparams(("arbitrary",), VMEM_LIMIT),
        name="proj0",
    )(xs, mod, g, w, w2, db, cos, sin)


def _gla_prepare(q_ref, k_ref, la_ref, st, cum, tot_row):
    c = GLA_CHUNK
    la = la_ref[...]
    la_hi = la.astype(BF16)
    la_lo = (la - la_hi.astype(F32)).astype(BF16)
    bc = (jnp.dot(cum, la_hi, preferred_element_type=F32)
          + jnp.dot(cum, la_lo, preferred_element_type=F32))
    tot = bc[tot_row:tot_row + 1, :]
    q_in = q_ref[...] * jnp.exp(bc)
    k_in = (k_ref[...] * jnp.exp(-bc)).astype(BF16)
    k_out = (k_ref[...] * jnp.exp(tot - bc)).astype(BF16)
    ones = jnp.ones((c, GLA_DV), BF16)
    tn = (((0,), (0,)), ((), ()))
    dcol = jnp.exp(lax.dot_general(la_hi, ones, tn, preferred_element_type=F32)
                   + lax.dot_general(la_lo, ones, tn, preferred_element_type=F32))
    return q_in, k_in, k_out, dcol, st[...].astype(BF16)


def _gla_head(h, prep, v_ref, o_ref, st, mask):
    q_in, k_in, k_out, dcol, s_prev = prep
    lane = lax.broadcasted_iota(jnp.int32, q_in.shape, 1)
    qh = jnp.where((lane // GLA_DK) == h, q_in, 0.0).astype(BF16)
    att = lax.dot_general(qh, k_in, (((1,), (1,)), ((), ())), preferred_element_type=F32)
    att = jnp.where(mask, att, 0.0).astype(BF16)
    vh = v_ref[:, h * GLA_DV:(h + 1) * GLA_DV].astype(BF16)
    o = (jnp.dot(att, vh, preferred_element_type=F32)
         + jnp.dot(qh, s_prev, preferred_element_type=F32))
    o_ref[:, h * GLA_DV:(h + 1) * GLA_DV] = o
    upd = lax.dot_general(k_out, vh, (((0,), (0,)), ((), ())), preferred_element_type=F32)
    r0, r1 = h * GLA_DK, (h + 1) * GLA_DK
    st[r0:r1, :] = st[r0:r1, :] * dcol[r0:r1, :] + upd[r0:r1, :]


def _gla_kernel(qf, kf, vf, laf, qb, kb, vb, lab, of_ref, ob_ref, sf, sb):
    s = pl.program_id(1)

    @pl.when(s == 0)
    def _():
        sf[...] = jnp.zeros_like(sf)
        sb[...] = jnp.zeros_like(sb)

    c = GLA_CHUNK
    ri = lax.broadcasted_iota(jnp.int32, (c, c), 0)
    ci = lax.broadcasted_iota(jnp.int32, (c, c), 1)
    pf = _gla_prepare(qf, kf, laf, sf, (ri >= ci).astype(BF16), c - 1)
    pb = _gla_prepare(qb, kb, lab, sb, (ri <= ci).astype(BF16), 0)
    for h in range(GLA_HEADS):
        _gla_head(h, pf, vf, of_ref, sf, ri >= ci)
        _gla_head(h, pb, vb, ob_ref, sb, ci > ri)


def _gla(gq, gk, gv, laf, lab, nbatch, nctx_chunks):
    m = gq.shape[0]
    c = GLA_CHUNK
    nch = m // nbatch // c
    hk, hv = GLA_HEADS * GLA_DK, GLA_HEADS * GLA_DV

    def fi(b, s):
        return (b * nch + s, 0)

    def bi(b, s):
        cb = jnp.where(s < nctx_chunks, nctx_chunks - 1 - s, nch - 1 + nctx_chunks - s)
        return (b * nch + cb, 0)

    sk = lambda im: pl.BlockSpec((c, hk), im)
    sv = lambda im: pl.BlockSpec((c, hv), im)
    return pl.pallas_call(
        _gla_kernel,
        grid=(nbatch, nch),
        in_specs=[sk(fi), sk(fi), sv(fi), sk(fi), sk(bi), sk(bi), sv(bi), sk(bi)],
        out_specs=[sv(fi), sv(bi)],
        out_shape=[jax.ShapeDtypeStruct((m, hv), F32)] * 2,
        scratch_shapes=[pltpu.VMEM((hk, GLA_DV), F32), pltpu.VMEM((hk, GLA_DV), F32)],
        compiler_params=_cparams(("arbitrary", "arbitrary")),
        name="gla_scan",
    )(gq, gk, gv, laf, gq, gk, gv, lab)


def _swa_kernel(q_ref, kc_ref, vc_ref, kp_ref, kcur_ref, kn_ref, vp_ref, vcur_ref, vn_ref, sink_ref,
                o_ref, *, nctx_blocks, nlat_blocks):
    j = pl.program_id(1)
    w = WINDOW
    n = j - nctx_blocks
    is_lat = j >= nctx_blocks
    nkc = kc_ref.shape[0]
    kall = jnp.concatenate([kc_ref[...], kp_ref[...], kcur_ref[...], kn_ref[...]], axis=0)
    vall = jnp.concatenate([vc_ref[...], vp_ref[...], vcur_ref[...], vn_ref[...]], axis=0)
    nk = kall.shape[0]
    lane = lax.broadcasted_iota(jnp.int32, kall.shape, 1)
    lo = lane < SWA_HEAD_DIM
    kroll = pltpu.roll(kall, SWA_HEAD_DIM, axis=1)
    vroll = pltpu.roll(vall, SWA_HEAD_DIM, axis=1)
    k2 = [jnp.where(lo, kall, kroll).astype(BF16), jnp.where(lo, kroll, kall).astype(BF16)]
    v2 = [jnp.where(lo, vall, vroll).astype(BF16), jnp.where(lo, vroll, vall).astype(BF16)]
    rep = SWA_HEADS // SWA_KV_HEADS
    qi = lax.broadcasted_iota(jnp.int32, (rep * w, nk), 0) % w
    kj = lax.broadcasted_iota(jnp.int32, (rep * w, nk), 1)
    qpos = n * w + qi
    kpos = (n - 1) * w + (kj - nkc)
    loc_ok = (jnp.abs(qpos - kpos) <= WINDOW) & (kpos >= 0) & (kpos < nlat_blocks * w) & is_lat
    valid = (kj < nkc) | ((kj >= nkc) & loc_ok)
    qlane = lax.broadcasted_iota(jnp.int32, (w, 128), 1)
    qlo = qlane < SWA_HEAD_DIM
    outs = []
    for g in range(SWA_KV_HEADS):
        qs, sinks = [], []
        for r in range(rep):
            pr, half = divmod(g * rep + r, 2)
            qp = q_ref[:, 128 * pr:128 * (pr + 1)]
            qs.append(jnp.where(qlo if half == 0 else ~qlo, qp, 0.0).astype(BF16))
            sinks.append(jnp.broadcast_to(sink_ref[g * rep + r:g * rep + r + 1, 0:1], (w, 1)))
        sink = jnp.concatenate(sinks, axis=0)
        s = lax.dot_general(jnp.concatenate(qs, axis=0), k2[g], (((1,), (1,)), ((), ())),
                            preferred_element_type=F32)
        s = jnp.where(valid, s, NEG)
        mx = jnp.maximum(jnp.max(s, axis=-1, keepdims=True), sink)
        p = jnp.exp(s - mx)
        den = jnp.sum(p, axis=-1, keepdims=True) + jnp.exp(sink - mx)
        o = jnp.dot(p.astype(BF16), v2[g], preferred_element_type=F32) / den
        outs += [o[r * w:(r + 1) * w, :] for r in range(rep)]
    for pr in range(SWA_HEADS // 2):
        o_ref[:, 128 * pr:128 * (pr + 1)] = jnp.where(qlo, outs[2 * pr], outs[2 * pr + 1])


def _swa(sq, sk, sv, sink, nbatch, nctx_blocks, nlat_blocks):
    m = sq.shape[0]
    w = WINDOW
    nblk = nctx_blocks + nlat_blocks
    nkc = nctx_blocks * w

    def qmap(b, j):
        return (b * nblk + j, 0)

    def nmap(off):
        def f(b, j):
            n = jnp.clip(j - nctx_blocks + off, 0, nlat_blocks - 1)
            return (b * nblk + nctx_blocks + n, 0)
        return f

    kvw = sk.shape[1]
    kcs = pl.BlockSpec((nkc, kvw), lambda b, j: (b * (nblk * w // nkc), 0))
    kvs = lambda off: pl.BlockSpec((w, kvw), nmap(off))
    return pl.pallas_call(
        functools.partial(_swa_kernel, nctx_blocks=nctx_blocks, nlat_blocks=nlat_blocks),
        grid=(nbatch, nblk),
        in_specs=[pl.BlockSpec((w, sq.shape[1]), qmap), kcs, kcs, kvs(-1), kvs(0), kvs(1),
                  kvs(-1), kvs(0), kvs(1), _full(sink.shape)],
        out_specs=pl.BlockSpec((w, sq.shape[1]), qmap),
        out_shape=jax.ShapeDtypeStruct(sq.shape, F32),
        compiler_params=_cparams(("arbitrary", "arbitrary")),
        name="swa_attn",
    )(sq, sk, sv, sk, sk, sk, sv, sv, sv, sink)


def _residual_epilogue(x, mix, mod_ref, n2_ref, xo_ref, f_ref, ft_ref, d):
    xn = x + _mod_rows(mod_ref, 2, d) * mix
    xo_ref[...] = xn
    f = _rms(xn) * n2_ref[...] * (1.0 + _mod_rows(mod_ref, 4, d)) + _mod_rows(mod_ref, 3, d)
    f_ref[...] = f
    ft_ref[...] = f.T.astype(BF16)


def _out0_kernel(x_ref, of_ref, ob_ref, gg_ref, a_ref, mod_ref, gn_ref, wo_ref, n2_ref,
                 xo_ref, f_ref, ft_ref, *, d):
    o = of_ref[...] + ob_ref[...]
    gate = gg_ref[...]
    gate = gate * _sigmoid(gate)
    parts = []
    for h in range(GLA_HEADS):
        oh = o[:, h * GLA_DV:(h + 1) * GLA_DV]
        parts.append(oh * lax.rsqrt(jnp.mean(oh * oh, axis=-1, keepdims=True) + NORM_EPS))
    on = jnp.concatenate(parts, axis=1) * gn_ref[...] * gate
    hv = GLA_HEADS * GLA_DV
    mix = _bdot(on, wo_ref[0:hv, :]) + _bdot(a_ref[...], wo_ref[hv:, :])
    _residual_epilogue(x_ref[...], mix, mod_ref, n2_ref, xo_ref, f_ref, ft_ref, d)


def _tok_spec(n):
    return pl.BlockSpec((TM, n), lambda i: (i, 0))


def _epilogue_specs(m, d):
    out_specs = [_tok_spec(d), _tok_spec(d), pl.BlockSpec((d, TM), lambda i: (0, i))]
    out_shape = [jax.ShapeDtypeStruct((m, d), F32), jax.ShapeDtypeStruct((m, d), F32),
                 jax.ShapeDtypeStruct((d, m), BF16)]
    return out_specs, out_shape


def _out0(xs, of, ob, gg, a, mod, gn, wo, n2, nb, nbatch):
    m, d = xs.shape
    out_specs, out_shape = _epilogue_specs(m, d)
    return pl.pallas_call(
        functools.partial(_out0_kernel, d=d),
        grid=(m // TM,),
        in_specs=[_tok_spec(d), _tok_spec(of.shape[1]), _tok_spec(ob.shape[1]), _tok_spec(gg.shape[1]),
                  _tok_spec(a.shape[1]), _mod_spec(mod, nb, nbatch), _full(gn.shape), _full(wo.shape),
                  _full(n2.shape)],
        out_specs=out_specs, out_shape=out_shape,
        compiler_params=_cparams(("arbitrary",), VMEM_LIMIT),
        name="out0",
    )(xs, of, ob, gg, a, mod, gn, wo, n2)


NO_RANK = 99.0
RANK_STEP = 1e27


def _top_vals(s, k, want_rank=False):
    lw = 128
    ng = s.shape[1] // lw
    rows = lax.broadcasted_iota(jnp.int32, (k, lw), 0)
    cur = [s[:, q * lw:(q + 1) * lw] for q in range(ng)]
    vals = [jnp.zeros((k, lw), F32) for _ in range(ng)]
    for t in range(k):
        for q in range(ng):
            mx = jnp.max(cur[q], axis=0, keepdims=True)
            vals[q] = jnp.where(rows == t, mx, vals[q])
            cur[q] = jnp.where(cur[q] == mx, NEG - t * RANK_STEP, cur[q])
    vals = jnp.concatenate(vals, axis=1)
    if not want_rank:
        return vals
    cur = jnp.concatenate(cur, axis=1)
    rank = jnp.where(cur < 0.5 * NEG, jnp.floor((NEG - cur) * (1.0 / RANK_STEP) + 0.5), NO_RANK)
    return vals, rank


def _peer_scores_kernel(f_ref, wh_ref, wl_ref, kh_ref, kl_ref, s1_ref, s2_ref):
    f = f_ref[...]
    fh = f.astype(BF16)
    fl = (f - fh.astype(F32)).astype(BF16)
    wh = wh_ref[...]
    q = (jnp.dot(fh, wh, preferred_element_type=F32) + jnp.dot(fl, wh, preferred_element_type=F32)
         + jnp.dot(fh, wl_ref[...], preferred_element_type=F32))
    qh = q.astype(BF16)
    ql = (q - qh.astype(F32)).astype(BF16)
    nt = (((1,), (1,)), ((), ()))
    for h in range(PEER_HEADS):
        for z, s_ref in enumerate((s1_ref, s2_ref)):
            c0 = (2 * h + z) * PEER_NKEYS
            kh = kh_ref[z, h]
            qhs = qh[:, c0:c0 + PEER_NKEYS]
            s_ref[h] = (lax.dot_general(kh, qhs, nt, preferred_element_type=F32)
                        + lax.dot_general(kh, ql[:, c0:c0 + PEER_NKEYS], nt, preferred_element_type=F32)
                        + lax.dot_general(kl_ref[z, h], qhs, nt, preferred_element_type=F32))


def _peer_q_kernel(s1_ref, s2_ref, r2_ref, cnt_ref, e1_ref, e2_ref):
    kk = PEER_TOPK
    for h in range(PEER_HEADS):
        s1 = s1_ref[h]
        s2 = s2_ref[h]
        v1 = _top_vals(s1, kk)
        v2, rank2 = _top_vals(s2, kk, want_rank=True)
        row8 = lax.broadcasted_iota(jnp.int32, (8, s1.shape[1]), 0)
        blocks = [v1[0:1, :] + v2]
        for r in range(1, 8):
            blocks.append(jnp.where(row8 < kk // (r + 1), v1[r:r + 1, :] + v2[0:8, :], NEG))
        blocks.append(v1[8:kk, :] + v2[0:1, :])
        top = _top_vals(jnp.concatenate(blocks, axis=0), kk)
        tau = top[kk - 1:kk, :]
        z = jnp.sum(jnp.exp(top - top[0:1, :]), axis=0, keepdims=True)
        for q in range(s1.shape[1] // 128):
            ln = slice(128 * q, 128 * (q + 1))
            v2b = [jnp.broadcast_to(v2[c:c + 1, ln], (8, 128)) for c in range(kk)]
            taub = jnp.broadcast_to(tau[:, ln], (8, 128))
            for rb in range(PEER_NKEYS // 8):
                x = s1[8 * rb:8 * (rb + 1), ln]
                cnt = jnp.zeros((8, 128), F32)
                for c in range(kk):
                    cnt = cnt + jnp.where(x + v2b[c] >= taub, 1.0, 0.0)
                cnt_ref[h, 8 * rb:8 * (rb + 1), ln] = cnt
        r2_ref[h] = rank2.astype(BF16)
        e1_ref[h] = jnp.exp(s1 - v1[0:1, :])
        e2_ref[h] = (jnp.exp(s2 - v2[0:1, :]) / z).astype(BF16)


def _peer_q(f, wh, wl, k1, k2):
    m, d = f.shape
    hh, nk = PEER_HEADS, PEER_NKEYS
    big = pl.BlockSpec((hh, nk, TM), lambda i: (0, 0, i))
    sh32 = jax.ShapeDtypeStruct((hh, nk, m), F32)
    sh16 = jax.ShapeDtypeStruct((hh, nk, m), BF16)
    kf = jnp.stack([k1, k2])
    kh = kf.astype(BF16)
    kl = (kf - kh.astype(F32)).astype(BF16)
    s1, s2 = pl.pallas_call(
        _peer_scores_kernel,
        grid=(m // TM,),
        in_specs=[_tok_spec(d), _full(wh.shape), _full(wl.shape), _full(kh.shape), _full(kl.shape)],
        out_specs=[big, big],
        out_shape=[sh32, sh32],
        compiler_params=_cparams(("arbitrary",), VMEM_LIMIT),
        name="peer_scores",
    )(f, wh, wl, kh, kl)
    return pl.pallas_call(
        _peer_q_kernel,
        grid=(m // TM,),
        in_specs=[big, big],
        out_specs=[big, big, big, big],
        out_shape=[sh16, sh32, sh32, sh16],
        compiler_params=_cparams(("arbitrary",), VMEM_LIMIT),
        name="peer_query",
    )(s1, s2)


PD_TM = 512
PD_TE = 2048
PD_SUBS = (1024, 1024)


def _gelu(x):
    return 0.5 * x * (1.0 + lax.erf(x * (2.0 ** -0.5)))


def _row_bf16(tile, r, rows):
    one = jnp.broadcast_to(tile[r:r + 1, :], (8, tile.shape[1]))
    one = jnp.concatenate([one, one], axis=0).astype(BF16)
    return jnp.concatenate([one] * (rows // 16), axis=0)


def _run_if(cond, fn):
    def body(_, carry):
        fn()
        return carry
    lax.fori_loop(0, cond.astype(jnp.int32), body, 0)


def _peer_dense_kernel(ft_ref, u_ref, vt_ref, r2_ref, cnt_ref, e1_ref, e2_ref, o_ref, acc_ref):
    k = pl.program_id(1)
    nk = PEER_NKEYS
    nslab = PD_TE // nk
    zero = jnp.zeros((nk, PD_TM), BF16)
    nsub = len(PD_SUBS)
    starts = [sum(PD_SUBS[:i]) for i in range(nsub + 1)]
    assert starts[-1] == PD_TE

    def first():
        acc_ref[...] = jnp.zeros_like(acc_ref)

    _run_if(k == 0, first)

    def hidden(sb):
        return jnp.dot(u_ref[starts[sb]:starts[sb + 1], :], ft_ref[...], preferred_element_type=F32)

    ht_next = hidden(0)
    for sb in range(nsub):
        e0, e1x = starts[sb], starts[sb + 1]
        ht = ht_next
        if sb + 1 < nsub:
            ht_next = hidden(sb + 1)
        gates = []
        for al in range(e0 // nk, e1x // nk):
            wg = zero
            a0 = pl.multiple_of(k * nslab + 8 * (al // 8), 8)
            for h in range(PEER_HEADS):
                cnt = _row_bf16(cnt_ref[h, pl.ds(a0, 8), :], al % 8, nk)
                e1 = _row_bf16(e1_ref[h, pl.ds(a0, 8), :], al % 8, nk)
                wg = wg + jnp.where(r2_ref[h] < cnt, e2_ref[h], zero) * e1
            gates.append(wg)
        ct = _gelu(ht).astype(BF16) * jnp.concatenate(gates, axis=0)
        acc_ref[...] += jnp.dot(vt_ref[:, e0:e1x], ct, preferred_element_type=F32)

    def last():
        o_ref[...] = acc_ref[...].T

    _run_if(k == pl.num_programs(1) - 1, last)


def _peer_dense(ft, u, vt, layer, r2, cnt, e1, e2):
    d, m = ft.shape
    ne = u.shape[1]
    hh, nk = PEER_HEADS, PEER_NKEYS
    big = pl.BlockSpec((hh, nk, PD_TM), lambda i, k: (0, 0, i))
    return pl.pallas_call(
        _peer_dense_kernel,
        grid=(m // PD_TM, ne // PD_TE),
        in_specs=[pl.BlockSpec((d, PD_TM), lambda i, k: (0, i)),
                  pl.BlockSpec((None, PD_TE, d), lambda i, k: (layer, k, 0)),
                  pl.BlockSpec((None, None, d, PD_TE), lambda i, k: (layer, k, 0, 0)),
                  big, big, big, big],
        out_specs=pl.BlockSpec((PD_TM, d), lambda i, k: (i, 0)),
        out_shape=jax.ShapeDtypeStruct((m, d), F32),
        scratch_shapes=[pltpu.VMEM((d, PD_TM), F32)],
        compiler_params=_cparams(("arbitrary", "arbitrary"), VMEM_LIMIT),
        name="peer_dense",
    )(ft, u, vt, r2, cnt, e1, e2)


def _res_kernel(x_ref, p_ref, mod_ref, modn_ref, g_ref, xo_ref, u_ref, *, d):
    xn = x_ref[...] + _mod_rows(mod_ref, 5, d) * p_ref[...]
    xo_ref[...] = xn
    u_ref[...] = _rms(xn) * g_ref[...] * (1.0 + _mod_rows(modn_ref, 1, d)) + _mod_rows(modn_ref, 0, d)


def _res(xs, p, mod, modn, g, nb, nbatch):
    m, d = xs.shape
    return pl.pallas_call(
        functools.partial(_res_kernel, d=d),
        grid=(m // TM,),
        in_specs=[_tok_spec(d), _tok_spec(d), _mod_spec(mod, nb, nbatch), _mod_spec(modn, nb, nbatch),
                  _full(g.shape)],
        out_specs=[_tok_spec(d), _tok_spec(d)],
        out_shape=[jax.ShapeDtypeStruct((m, d), F32)] * 2,
        compiler_params=_cparams(("arbitrary",)),
        name="peer_residual",
    )(xs, p, mod, modn, g)


def _final_kernel(x_ref, p_ref, mod_ref, g_ref, y_ref, *, d):
    y_ref[...] = _rms(x_ref[...] + _mod_rows(mod_ref, 5, d) * p_ref[...]) * g_ref[...]


def _final(xs, p, mod, g, nb):
    m, d = xs.shape
    return pl.pallas_call(
        functools.partial(_final_kernel, d=d),
        grid=(m // TM,),
        in_specs=[_tok_spec(d), _tok_spec(d), _mod_spec_lat(mod, nb - 1), _full(g.shape)],
        out_specs=_tok_spec(d),
        out_shape=jax.ShapeDtypeStruct((m, d), F32),
        compiler_params=_cparams(("arbitrary",)),
        name="final_norm",
    )(xs, p, mod, g)


def _split_dot(x, w):
    hi = x.astype(BF16)
    lo = (x - hi.astype(F32)).astype(BF16)
    return jnp.dot(hi, w, preferred_element_type=F32) + jnp.dot(lo, w, preferred_element_type=F32)


def _seg_sum(x, g_ref, gt_ref):
    return _split_dot(_split_dot(x, g_ref[...]), gt_ref[...])


def _rwkv_prep_kernel(u_ref, up_ref, un_ref, mu_ref, wr_ref, wk_ref, wv_ref, w0_ref, w1_ref, w2_ref,
                      a0_ref, a1_ref, a2_ref, g1_ref, g2_ref, kk_ref_, ka_ref, rk_ref, sg_ref, sgt_ref,
                      wf_o, wb_o, kf_o, kb_o, bf_o, bb_o, r_o, v_o, kk_o, g_o, bonus_o, *, nb, d):
    i = pl.program_id(0)
    p = i % nb
    u = u_ref[...]
    has_prev = (p >= 2).astype(F32)
    has_next = ((p >= 1) & (p <= nb - 2)).astype(F32)
    prev_row = up_ref[7:8, :] * has_prev
    next_row = un_ref[0:1, :] * has_next
    rows = lax.broadcasted_iota(jnp.int32, u.shape, 0)
    up = jnp.where(rows == 0, prev_row, pltpu.roll(u, 1, axis=0))
    un = jnp.where(rows == TM - 1, next_row, pltpu.roll(u, TM - 1, axis=0))
    xx = 0.5 * (up + un) - u

    def mix(zi):
        return (u + xx * mu_ref[zi:zi + 1, :]).astype(BF16)

    r = jnp.dot(mix(0), wr_ref[...], preferred_element_type=F32)
    k = jnp.dot(mix(2), wk_ref[...], preferred_element_type=F32)
    v = jnp.dot(mix(3), wv_ref[...], preferred_element_type=F32)
    lw = w0_ref[...] + _bdot(jnp.tanh(jnp.dot(mix(1), w1_ref[...], preferred_element_type=F32)), w2_ref[...])
    decay = jnp.exp(-_sigmoid(lw) * float(np.exp(-0.5)))
    a = _sigmoid(a0_ref[...] + _bdot(jnp.dot(mix(4), a1_ref[...], preferred_element_type=F32), a2_ref[...]))
    g = _bdot(_sigmoid(jnp.dot(mix(5), g1_ref[...], preferred_element_type=F32)), g2_ref[...])
    kk = k * kk_ref_[...]
    kk = kk * lax.rsqrt(jnp.maximum(_seg_sum(kk * kk, sg_ref, sgt_ref), 1e-24))
    ka = ka_ref[...]
    kf = k * (1.0 + (a[:, :d] - 1.0) * ka)
    kb = k * (1.0 + (a[:, d:] - 1.0) * ka)
    bonus = _seg_sum(r * (kf + kb) * rk_ref[...], sg_ref, sgt_ref) * v
    wf_o[...] = decay[:, :d]
    wb_o[...] = decay[:, d:]
    kf_o[...] = kf
    kb_o[...] = kb
    bf_o[...] = kk * a[:, :d]
    bb_o[...] = kk * a[:, d:]
    r_o[...] = r
    v_o[...] = v
    kk_o[...] = kk
    g_o[...] = g
    bonus_o[...] = bonus


def _rwkv_prep(u, consts, nb):
    m, d = u.shape
    r8 = TM // 8
    nb8 = m // 8
    in_specs = [_tok_spec(d),
                pl.BlockSpec((8, d), lambda i: (jnp.maximum(i * r8 - 1, 0), 0)),
                pl.BlockSpec((8, d), lambda i: (jnp.minimum((i + 1) * r8, nb8 - 1), 0))]
    in_specs += [_full(c.shape) for c in consts]
    return pl.pallas_call(
        functools.partial(_rwkv_prep_kernel, nb=nb, d=d),
        grid=(m // TM,),
        in_specs=in_specs,
        out_specs=[_tok_spec(d)] * 11,
        out_shape=[jax.ShapeDtypeStruct((m, d), F32)] * 11,
        compiler_params=_cparams(("arbitrary",), VMEM_LIMIT),
        name="rwkv_prep",
    )(u, u, u, *consts)


RW_TC = 256
RW_UNROLL = 8


def _rwkv_scan_kernel(wf, kf, bf, rf, vf, kkf, wb, kb, bb, rb, vb, kkb, yf_ref, yb_ref,
                      st, vk_a, vk_b, ycol_scr, sa_scr, *, npair):
    s = pl.program_id(1)
    n = RW_HEAD_DIM
    grp = 8

    @pl.when(s == 0)
    def _():
        st[...] = jnp.zeros_like(st)
        ycol_scr[...] = jnp.zeros_like(ycol_scr)

    w2 = 2 * n
    ri = lax.broadcasted_iota(jnp.int32, (n, w2), 0)
    li = lax.broadcasted_iota(jnp.int32, (n, w2), 1)
    eye_a = (li == ri).astype(BF16)
    bi = lax.broadcasted_iota(jnp.int32, (w2, w2), 0)
    bj = lax.broadcasted_iota(jnp.int32, (w2, w2), 1)
    ones_blk = ((bi // n) == (bj // n)).astype(BF16)
    b4i = lax.broadcasted_iota(jnp.int32, (2 * w2, 2 * w2), 0)
    b4j = lax.broadcasted_iota(jnp.int32, (2 * w2, 2 * w2), 1)
    ones_blk2 = ((b4i // n) == (b4j // n)).astype(BF16)
    lane8 = lax.broadcasted_iota(jnp.int32, (grp, w2), 1)
    rows16 = lax.broadcasted_iota(jnp.int32, (2 * grp, w2), 0)
    lane16 = lax.broadcasted_iota(jnp.int32, (2 * grp, w2), 1)
    untr = ((lane16 == rows16) | (lane16 == rows16 + n)).astype(BF16)
    dirs = ((wf, kf, bf, rf, vf, kkf, yf_ref), (wb, kb, bb, rb, vb, kkb, yb_ref))
    nch = 2 * npair
    ngrp = RW_TC // grp
    nt = (((1,), (1,)), ((), ()))

    def halves(tile):
        return jnp.concatenate([jnp.where(lane8 < n, tile, 0.0), jnp.where(lane8 >= n, tile, 0.0)], axis=0)

    assert nch == 2 * grp

    def group_base(g, dd):
        base = (g if dd == 0 else ngrp - 1 - g) * grp
        return base if isinstance(base, int) else pl.multiple_of(base, grp)

    def lanes(c):
        p = c % npair
        return slice(w2 * p, w2 * (p + 1))

    def precompute(g, chains, vk_dst):
        gq = jnp.minimum(g, ngrp - 1)
        v2, ks = [], []
        for c in chains:
            dd = c // npair
            base = group_base(gq, dd)
            v_t = dirs[dd][4][pl.ds(base, grp), lanes(c)]
            ks.append(dirs[dd][1][pl.ds(base, grp), lanes(c)])
            v2 += [jnp.where(lane8 < n, v_t, 0.0), jnp.where(lane8 < n, pltpu.roll(v_t, n, axis=1), 0.0)]
        vt = lax.dot_general(eye_a, jnp.concatenate(v2, axis=0).astype(BF16), nt,
                             preferred_element_type=F32).astype(BF16)
        for idx, c in enumerate(chains):
            k2 = halves(ks[idx])
            kblk = jnp.concatenate(
                [jnp.where((rows16 == j) | (rows16 == grp + j), k2, 0.0) for j in range(grp)], axis=1)
            vk_dst[c] = jnp.dot(vt[:, 2 * grp * idx:2 * grp * (idx + 1)], kblk.astype(BF16),
                                preferred_element_type=F32)

    def untranspose(g, chains, ycol_prev):
        for c in chains:
            dd = c // npair
            yc = ycol_prev[c]
            ycs = jnp.concatenate([jnp.where(li < n, yc, 0.0), jnp.where(li >= n, yc, 0.0)], axis=0)
            yt = lax.dot_general(untr, ycs.astype(BF16), nt, preferred_element_type=F32)
            dirs[dd][6][pl.ds(group_base(g, dd), grp), lanes(c)] = yt[0:grp, :]

    def run_group(g, vk_cur, vk_next, states, sa, ycol_prev):
        tiles = []
        for c in range(nch):
            dd = c // npair
            base = group_base(g, dd)
            tiles.append([dirs[dd][q][pl.ds(base, grp), lanes(c)] for q in (0, 2, 3, 5)])
        gprev = jnp.maximum(g - 1, 0)
        gnext = jnp.minimum(g + 1, ngrp - 1)
        kap_next = []
        for c in range(nch):
            dd = c // npair
            kap_next.append(dirs[dd][5][pl.ds(group_base(gnext, dd), grp), lanes(c)])
        states, sa = list(states), list(sa)
        ycol = [jnp.zeros((n, w2), F32) for _ in range(nch)]
        for j in range(grp):
            lhs = []
            for c in range(nch):
                w_t, b_t, r_t, kk_t = tiles[c]
                fwd = c < npair
                jj = j if fwd else grp - 1 - j
                row = lambda tl, q: jnp.broadcast_to(tl[q:q + 1, :], (n, w2))
                states[c] = (states[c] * row(w_t, jj) - sa[c] * row(b_t, jj)
                             + vk_cur[c, :, jj * w2:(jj + 1) * w2])
                if j + 1 < grp:
                    kap = row(kk_t, jj + 1 if fwd else jj - 1)
                else:
                    kap = row(kap_next[c], 0 if fwd else grp - 1)
                lhs.append(jnp.concatenate([(states[c] * kap).astype(BF16),
                                            (states[c] * row(r_t, jj)).astype(BF16)], axis=1))
            red = jnp.dot(jnp.concatenate(lhs, axis=0), ones_blk2, preferred_element_type=F32)
            for c in range(nch):
                jj = j if c < npair else grp - 1 - j
                sa[c] = red[c * n:(c + 1) * n, :w2]
                ycol[c] = jnp.where((li % n) == jj, red[c * n:(c + 1) * n, w2:], ycol[c])
            pair = (2 * j, 2 * j + 1)
            precompute(g + 1, pair, vk_next)
            untranspose(gprev, pair, ycol_prev)
        return states, sa, ycol

    precompute(0, tuple(range(nch)), vk_a)
    first = []
    for c in range(nch):
        dd = c // npair
        kk_t = dirs[dd][5][pl.ds(group_base(0, dd), grp), lanes(c)]
        q0 = 0 if dd == 0 else grp - 1
        first.append((st[c] * jnp.broadcast_to(kk_t[q0:q0 + 1, :], (n, w2))).astype(BF16))
    sa0 = jnp.dot(jnp.concatenate(first, axis=0), ones_blk, preferred_element_type=F32)
    for c in range(nch):
        sa_scr[c] = sa0[c * n:(c + 1) * n, :]

    def two_groups(i, carry):
        states = [st[c] for c in range(nch)]
        sa = [sa_scr[c] for c in range(nch)]
        ycol = [ycol_scr[c] for c in range(nch)]
        for u in range(0, RW_UNROLL, 2):
            states, sa, ycol = run_group(RW_UNROLL * i + u, vk_a, vk_b, states, sa, ycol)
            states, sa, ycol = run_group(RW_UNROLL * i + u + 1, vk_b, vk_a, states, sa, ycol)
        for c in range(nch):
            st[c] = states[c]
            sa_scr[c] = sa[c]
            ycol_scr[c] = ycol[c]
        return carry

    lax.fori_loop(0, ngrp // RW_UNROLL, two_groups, 0)
    untranspose(ngrp - 1, tuple(range(nch)), [ycol_scr[c] for c in range(nch)])


def _rwkv_scan(wf, kf, bf, wb, kb, bb, r, v, kk, nbatch):
    m, d = r.shape
    nch = m // nbatch // RW_TC
    npair = d // (2 * RW_HEAD_DIM)

    def fi(b, s):
        return (b * nch + s, 0)

    def bi(b, s):
        return (b * nch + jnp.where(s == 0, 0, nch - s), 0)

    sf = pl.BlockSpec((RW_TC, d), fi)
    sb = pl.BlockSpec((RW_TC, d), bi)
    return pl.pallas_call(
        functools.partial(_rwkv_scan_kernel, npair=npair),
        grid=(nbatch, nch),
        in_specs=[sf] * 6 + [sb] * 6,
        out_specs=[sf, sb],
        out_shape=[jax.ShapeDtypeStruct((m, d), F32)] * 2,
        scratch_shapes=[pltpu.VMEM((2 * npair, RW_HEAD_DIM, 2 * RW_HEAD_DIM), F32),
                        pltpu.VMEM((2 * npair, RW_HEAD_DIM, 8 * 2 * RW_HEAD_DIM), F32),
                        pltpu.VMEM((2 * npair, RW_HEAD_DIM, 8 * 2 * RW_HEAD_DIM), F32),
                        pltpu.VMEM((2 * npair, RW_HEAD_DIM, 2 * RW_HEAD_DIM), F32),
                        pltpu.VMEM((2 * npair, RW_HEAD_DIM, 2 * RW_HEAD_DIM), F32)],
        compiler_params=_cparams(("arbitrary", "arbitrary"), VMEM_LIMIT),
        name="rwkv_scan",
    )(wf, kf, bf, r, v, kk, wb, kb, bb, r, v, kk)


def _rwkv_out_kernel(x_ref, yf_ref, yb_ref, bonus_ref, g_ref, mod_ref, lg_ref, lb_ref, wo_ref, n2_ref,
                     sg_ref, sgt_ref, xo_ref, f_ref, ft_ref, *, d):
    y = yf_ref[...] + yb_ref[...]
    inv = 1.0 / RW_HEAD_DIM
    mean = _seg_sum(y, sg_ref, sgt_ref) * inv
    yc = y - mean
    var = _seg_sum(yc * yc, sg_ref, sgt_ref) * inv
    yn = yc * lax.rsqrt(var + RW_GN_EPS) * lg_ref[...] + lb_ref[...]
    out = (yn + bonus_ref[...]) * g_ref[...]
    mix = _bdot(out, wo_ref[...])
    _residual_epilogue(x_ref[...], mix, mod_ref, n2_ref, xo_ref, f_ref, ft_ref, d)


def _rwkv_out(xs, yf, yb, bonus, g, mod, lg, lb, wo, n2, sg, sgt, nb, nbatch):
    d = xs.shape[1]
    m = nbatch * (nb - 1) * TM
    out_specs, out_shape = _epilogue_specs(m, d)
    return pl.pallas_call(
        functools.partial(_rwkv_out_kernel, d=d),
        grid=(m // TM,),
        in_specs=[_lat_tok_spec(d, nb)] * 5 + [_mod_spec_lat(mod, nb - 1)]
        + [_full(a.shape) for a in (lg, lb, wo, n2, sg, sgt)],
        out_specs=out_specs, out_shape=out_shape,
        compiler_params=_cparams(("arbitrary",), VMEM_LIMIT),
        name="rwkv_out",
    )(xs, yf, yb, bonus, g, mod, lg, lb, wo, n2, sg, sgt)


def _rope_tables(t_lat, n_ctx):
    rows = t_lat // GRID_W
    row = jnp.broadcast_to(jnp.arange(rows, dtype=F32)[:, None], (rows, GRID_W)).reshape(-1)
    col = jnp.broadcast_to(jnp.arange(GRID_W, dtype=F32)[None, :], (rows, GRID_W)).reshape(-1)
    n_freq = SWA_HEAD_DIM // 4
    inv_freq = ROPE_BASE ** (-jnp.arange(n_freq, dtype=F32) / n_freq)
    ar = row[:, None] * inv_freq
    ac = col[:, None] * inv_freq
    cos = jnp.concatenate([jnp.cos(ar), jnp.cos(ar), jnp.cos(ac), jnp.cos(ac)], axis=1)
    sin = jnp.concatenate([-jnp.sin(ar), jnp.sin(ar), -jnp.sin(ac), jnp.sin(ac)], axis=1)
    cos = jnp.concatenate([jnp.ones((n_ctx, SWA_HEAD_DIM), F32), cos], axis=0)
    sin = jnp.concatenate([jnp.zeros((n_ctx, SWA_HEAD_DIM), F32), sin], axis=0)
    return jnp.tile(cos, (1, 2)), jnp.tile(sin, (1, 2))


def _peer(f, ft, w_q, k1, k2, u_all, vt_all, layer):
    wh = w_q.astype(BF16)
    wl = (w_q - wh.astype(F32)).astype(BF16)
    r2, cnt, e1, e2 = _peer_q(f, wh, wl, k1, k2)
    return _peer_dense(ft, u_all, vt_all, layer, r2, cnt, e1, e2)


def kernel(x, c, ctx, c_ctx, ada_w, ada_b, norm1_g, norm2_g, ab_w_in, gla_dec_w2, gla_dec_b, gla_norm_g,
           swa_sink, ab_w_out, rw_mu, rw_w_rkv, rw_w_o, rw_w0, rw_w1, rw_w2, rw_a0, rw_a1, rw_a2, rw_g1,
           rw_g2, rw_k_k, rw_k_a, rw_r_k, rw_ln_g, rw_ln_b, peer_w_q, peer_k1, peer_k2, peer_u, peer_v,
           final_g):
    nbatch, t_lat, d = x.shape
    n_ctx = ctx.shape[1]
    assert n_ctx == TM and t_lat % TM == 0 and d == 1024
    s_tok = n_ctx + t_lat
    nb = s_tok // TM
    m = nbatch * s_tok
    xs = jnp.concatenate([ctx, x], axis=1).reshape(m, d)

    cc = jnp.zeros((16, d), F32).at[:nbatch].set(c).at[nbatch].set(c_ctx)
    mod0 = _ada_table(cc, ada_w[0], ada_b[0]).reshape(16, 1, 6 * d)
    mod1 = _ada_table(cc, ada_w[1], ada_b[1]).reshape(16, 1, 6 * d)
    row2 = lambda a: a.reshape(1, -1)

    w_in = ab_w_in[0]
    cuts = np.cumsum((256, 256, 512, 512, 32, 512, 128, 128))[:-1].tolist()
    wgq, wgk, wgv, wgg, wlr, wsq, wsk, wsv = jnp.split(w_in, cuts, axis=1)
    wp = jnp.concatenate([wgq, wgk, wgv, wgg, wsq, wsk, wsv, wlr,
                          jnp.zeros((d, _P_END - _P_LR - 32), F32)], axis=1).astype(BF16)
    hk = GLA_HEADS * GLA_DK
    w2p = jnp.zeros((128, 2 * hk), F32)
    w2p = w2p.at[0:GLA_LOWRANK, 0:hk].set(gla_dec_w2[0, 0]).at[GLA_LOWRANK:2 * GLA_LOWRANK, hk:].set(gla_dec_w2[0, 1])
    db = gla_dec_b[0].reshape(1, 2 * hk)
    cos, sin = _rope_tables(t_lat, n_ctx)
    gq, gk, gv, gg, laf, lab, sq, sk, sv = _proj0(xs, mod0, row2(norm1_g[0]), wp, w2p, db, cos, sin, nb, nbatch)
    of, ob = _gla(gq, gk, gv, laf, lab, nbatch, n_ctx // GLA_CHUNK)
    sink = jnp.broadcast_to(swa_sink[0][:, None], (SWA_HEADS, 128))
    att = _swa(sq, sk, sv, sink, nbatch, n_ctx // WINDOW, t_lat // WINDOW)
    xs, f, ft = _out0(xs, of, ob, gg, att, mod0, row2(gla_norm_g[0]), ab_w_out[0].astype(BF16),
                      row2(norm2_g[0]), nb, nbatch)
    u_all = peer_u.astype(BF16)
    vt_all = jnp.swapaxes(peer_v.reshape(peer_v.shape[0], -1, PD_TE, d), 2, 3).astype(BF16)
    p = _peer(f, ft, peer_w_q[0], peer_k1[0], peer_k2[0], u_all, vt_all, 0)
    xs, u = _res(xs, p, mod0, mod1, row2(norm1_g[1]), nb, nbatch)

    lora = rw_w1.shape[-1]
    w1c = jnp.concatenate([rw_w1[0, 0], rw_w1[0, 1]], axis=1).astype(BF16)
    w2c = jnp.zeros((2 * lora, 2 * d), F32).at[:lora, :d].set(rw_w2[0, 0]).at[lora:, d:].set(rw_w2[0, 1]).astype(BF16)
    la = rw_a1.shape[-1]
    a1c = jnp.concatenate([rw_a1[0, 0], rw_a1[0, 1]], axis=1).astype(BF16)
    a2c = jnp.zeros((2 * la, 2 * d), F32).at[:la, :d].set(rw_a2[0, 0]).at[la:, d:].set(rw_a2[0, 1]).astype(BF16)
    lg = rw_g1.shape[-1]
    g1p = jnp.zeros((d, 256), F32).at[:, :lg].set(rw_g1[0]).astype(BF16)
    g2p = jnp.zeros((256, d), F32).at[:lg].set(rw_g2[0]).astype(BF16)
    sg = (jnp.arange(d)[:, None] // RW_HEAD_DIM == jnp.arange(128)[None, :]).astype(BF16)
    sgt = sg.T
    consts = [rw_mu[0], rw_w_rkv[0, 0].astype(BF16), rw_w_rkv[0, 1].astype(BF16), rw_w_rkv[0, 2].astype(BF16),
              rw_w0[0].reshape(1, 2 * d), w1c, w2c, rw_a0[0].reshape(1, 2 * d), a1c, a2c, g1p, g2p,
              row2(rw_k_k[0]), row2(rw_k_a[0]), rw_r_k[0].reshape(1, d), sg, sgt]
    mu8 = jnp.zeros((8, d), F32).at[:6].set(rw_mu[0])
    consts[0] = mu8
    wf, wb, kf, kb, bf, bb, r, v, kk, g, bonus = _rwkv_prep(u, consts, nb)
    yf, yb = _rwkv_scan(wf, kf, bf, wb, kb, bb, r, v, kk, nbatch)
    xs, f, ft = _rwkv_out(xs, yf, yb, bonus, g, mod1, row2(rw_ln_g[0]), row2(rw_ln_b[0]),
                          rw_w_o[0].astype(BF16), row2(norm2_g[1]), sg, sgt, nb, nbatch)
    p = _peer(f, ft, peer_w_q[1], peer_k1[1], peer_k2[1], u_all, vt_all, 1)
    y = _final(xs, p, mod1, row2(final_g), nb)
    return y.reshape(nbatch, t_lat, d)
```

```python
import functools

import numpy as np
import jax
import jax.numpy as jnp
from jax import lax
from jax.experimental import pallas as pl
from jax.experimental.pallas import tpu as pltpu

F32 = jnp.float32
BF16 = jnp.bfloat16
HI = lax.Precision.HIGHEST

NORM_EPS = 1e-6
GLA_HEADS, GLA_DK, GLA_DV, GLA_LOWRANK, GLA_TAU, GLA_CHUNK = 4, 64, 128, 16, 16.0, 64
SWA_HEADS, SWA_KV_HEADS, SWA_HEAD_DIM, WINDOW = 8, 2, 64, 128
ROPE_BASE = 10000.0
GRID_W = 64
RW_HEAD_DIM = 64
RW_GN_EPS = 64e-5
PEER_HEADS, PEER_NKEYS, PEER_TOPK = 8, 128, 16
NEG = -1e30

TM = 256
VMEM_LIMIT = 56 * 1024 * 1024


def _cparams(sem, vmem=None):
    return pltpu.CompilerParams(dimension_semantics=sem, vmem_limit_bytes=vmem)


def _bdot(a, b):
    return jnp.dot(a.astype(BF16), b.astype(BF16), preferred_element_type=F32)


def _hdot(a, b):
    return jnp.dot(a, b, precision=HI, preferred_element_type=F32)


def _sigmoid(x):
    return 1.0 / (1.0 + jnp.exp(-x))


def _rms(x):
    return x * lax.rsqrt(jnp.mean(x * x, axis=-1, keepdims=True) + NORM_EPS)


def _mod_rows(mod_ref, k, d):
    return mod_ref[0, :, k * d:(k + 1) * d]


def _mod_spec(mod, nb, nbatch):
    return pl.BlockSpec((1, 1, mod.shape[2]), lambda i: (jnp.where(i % nb == 0, nbatch, i // nb), 0, 0))


def _mod_spec_lat(mod, nbl):
    return pl.BlockSpec((1, 1, mod.shape[2]), lambda i: (i // nbl, 0, 0))


def _lat_tok_spec(n, nb):
    return pl.BlockSpec((TM, n), lambda i: ((i // (nb - 1)) * nb + 1 + i % (nb - 1), 0))


def _full(shape):
    n = len(shape)
    return pl.BlockSpec(shape, lambda *_: (0,) * n)


def _ada_kernel(c_ref, w_ref, b_ref, o_ref):
    c = c_ref[...]
    s = c * _sigmoid(c)
    o_ref[...] = _hdot(s, w_ref[...]) + b_ref[...]


def _ada_table(cc, w, b):
    rows, d = cc.shape
    n = w.shape[1]
    tn = 512
    return pl.pallas_call(
        _ada_kernel,
        grid=(n // tn,),
        in_specs=[_full((rows, d)), pl.BlockSpec((d, tn), lambda j: (0, j)),
                  pl.BlockSpec((1, tn), lambda j: (0, j))],
        out_specs=pl.BlockSpec((rows, tn), lambda j: (0, j)),
        out_shape=jax.ShapeDtypeStruct((rows, n), F32),
        compiler_params=_cparams(("arbitrary",)),
        name="ada_table",
    )(cc, w, b.reshape(1, n))


_P_GQ, _P_GK, _P_GV, _P_GG, _P_SQ, _P_SK, _P_SV, _P_LR, _P_END = 0, 256, 512, 1024, 1536, 2048, 2176, 2304, 2432


def _rope(x, cos, sin):
    lane = lax.broadcasted_iota(jnp.int32, x.shape, 1)
    up = pltpu.roll(x, 112, axis=1)
    dn = pltpu.roll(x, 16, axis=1)
    sw = jnp.where((lane % 32) < 16, up, dn)
    return x * cos + sw * sin


def _proj0_kernel(x_ref, mod_ref, g_ref, w_ref, w2_ref, db_ref, cos_ref, sin_ref,
                  gq_ref, gk_ref, gv_ref, gg_ref, laf_ref, lab_ref, sq_ref, sk_ref, sv_ref,
                  *, d):
    u = (_rms(x_ref[...]) * g_ref[...] * (1.0 + _mod_rows(mod_ref, 1, d))
         + _mod_rows(mod_ref, 0, d)).astype(BF16)

    def seg(a, b):
        return jnp.dot(u, w_ref[:, a:b], preferred_element_type=F32)

    gq_ref[...] = seg(_P_GQ, _P_GK) * (GLA_DK ** -0.5)
    gk_ref[...] = seg(_P_GK, _P_GV)
    gv_ref[...] = seg(_P_GV, _P_GG)
    gg_ref[...] = seg(_P_GG, _P_SQ)
    lr = seg(_P_LR, _P_END)
    z = _hdot(lr, w2_ref[...]) + db_ref[...]
    la = (jnp.minimum(z, 0.0) - jnp.log(1.0 + jnp.exp(-jnp.abs(z)))) * (1.0 / GLA_TAU)
    hk = GLA_HEADS * GLA_DK
    laf_ref[...] = la[:, :hk]
    lab_ref[...] = la[:, hk:]
    cos = cos_ref[...]
    sin = sin_ref[...]
    sq = seg(_P_SQ, _P_SK) * (SWA_HEAD_DIM ** -0.5)
    for c in range(4):
        sq_ref[:, 128 * c:128 * (c + 1)] = _rope(sq[:, 128 * c:128 * (c + 1)], cos, sin)
    sk_ref[...] = _rope(seg(_P_SK, _P_SV), cos, sin)
    sv_ref[...] = seg(_P_SV, _P_LR)


def _proj0(xs, mod, g, w, w2, db, cos, sin, nb, nbatch):
    m, d = xs.shape
    widths = (256, 256, 512, 512, 256, 256, 512, 128, 128)
    return pl.pallas_call(
        functools.partial(_proj0_kernel, d=d),
        grid=(m // TM,),
        in_specs=[pl.BlockSpec((TM, d), lambda i: (i, 0)), _mod_spec(mod, nb, nbatch), _full((1, d)),
                  _full(w.shape), _full(w2.shape), _full(db.shape),
                  pl.BlockSpec((TM, 128), lambda i: (i % nb, 0)),
                  pl.BlockSpec((TM, 128), lambda i: (i % nb, 0))],
        out_specs=[pl.BlockSpec((TM, n), lambda i: (i, 0)) for n in widths],
        out_shape=[jax.ShapeDtypeStruct((m, n), F32) for n in widths],
        compiler_params=_cparams(("arbitrary",), VMEM_LIMIT),
        name="proj0",
    )(xs, mod, g, w, w2, db, cos, sin)


def _gla_prepare(q_ref, k_ref, la_ref, st, cum, tot_row):
    c = GLA_CHUNK
    la = la_ref[...]
    la_hi = la.astype(BF16)
    la_lo = (la - la_hi.astype(F32)).astype(BF16)
    bc = (jnp.dot(cum, la_hi, preferred_element_type=F32)
          + jnp.dot(cum, la_lo, preferred_element_type=F32))
    tot = bc[tot_row:tot_row + 1, :]
    q_in = q_ref[...] * jnp.exp(bc)
    k_in = (k_ref[...] * jnp.exp(-bc)).astype(BF16)
    k_out = (k_ref[...] * jnp.exp(tot - bc)).astype(BF16)
    ones = jnp.ones((c, GLA_DV), BF16)
    tn = (((0,), (0,)), ((), ()))
    dcol = jnp.exp(lax.dot_general(la_hi, ones, tn, preferred_element_type=F32)
                   + lax.dot_general(la_lo, ones, tn, preferred_element_type=F32))
    return q_in, k_in, k_out, dcol, st[...].astype(BF16)


def _gla_head(h, prep, v_ref, o_ref, st, mask):
    q_in, k_in, k_out, dcol, s_prev = prep
    lane = lax.broadcasted_iota(jnp.int32, q_in.shape, 1)
    qh = jnp.where((lane // GLA_DK) == h, q_in, 0.0).astype(BF16)
    att = lax.dot_general(qh, k_in, (((1,), (1,)), ((), ())), preferred_element_type=F32)
    att = jnp.where(mask, att, 0.0).astype(BF16)
    vh = v_ref[:, h * GLA_DV:(h + 1) * GLA_DV].astype(BF16)
    o = (jnp.dot(att, vh, preferred_element_type=F32)
         + jnp.dot(qh, s_prev, preferred_element_type=F32))
    o_ref[:, h * GLA_DV:(h + 1) * GLA_DV] = o
    upd = lax.dot_general(k_out, vh, (((0,), (0,)), ((), ())), preferred_element_type=F32)
    r0, r1 = h * GLA_DK, (h + 1) * GLA_DK
    st[r0:r1, :] = st[r0:r1, :] * dcol[r0:r1, :] + upd[r0:r1, :]


def _gla_kernel(qf, kf, vf, laf, qb, kb, vb, lab, of_ref, ob_ref, sf, sb):
    s = pl.program_id(1)

    @pl.when(s == 0)
    def _():
        sf[...] = jnp.zeros_like(sf)
        sb[...] = jnp.zeros_like(sb)

    c = GLA_CHUNK
    ri = lax.broadcasted_iota(jnp.int32, (c, c), 0)
    ci = lax.broadcasted_iota(jnp.int32, (c, c), 1)
    nb = qf.shape[0]
    pf = [_gla_prepare(qf.at[b], kf.at[b], laf.at[b], sf.at[b], (ri >= ci).astype(BF16), c - 1) for b in range(nb)]
    pb = [_gla_prepare(qb.at[b], kb.at[b], lab.at[b], sb.at[b], (ri <= ci).astype(BF16), 0) for b in range(nb)]
    for h in range(GLA_HEADS):
        for b in range(nb):
            _gla_head(h, pf[b], vf.at[b], of_ref.at[b], sf.at[b], ri >= ci)
            _gla_head(h, pb[b], vb.at[b], ob_ref.at[b], sb.at[b], ci > ri)


GLA_NB = 4


def _gla(gq, gk, gv, laf, lab, nbatch, nctx_chunks):
    m = gq.shape[0]
    c = GLA_CHUNK
    nch = m // nbatch // c
    hk, hv = GLA_HEADS * GLA_DK, GLA_HEADS * GLA_DV
    nbs = GLA_NB if nbatch % GLA_NB == 0 else 1
    r3 = lambda a: a.reshape(nbatch, nch * c, a.shape[1])

    def fi(b, s):
        return (b, s, 0)

    def bi(b, s):
        return (b, jnp.where(s < nctx_chunks, nctx_chunks - 1 - s, nch - 1 + nctx_chunks - s), 0)

    sk = lambda im: pl.BlockSpec((nbs, c, hk), im)
    sv = lambda im: pl.BlockSpec((nbs, c, hv), im)
    of, ob = pl.pallas_call(
        _gla_kernel,
        grid=(nbatch // nbs, nch),
        in_specs=[sk(fi), sk(fi), sv(fi), sk(fi), sk(bi), sk(bi), sv(bi), sk(bi)],
        out_specs=[sv(fi), sv(bi)],
        out_shape=[jax.ShapeDtypeStruct((nbatch, nch * c, hv), F32)] * 2,
        scratch_shapes=[pltpu.VMEM((nbs, hk, GLA_DV), F32), pltpu.VMEM((nbs, hk, GLA_DV), F32)],
        compiler_params=_cparams(("arbitrary", "arbitrary")),
        name="gla_scan",
    )(r3(gq), r3(gk), r3(gv), r3(laf), r3(gq), r3(gk), r3(gv), r3(lab))
    return of.reshape(m, hv), ob.reshape(m, hv)


def _swa_kernel(q_ref, kc_ref, vc_ref, kp_ref, kcur_ref, kn_ref, vp_ref, vcur_ref, vn_ref, sink_ref,
                o_ref, *, nctx_blocks, nlat_blocks):
    j = pl.program_id(1)
    w = WINDOW
    n = j - nctx_blocks
    is_lat = j >= nctx_blocks
    nkc = kc_ref.shape[0]
    kall = jnp.concatenate([kc_ref[...], kp_ref[...], kcur_ref[...], kn_ref[...]], axis=0)
    vall = jnp.concatenate([vc_ref[...], vp_ref[...], vcur_ref[...], vn_ref[...]], axis=0)
    nk = kall.shape[0]
    lane = lax.broadcasted_iota(jnp.int32, kall.shape, 1)
    lo = lane < SWA_HEAD_DIM
    kroll = pltpu.roll(kall, SWA_HEAD_DIM, axis=1)
    vroll = pltpu.roll(vall, SWA_HEAD_DIM, axis=1)
    k2 = [jnp.where(lo, kall, kroll).astype(BF16), jnp.where(lo, kroll, kall).astype(BF16)]
    v2 = [jnp.where(lo, vall, vroll).astype(BF16), jnp.where(lo, vroll, vall).astype(BF16)]
    rep = SWA_HEADS // SWA_KV_HEADS
    qi = lax.broadcasted_iota(jnp.int32, (rep * w, nk), 0) % w
    kj = lax.broadcasted_iota(jnp.int32, (rep * w, nk), 1)
    qpos = n * w + qi
    kpos = (n - 1) * w + (kj - nkc)
    loc_ok = (jnp.abs(qpos - kpos) <= WINDOW) & (kpos >= 0) & (kpos < nlat_blocks * w) & is_lat
    valid = (kj < nkc) | ((kj >= nkc) & loc_ok)
    qlane = lax.broadcasted_iota(jnp.int32, (w, 128), 1)
    qlo = qlane < SWA_HEAD_DIM
    outs = []
    for g in range(SWA_KV_HEADS):
        qs, sinks = [], []
        for r in range(rep):
            pr, half = divmod(g * rep + r, 2)
            qp = q_ref[:, 128 * pr:128 * (pr + 1)]
            qs.append(jnp.where(qlo if half == 0 else ~qlo, qp, 0.0).astype(BF16))
            sinks.append(jnp.broadcast_to(sink_ref[g * rep + r:g * rep + r + 1, 0:1], (w, 1)))
        sink = jnp.concatenate(sinks, axis=0)
        s = lax.dot_general(jnp.concatenate(qs, axis=0), k2[g], (((1,), (1,)), ((), ())),
                            preferred_element_type=F32)
        s = jnp.where(valid, s, NEG)
        mx = jnp.maximum(jnp.max(s, axis=-1, keepdims=True), sink)
        p = jnp.exp(s - mx)
        den = jnp.sum(p, axis=-1, keepdims=True) + jnp.exp(sink - mx)
        o = jnp.dot(p.astype(BF16), v2[g], preferred_element_type=F32) / den
        outs += [o[r * w:(r + 1) * w, :] for r in range(rep)]
    for pr in range(SWA_HEADS // 2):
        o_ref[:, 128 * pr:128 * (pr + 1)] = jnp.where(qlo, outs[2 * pr], outs[2 * pr + 1])


def _swa(sq, sk, sv, sink, nbatch, nctx_blocks, nlat_blocks):
    m = sq.shape[0]
    w = WINDOW
    nblk = nctx_blocks + nlat_blocks
    nkc = nctx_blocks * w

    def qmap(b, j):
        return (b * nblk + j, 0)

    def nmap(off):
        def f(b, j):
            n = jnp.clip(j - nctx_blocks + off, 0, nlat_blocks - 1)
            return (b * nblk + nctx_blocks + n, 0)
        return f

    kvw = sk.shape[1]
    kcs = pl.BlockSpec((nkc, kvw), lambda b, j: (b * (nblk * w // nkc), 0))
    kvs = lambda off: pl.BlockSpec((w, kvw), nmap(off))
    return pl.pallas_call(
        functools.partial(_swa_kernel, nctx_blocks=nctx_blocks, nlat_blocks=nlat_blocks),
        grid=(nbatch, nblk),
        in_specs=[pl.BlockSpec((w, sq.shape[1]), qmap), kcs, kcs, kvs(-1), kvs(0), kvs(1),
                  kvs(-1), kvs(0), kvs(1), _full(sink.shape)],
        out_specs=pl.BlockSpec((w, sq.shape[1]), qmap),
        out_shape=jax.ShapeDtypeStruct(sq.shape, F32),
        compiler_params=_cparams(("arbitrary", "arbitrary")),
        name="swa_attn",
    )(sq, sk, sv, sk, sk, sk, sv, sv, sv, sink)


def _residual_epilogue(x, mix, mod_ref, n2_ref, xo_ref, f_ref, ft_ref, d):
    xn = x + _mod_rows(mod_ref, 2, d) * mix
    xo_ref[...] = xn
    f = _rms(xn) * n2_ref[...] * (1.0 + _mod_rows(mod_ref, 4, d)) + _mod_rows(mod_ref, 3, d)
    f_ref[...] = f
    ft_ref[...] = f.T.astype(BF16)


def _out0_kernel(x_ref, of_ref, ob_ref, gg_ref, a_ref, mod_ref, gn_ref, wo_ref, n2_ref,
                 xo_ref, f_ref, ft_ref, *, d):
    o = of_ref[...] + ob_ref[...]
    gate = gg_ref[...]
    gate = gate * _sigmoid(gate)
    parts = []
    for h in range(GLA_HEADS):
        oh = o[:, h * GLA_DV:(h + 1) * GLA_DV]
        parts.append(oh * lax.rsqrt(jnp.mean(oh * oh, axis=-1, keepdims=True) + NORM_EPS))
    on = jnp.concatenate(parts, axis=1) * gn_ref[...] * gate
    hv = GLA_HEADS * GLA_DV
    mix = _bdot(on, wo_ref[0:hv, :]) + _bdot(a_ref[...], wo_ref[hv:, :])
    _residual_epilogue(x_ref[...], mix, mod_ref, n2_ref, xo_ref, f_ref, ft_ref, d)


def _tok_spec(n):
    return pl.BlockSpec((TM, n), lambda i: (i, 0))


def _epilogue_specs(m, d):
    out_specs = [_tok_spec(d), _tok_spec(d), pl.BlockSpec((d, TM), lambda i: (0, i))]
    out_shape = [jax.ShapeDtypeStruct((m, d), F32), jax.ShapeDtypeStruct((m, d), F32),
                 jax.ShapeDtypeStruct((d, m), BF16)]
    return out_specs, out_shape


def _out0(xs, of, ob, gg, a, mod, gn, wo, n2, nb, nbatch):
    m, d = xs.shape
    out_specs, out_shape = _epilogue_specs(m, d)
    return pl.pallas_call(
        functools.partial(_out0_kernel, d=d),
        grid=(m // TM,),
        in_specs=[_tok_spec(d), _tok_spec(of.shape[1]), _tok_spec(ob.shape[1]), _tok_spec(gg.shape[1]),
                  _tok_spec(a.shape[1]), _mod_spec(mod, nb, nbatch), _full(gn.shape), _full(wo.shape),
                  _full(n2.shape)],
        out_specs=out_specs, out_shape=out_shape,
        compiler_params=_cparams(("arbitrary",), VMEM_LIMIT),
        name="out0",
    )(xs, of, ob, gg, a, mod, gn, wo, n2)


NO_RANK = 99.0
RANK_STEP = 1e27


def _top_vals(s, k, want_rank=False):
    lw = 128
    ng = s.shape[1] // lw
    rows = lax.broadcasted_iota(jnp.int32, (k, lw), 0)
    cur = [s[:, q * lw:(q + 1) * lw] for q in range(ng)]
    vals = [jnp.zeros((k, lw), F32) for _ in range(ng)]
    for t in range(k):
        for q in range(ng):
            mx = jnp.max(cur[q], axis=0, keepdims=True)
            vals[q] = jnp.where(rows == t, mx, vals[q])
            cur[q] = jnp.where(cur[q] == mx, NEG - t * RANK_STEP, cur[q])
    vals = jnp.concatenate(vals, axis=1)
    if not want_rank:
        return vals
    cur = jnp.concatenate(cur, axis=1)
    rank = jnp.where(cur < 0.5 * NEG, jnp.floor((NEG - cur) * (1.0 / RANK_STEP) + 0.5), NO_RANK)
    return vals, rank


def _peer_scores_kernel(f_ref, wh_ref, wl_ref, kh_ref, kl_ref, s1_ref, s2_ref):
    f = f_ref[...]
    fh = f.astype(BF16)
    fl = (f - fh.astype(F32)).astype(BF16)
    wh = wh_ref[...]
    q = (jnp.dot(fh, wh, preferred_element_type=F32) + jnp.dot(fl, wh, preferred_element_type=F32)
         + jnp.dot(fh, wl_ref[...], preferred_element_type=F32))
    qh = q.astype(BF16)
    ql = (q - qh.astype(F32)).astype(BF16)
    nt = (((1,), (1,)), ((), ()))
    for h in range(PEER_HEADS):
        for z, s_ref in enumerate((s1_ref, s2_ref)):
            c0 = (2 * h + z) * PEER_NKEYS
            kh = kh_ref[z, h]
            qhs = qh[:, c0:c0 + PEER_NKEYS]
            s_ref[h] = (lax.dot_general(kh, qhs, nt, preferred_element_type=F32)
                        + lax.dot_general(kh, ql[:, c0:c0 + PEER_NKEYS], nt, preferred_element_type=F32)
                        + lax.dot_general(kl_ref[z, h], qhs, nt, preferred_element_type=F32))


def _peer_q_kernel(s1_ref, s2_ref, r2_ref, cnt_ref, e1_ref, e2_ref):
    kk = PEER_TOPK
    for h in range(PEER_HEADS):
        s1 = s1_ref[h]
        s2 = s2_ref[h]
        v1 = _top_vals(s1, kk)
        v2, rank2 = _top_vals(s2, kk, want_rank=True)
        row8 = lax.broadcasted_iota(jnp.int32, (8, s1.shape[1]), 0)
        blocks = [v1[0:1, :] + v2]
        for r in range(1, 8):
            blocks.append(jnp.where(row8 < kk // (r + 1), v1[r:r + 1, :] + v2[0:8, :], NEG))
        blocks.append(v1[8:kk, :] + v2[0:1, :])
        top = _top_vals(jnp.concatenate(blocks, axis=0), kk)
        tau = top[kk - 1:kk, :]
        z = jnp.sum(jnp.exp(top - top[0:1, :]), axis=0, keepdims=True)
        for q in range(s1.shape[1] // 128):
            ln = slice(128 * q, 128 * (q + 1))
            v2b = [jnp.broadcast_to(v2[c:c + 1, ln], (8, 128)) for c in range(kk)]
            taub = jnp.broadcast_to(tau[:, ln], (8, 128))
            for rb in range(PEER_NKEYS // 8):
                x = s1[8 * rb:8 * (rb + 1), ln]
                cnt = jnp.zeros((8, 128), F32)
                for c in range(kk):
                    cnt = cnt + jnp.where(x + v2b[c] >= taub, 1.0, 0.0)
                cnt_ref[h, 8 * rb:8 * (rb + 1), ln] = cnt
        r2_ref[h] = rank2.astype(BF16)
        e1_ref[h] = jnp.exp(s1 - v1[0:1, :])
        e2_ref[h] = (jnp.exp(s2 - v2[0:1, :]) / z).astype(BF16)


def _peer_q(f, wh, wl, k1, k2):
    m, d = f.shape
    hh, nk = PEER_HEADS, PEER_NKEYS
    big = pl.BlockSpec((hh, nk, TM), lambda i: (0, 0, i))
    sh32 = jax.ShapeDtypeStruct((hh, nk, m), F32)
    sh16 = jax.ShapeDtypeStruct((hh, nk, m), BF16)
    kf = jnp.stack([k1, k2])
    kh = kf.astype(BF16)
    kl = (kf - kh.astype(F32)).astype(BF16)
    s1, s2 = pl.pallas_call(
        _peer_scores_kernel,
        grid=(m // TM,),
        in_specs=[_tok_spec(d), _full(wh.shape), _full(wl.shape), _full(kh.shape), _full(kl.shape)],
        out_specs=[big, big],
        out_shape=[sh32, sh32],
        compiler_params=_cparams(("arbitrary",), VMEM_LIMIT),
        name="peer_scores",
    )(f, wh, wl, kh, kl)
    return pl.pallas_call(
        _peer_q_kernel,
        grid=(m // TM,),
        in_specs=[big, big],
        out_specs=[big, big, big, big],
        out_shape=[sh16, sh32, sh32, sh16],
        compiler_params=_cparams(("arbitrary",), VMEM_LIMIT),
        name="peer_query",
    )(s1, s2)


PD_TM = 512
PD_TE = 2048
PD_SUBS = (1024, 1024)


def _gelu(x):
    return 0.5 * x * (1.0 + lax.erf(x * (2.0 ** -0.5)))


def _row_bf16(tile, r, rows):
    one = jnp.broadcast_to(tile[r:r + 1, :], (8, tile.shape[1]))
    one = jnp.concatenate([one, one], axis=0).astype(BF16)
    return jnp.concatenate([one] * (rows // 16), axis=0)


def _run_if(cond, fn):
    def body(_, carry):
        fn()
        return carry
    lax.fori_loop(0, cond.astype(jnp.int32), body, 0)


def _peer_dense_kernel(ft_ref, u_ref, vt_ref, r2_ref, cnt_ref, e1_ref, e2_ref, o_ref, acc_ref):
    k = pl.program_id(1)
    nk = PEER_NKEYS
    nslab = PD_TE // nk
    zero = jnp.zeros((nk, PD_TM), BF16)
    nsub = len(PD_SUBS)
    starts = [sum(PD_SUBS[:i]) for i in range(nsub + 1)]
    assert starts[-1] == PD_TE

    def first():
        acc_ref[...] = jnp.zeros_like(acc_ref)

    _run_if(k == 0, first)

    def hidden(sb):
        return jnp.dot(u_ref[starts[sb]:starts[sb + 1], :], ft_ref[...], preferred_element_type=F32)

    ht_next = hidden(0)
    for sb in range(nsub):
        e0, e1x = starts[sb], starts[sb + 1]
        ht = ht_next
        if sb + 1 < nsub:
            ht_next = hidden(sb + 1)
        gates = []
        for al in range(e0 // nk, e1x // nk):
            wg = zero
            a0 = pl.multiple_of(k * nslab + 8 * (al // 8), 8)
            for h in range(PEER_HEADS):
                cnt = _row_bf16(cnt_ref[h, pl.ds(a0, 8), :], al % 8, nk)
                e1 = _row_bf16(e1_ref[h, pl.ds(a0, 8), :], al % 8, nk)
                wg = wg + jnp.where(r2_ref[h] < cnt, e2_ref[h], zero) * e1
            gates.append(wg)
        ct = _gelu(ht).astype(BF16) * jnp.concatenate(gates, axis=0)
        acc_ref[...] += jnp.dot(vt_ref[:, e0:e1x], ct, preferred_element_type=F32)

    def last():
        o_ref[...] = acc_ref[...].T

    _run_if(k == pl.num_programs(1) - 1, last)


def _peer_dense(ft, u, vt, layer, r2, cnt, e1, e2):
    d, m = ft.shape
    ne = u.shape[1]
    hh, nk = PEER_HEADS, PEER_NKEYS
    big = pl.BlockSpec((hh, nk, PD_TM), lambda i, k: (0, 0, i))
    return pl.pallas_call(
        _peer_dense_kernel,
        grid=(m // PD_TM, ne // PD_TE),
        in_specs=[pl.BlockSpec((d, PD_TM), lambda i, k: (0, i)),
                  pl.BlockSpec((None, PD_TE, d), lambda i, k: (layer, k, 0)),
                  pl.BlockSpec((None, None, d, PD_TE), lambda i, k: (layer, k, 0, 0)),
                  big, big, big, big],
        out_specs=pl.BlockSpec((PD_TM, d), lambda i, k: (i, 0)),
        out_shape=jax.ShapeDtypeStruct((m, d), F32),
        scratch_shapes=[pltpu.VMEM((d, PD_TM), F32)],
        compiler_params=_cparams(("arbitrary", "arbitrary"), VMEM_LIMIT),
        name="peer_dense",
    )(ft, u, vt, r2, cnt, e1, e2)


def _res_kernel(x_ref, p_ref, mod_ref, modn_ref, g_ref, xo_ref, u_ref, *, d):
    xn = x_ref[...] + _mod_rows(mod_ref, 5, d) * p_ref[...]
    xo_ref[...] = xn
    u_ref[...] = _rms(xn) * g_ref[...] * (1.0 + _mod_rows(modn_ref, 1, d)) + _mod_rows(modn_ref, 0, d)


def _res(xs, p, mod, modn, g, nb, nbatch):
    m, d = xs.shape
    return pl.pallas_call(
        functools.partial(_res_kernel, d=d),
        grid=(m // TM,),
        in_specs=[_tok_spec(d), _tok_spec(d), _mod_spec(mod, nb, nbatch), _mod_spec(modn, nb, nbatch),
                  _full(g.shape)],
        out_specs=[_tok_spec(d), _tok_spec(d)],
        out_shape=[jax.ShapeDtypeStruct((m, d), F32)] * 2,
        compiler_params=_cparams(("arbitrary",)),
        name="peer_residual",
    )(xs, p, mod, modn, g)


def _final_kernel(x_ref, p_ref, mod_ref, g_ref, y_ref, *, d):
    y_ref[...] = _rms(x_ref[...] + _mod_rows(mod_ref, 5, d) * p_ref[...]) * g_ref[...]


def _final(xs, p, mod, g, nb):
    m, d = xs.shape
    return pl.pallas_call(
        functools.partial(_final_kernel, d=d),
        grid=(m // TM,),
        in_specs=[_tok_spec(d), _tok_spec(d), _mod_spec_lat(mod, nb - 1), _full(g.shape)],
        out_specs=_tok_spec(d),
        out_shape=jax.ShapeDtypeStruct((m, d), F32),
        compiler_params=_cparams(("arbitrary",)),
        name="final_norm",
    )(xs, p, mod, g)


def _split_dot(x, w):
    hi = x.astype(BF16)
    lo = (x - hi.astype(F32)).astype(BF16)
    return jnp.dot(hi, w, preferred_element_type=F32) + jnp.dot(lo, w, preferred_element_type=F32)


def _seg_sum(x, g_ref, gt_ref):
    return _split_dot(_split_dot(x, g_ref[...]), gt_ref[...])


def _rwkv_prep_kernel(u_ref, up_ref, un_ref, mu_ref, wr_ref, wk_ref, wv_ref, w0_ref, w1_ref, w2_ref,
                      a0_ref, a1_ref, a2_ref, g1_ref, g2_ref, kk_ref_, ka_ref, rk_ref, sg_ref, sgt_ref,
                      wf_o, wb_o, kf_o, kb_o, bf_o, bb_o, r_o, v_o, kk_o, g_o, bonus_o, *, nb, d):
    i = pl.program_id(0)
    p = i % nb
    u = u_ref[...]
    has_prev = (p >= 2).astype(F32)
    has_next = ((p >= 1) & (p <= nb - 2)).astype(F32)
    prev_row = up_ref[7:8, :] * has_prev
    next_row = un_ref[0:1, :] * has_next
    rows = lax.broadcasted_iota(jnp.int32, u.shape, 0)
    up = jnp.where(rows == 0, prev_row, pltpu.roll(u, 1, axis=0))
    un = jnp.where(rows == TM - 1, next_row, pltpu.roll(u, TM - 1, axis=0))
    xx = 0.5 * (up + un) - u

    def mix(zi):
        return (u + xx * mu_ref[zi:zi + 1, :]).astype(BF16)

    r = jnp.dot(mix(0), wr_ref[...], preferred_element_type=F32)
    k = jnp.dot(mix(2), wk_ref[...], preferred_element_type=F32)
    v = jnp.dot(mix(3), wv_ref[...], preferred_element_type=F32)
    lw = w0_ref[...] + _bdot(jnp.tanh(jnp.dot(mix(1), w1_ref[...], preferred_element_type=F32)), w2_ref[...])
    decay = jnp.exp(-_sigmoid(lw) * float(np.exp(-0.5)))
    a = _sigmoid(a0_ref[...] + _bdot(jnp.dot(mix(4), a1_ref[...], preferred_element_type=F32), a2_ref[...]))
    g = _bdot(_sigmoid(jnp.dot(mix(5), g1_ref[...], preferred_element_type=F32)), g2_ref[...])
    kk = k * kk_ref_[...]
    kk = kk * lax.rsqrt(jnp.maximum(_seg_sum(kk * kk, sg_ref, sgt_ref), 1e-24))
    ka = ka_ref[...]
    kf = k * (1.0 + (a[:, :d] - 1.0) * ka)
    kb = k * (1.0 + (a[:, d:] - 1.0) * ka)
    bonus = _seg_sum(r * (kf + kb) * rk_ref[...], sg_ref, sgt_ref) * v
    wf_o[...] = decay[:, :d]
    wb_o[...] = decay[:, d:]
    kf_o[...] = kf
    kb_o[...] = kb
    bf_o[...] = kk * a[:, :d]
    bb_o[...] = kk * a[:, d:]
    r_o[...] = r
    v_o[...] = v
    kk_o[...] = kk
    g_o[...] = g
    bonus_o[...] = bonus


def _rwkv_prep(u, consts, nb):
    m, d = u.shape
    r8 = TM // 8
    nb8 = m // 8
    in_specs = [_tok_spec(d),
                pl.BlockSpec((8, d), lambda i: (jnp.maximum(i * r8 - 1, 0), 0)),
                pl.BlockSpec((8, d), lambda i: (jnp.minimum((i + 1) * r8, nb8 - 1), 0))]
    in_specs += [_full(c.shape) for c in consts]
    return pl.pallas_call(
        functools.partial(_rwkv_prep_kernel, nb=nb, d=d),
        grid=(m // TM,),
        in_specs=in_specs,
        out_specs=[_tok_spec(d)] * 11,
        out_shape=[jax.ShapeDtypeStruct((m, d), F32)] * 11,
        compiler_params=_cparams(("arbitrary",), VMEM_LIMIT),
        name="rwkv_prep",
    )(u, u, u, *consts)


RW_TC = 256
RW_UNROLL = 8


def _rwkv_scan_kernel(wf, kf, bf, rf, vf, kkf, wb, kb, bb, rb, vb, kkb, yf_ref, yb_ref,
                      st, vk_a, vk_b, ycol_scr, sa_scr, *, npair):
    s = pl.program_id(1)
    n = RW_HEAD_DIM
    grp = 8

    @pl.when(s == 0)
    def _():
        st[...] = jnp.zeros_like(st)
        ycol_scr[...] = jnp.zeros_like(ycol_scr)

    w2 = 2 * n
    ri = lax.broadcasted_iota(jnp.int32, (n, w2), 0)
    li = lax.broadcasted_iota(jnp.int32, (n, w2), 1)
    eye_a = (li == ri).astype(BF16)
    bi = lax.broadcasted_iota(jnp.int32, (w2, w2), 0)
    bj = lax.broadcasted_iota(jnp.int32, (w2, w2), 1)
    ones_blk = ((bi // n) == (bj // n)).astype(BF16)
    b4i = lax.broadcasted_iota(jnp.int32, (2 * w2, 2 * w2), 0)
    b4j = lax.broadcasted_iota(jnp.int32, (2 * w2, 2 * w2), 1)
    ones_blk2 = ((b4i // n) == (b4j // n)).astype(BF16)
    lane8 = lax.broadcasted_iota(jnp.int32, (grp, w2), 1)
    rows16 = lax.broadcasted_iota(jnp.int32, (2 * grp, w2), 0)
    lane16 = lax.broadcasted_iota(jnp.int32, (2 * grp, w2), 1)
    untr = ((lane16 == rows16) | (lane16 == rows16 + n)).astype(BF16)
    dirs = ((wf, kf, bf, rf, vf, kkf, yf_ref), (wb, kb, bb, rb, vb, kkb, yb_ref))
    nch = 2 * npair
    ngrp = RW_TC // grp
    nt = (((1,), (1,)), ((), ()))

    def halves(tile):
        return jnp.concatenate([jnp.where(lane8 < n, tile, 0.0), jnp.where(lane8 >= n, tile, 0.0)], axis=0)

    assert nch == 2 * grp

    def group_base(g, dd):
        base = (g if dd == 0 else ngrp - 1 - g) * grp
        return base if isinstance(base, int) else pl.multiple_of(base, grp)

    def lanes(c):
        p = c % npair
        return slice(w2 * p, w2 * (p + 1))

    def precompute(g, chains, vk_dst):
        gq = jnp.minimum(g, ngrp - 1)
        v2, ks = [], []
        for c in chains:
            dd = c // npair
            base = group_base(gq, dd)
            v_t = dirs[dd][4][pl.ds(base, grp), lanes(c)]
            ks.append(dirs[dd][1][pl.ds(base, grp), lanes(c)])
            v2 += [jnp.where(lane8 < n, v_t, 0.0), jnp.where(lane8 < n, pltpu.roll(v_t, n, axis=1), 0.0)]
        vt = lax.dot_general(eye_a, jnp.concatenate(v2, axis=0).astype(BF16), nt,
                             preferred_element_type=F32).astype(BF16)
        for idx, c in enumerate(chains):
            k2 = halves(ks[idx])
            kblk = jnp.concatenate(
                [jnp.where((rows16 == j) | (rows16 == grp + j), k2, 0.0) for j in range(grp)], axis=1)
            vk_dst[c] = jnp.dot(vt[:, 2 * grp * idx:2 * grp * (idx + 1)], kblk.astype(BF16),
                                preferred_element_type=F32)

    def untranspose(g, chains, ycol_prev):
        for c in chains:
            dd = c // npair
            yc = ycol_prev[c]
            ycs = jnp.concatenate([jnp.where(li < n, yc, 0.0), jnp.where(li >= n, yc, 0.0)], axis=0)
            yt = lax.dot_general(untr, ycs.astype(BF16), nt, preferred_element_type=F32)
            dirs[dd][6][pl.ds(group_base(g, dd), grp), lanes(c)] = yt[0:grp, :]

    def run_group(g, vk_cur, vk_next, states, sa, ycol_prev):
        tiles = []
        for c in range(nch):
            dd = c // npair
            base = group_base(g, dd)
            tiles.append([dirs[dd][q][pl.ds(base, grp), lanes(c)] for q in (0, 2, 3, 5)])
        gprev = jnp.maximum(g - 1, 0)
        gnext = jnp.minimum(g + 1, ngrp - 1)
        kap_next = []
        for c in range(nch):
            dd = c // npair
            kap_next.append(dirs[dd][5][pl.ds(group_base(gnext, dd), grp), lanes(c)])
        states, sa = list(states), list(sa)
        ycol = [jnp.zeros((n, w2), F32) for _ in range(nch)]
        for j in range(grp):
            lhs = []
            for c in range(nch):
                w_t, b_t, r_t, kk_t = tiles[c]
                fwd = c < npair
                jj = j if fwd else grp - 1 - j
                row = lambda tl, q: jnp.broadcast_to(tl[q:q + 1, :], (n, w2))
                states[c] = (states[c] * row(w_t, jj) - sa[c] * row(b_t, jj)
                             + vk_cur[c, :, jj * w2:(jj + 1) * w2])
                if j + 1 < grp:
                    kap = row(kk_t, jj + 1 if fwd else jj - 1)
                else:
                    kap = row(kap_next[c], 0 if fwd else grp - 1)
                lhs.append(jnp.concatenate([(states[c] * kap).astype(BF16),
                                            (states[c] * row(r_t, jj)).astype(BF16)], axis=1))
            red = jnp.dot(jnp.concatenate(lhs, axis=0), ones_blk2, preferred_element_type=F32)
            for c in range(nch):
                jj = j if c < npair else grp - 1 - j
                sa[c] = red[c * n:(c + 1) * n, :w2]
                ycol[c] = jnp.where((li % n) == jj, red[c * n:(c + 1) * n, w2:], ycol[c])
            pair = (2 * j, 2 * j + 1)
            precompute(g + 1, pair, vk_next)
            untranspose(gprev, pair, ycol_prev)
        return states, sa, ycol

    precompute(0, tuple(range(nch)), vk_a)
    first = []
    for c in range(nch):
        dd = c // npair
        kk_t = dirs[dd][5][pl.ds(group_base(0, dd), grp), lanes(c)]
        q0 = 0 if dd == 0 else grp - 1
        first.append((st[c] * jnp.broadcast_to(kk_t[q0:q0 + 1, :], (n, w2))).astype(BF16))
    sa0 = jnp.dot(jnp.concatenate(first, axis=0), ones_blk, preferred_element_type=F32)
    for c in range(nch):
        sa_scr[c] = sa0[c * n:(c + 1) * n, :]

    def two_groups(i, carry):
        states = [st[c] for c in range(nch)]
        sa = [sa_scr[c] for c in range(nch)]
        ycol = [ycol_scr[c] for c in range(nch)]
        for u in range(0, RW_UNROLL, 2):
            states, sa, ycol = run_group(RW_UNROLL * i + u, vk_a, vk_b, states, sa, ycol)
            states, sa, ycol = run_group(RW_UNROLL * i + u + 1, vk_b, vk_a, states, sa, ycol)
        for c in range(nch):
            st[c] = states[c]
            sa_scr[c] = sa[c]
            ycol_scr[c] = ycol[c]
        return carry

    lax.fori_loop(0, ngrp // RW_UNROLL, two_groups, 0)
    untranspose(ngrp - 1, tuple(range(nch)), [ycol_scr[c] for c in range(nch)])


def _rwkv_scan(wf, kf, bf, wb, kb, bb, r, v, kk, nbatch):
    m, d = r.shape
    nch = m // nbatch // RW_TC
    npair = d // (2 * RW_HEAD_DIM)

    def fi(b, s):
        return (b * nch + s, 0)

    def bi(b, s):
        return (b * nch + jnp.where(s == 0, 0, nch - s), 0)

    sf = pl.BlockSpec((RW_TC, d), fi)
    sb = pl.BlockSpec((RW_TC, d), bi)
    return pl.pallas_call(
        functools.partial(_rwkv_scan_kernel, npair=npair),
        grid=(nbatch, nch),
        in_specs=[sf] * 6 + [sb] * 6,
        out_specs=[sf, sb],
        out_shape=[jax.ShapeDtypeStruct((m, d), F32)] * 2,
        scratch_shapes=[pltpu.VMEM((2 * npair, RW_HEAD_DIM, 2 * RW_HEAD_DIM), F32),
                        pltpu.VMEM((2 * npair, RW_HEAD_DIM, 8 * 2 * RW_HEAD_DIM), F32),
                        pltpu.VMEM((2 * npair, RW_HEAD_DIM, 8 * 2 * RW_HEAD_DIM), F32),
                        pltpu.VMEM((2 * npair, RW_HEAD_DIM, 2 * RW_HEAD_DIM), F32),
                        pltpu.VMEM((2 * npair, RW_HEAD_DIM, 2 * RW_HEAD_DIM), F32)],
        compiler_params=_cparams(("arbitrary", "arbitrary"), VMEM_LIMIT),
        name="rwkv_scan",
    )(wf, kf, bf, r, v, kk, wb, kb, bb, r, v, kk)


def _rwkv_out_kernel(x_ref, yf_ref, yb_ref, bonus_ref, g_ref, mod_ref, lg_ref, lb_ref, wo_ref, n2_ref,
                     sg_ref, sgt_ref, xo_ref, f_ref, ft_ref, *, d):
    y = yf_ref[...] + yb_ref[...]
    inv = 1.0 / RW_HEAD_DIM
    mean = _seg_sum(y, sg_ref, sgt_ref) * inv
    yc = y - mean
    var = _seg_sum(yc * yc, sg_ref, sgt_ref) * inv
    yn = yc * lax.rsqrt(var + RW_GN_EPS) * lg_ref[...] + lb_ref[...]
    out = (yn + bonus_ref[...]) * g_ref[...]
    mix = _bdot(out, wo_ref[...])
    _residual_epilogue(x_ref[...], mix, mod_ref, n2_ref, xo_ref, f_ref, ft_ref, d)


def _rwkv_out(xs, yf, yb, bonus, g, mod, lg, lb, wo, n2, sg, sgt, nb, nbatch):
    d = xs.shape[1]
    m = nbatch * (nb - 1) * TM
    out_specs, out_shape = _epilogue_specs(m, d)
    return pl.pallas_call(
        functools.partial(_rwkv_out_kernel, d=d),
        grid=(m // TM,),
        in_specs=[_lat_tok_spec(d, nb)] * 5 + [_mod_spec_lat(mod, nb - 1)]
        + [_full(a.shape) for a in (lg, lb, wo, n2, sg, sgt)],
        out_specs=out_specs, out_shape=out_shape,
        compiler_params=_cparams(("arbitrary",), VMEM_LIMIT),
        name="rwkv_out",
    )(xs, yf, yb, bonus, g, mod, lg, lb, wo, n2, sg, sgt)


def _rope_tables(t_lat, n_ctx):
    rows = t_lat // GRID_W
    row = jnp.broadcast_to(jnp.arange(rows, dtype=F32)[:, None], (rows, GRID_W)).reshape(-1)
    col = jnp.broadcast_to(jnp.arange(GRID_W, dtype=F32)[None, :], (rows, GRID_W)).reshape(-1)
    n_freq = SWA_HEAD_DIM // 4
    inv_freq = ROPE_BASE ** (-jnp.arange(n_freq, dtype=F32) / n_freq)
    ar = row[:, None] * inv_freq
    ac = col[:, None] * inv_freq
    cos = jnp.concatenate([jnp.cos(ar), jnp.cos(ar), jnp.cos(ac), jnp.cos(ac)], axis=1)
    sin = jnp.concatenate([-jnp.sin(ar), jnp.sin(ar), -jnp.sin(ac), jnp.sin(ac)], axis=1)
    cos = jnp.concatenate([jnp.ones((n_ctx, SWA_HEAD_DIM), F32), cos], axis=0)
    sin = jnp.concatenate([jnp.zeros((n_ctx, SWA_HEAD_DIM), F32), sin], axis=0)
    return jnp.tile(cos, (1, 2)), jnp.tile(sin, (1, 2))


def _peer(f, ft, w_q, k1, k2, u_all, vt_all, layer):
    wh = w_q.astype(BF16)
    wl = (w_q - wh.astype(F32)).astype(BF16)
    r2, cnt, e1, e2 = _peer_q(f, wh, wl, k1, k2)
    return _peer_dense(ft, u_all, vt_all, layer, r2, cnt, e1, e2)


def kernel(x, c, ctx, c_ctx, ada_w, ada_b, norm1_g, norm2_g, ab_w_in, gla_dec_w2, gla_dec_b, gla_norm_g,
           swa_sink, ab_w_out, rw_mu, rw_w_rkv, rw_w_o, rw_w0, rw_w1, rw_w2, rw_a0, rw_a1, rw_a2, rw_g1,
           rw_g2, rw_k_k, rw_k_a, rw_r_k, rw_ln_g, rw_ln_b, peer_w_q, peer_k1, peer_k2, peer_u, peer_v,
           final_g):
    nbatch, t_lat, d = x.shape
    n_ctx = ctx.shape[1]
    assert n_ctx == TM and t_lat % TM == 0 and d == 1024
    s_tok = n_ctx + t_lat
    nb = s_tok // TM
    m = nbatch * s_tok
    xs = jnp.concatenate([ctx, x], axis=1).reshape(m, d)

    cc = jnp.zeros((16, d), F32).at[:nbatch].set(c).at[nbatch].set(c_ctx)
    mod0 = _ada_table(cc, ada_w[0], ada_b[0]).reshape(16, 1, 6 * d)
    mod1 = _ada_table(cc, ada_w[1], ada_b[1]).reshape(16, 1, 6 * d)
    row2 = lambda a: a.reshape(1, -1)

    w_in = ab_w_in[0]
    cuts = np.cumsum((256, 256, 512, 512, 32, 512, 128, 128))[:-1].tolist()
    wgq, wgk, wgv, wgg, wlr, wsq, wsk, wsv = jnp.split(w_in, cuts, axis=1)
    wp = jnp.concatenate([wgq, wgk, wgv, wgg, wsq, wsk, wsv, wlr,
                          jnp.zeros((d, _P_END - _P_LR - 32), F32)], axis=1).astype(BF16)
    hk = GLA_HEADS * GLA_DK
    w2p = jnp.zeros((128, 2 * hk), F32)
    w2p = w2p.at[0:GLA_LOWRANK, 0:hk].set(gla_dec_w2[0, 0]).at[GLA_LOWRANK:2 * GLA_LOWRANK, hk:].set(gla_dec_w2[0, 1])
    db = gla_dec_b[0].reshape(1, 2 * hk)
    cos, sin = _rope_tables(t_lat, n_ctx)
    gq, gk, gv, gg, laf, lab, sq, sk, sv = _proj0(xs, mod0, row2(norm1_g[0]), wp, w2p, db, cos, sin, nb, nbatch)
    of, ob = _gla(gq, gk, gv, laf, lab, nbatch, n_ctx // GLA_CHUNK)
    sink = jnp.broadcast_to(swa_sink[0][:, None], (SWA_HEADS, 128))
    att = _swa(sq, sk, sv, sink, nbatch, n_ctx // WINDOW, t_lat // WINDOW)
    xs, f, ft = _out0(xs, of, ob, gg, att, mod0, row2(gla_norm_g[0]), ab_w_out[0].astype(BF16),
                      row2(norm2_g[0]), nb, nbatch)
    u_all = peer_u.astype(BF16)
    vt_all = jnp.swapaxes(peer_v.reshape(peer_v.shape[0], -1, PD_TE, d), 2, 3).astype(BF16)
    p = _peer(f, ft, peer_w_q[0], peer_k1[0], peer_k2[0], u_all, vt_all, 0)
    xs, u = _res(xs, p, mod0, mod1, row2(norm1_g[1]), nb, nbatch)

    lora = rw_w1.shape[-1]
    w1c = jnp.concatenate([rw_w1[0, 0], rw_w1[0, 1]], axis=1).astype(BF16)
    w2c = jnp.zeros((2 * lora, 2 * d), F32).at[:lora, :d].set(rw_w2[0, 0]).at[lora:, d:].set(rw_w2[0, 1]).astype(BF16)
    la = rw_a1.shape[-1]
    a1c = jnp.concatenate([rw_a1[0, 0], rw_a1[0, 1]], axis=1).astype(BF16)
    a2c = jnp.zeros((2 * la, 2 * d), F32).at[:la, :d].set(rw_a2[0, 0]).at[la:, d:].set(rw_a2[0, 1]).astype(BF16)
    lg = rw_g1.shape[-1]
    g1p = jnp.zeros((d, 256), F32).at[:, :lg].set(rw_g1[0]).astype(BF16)
    g2p = jnp.zeros((256, d), F32).at[:lg].set(rw_g2[0]).astype(BF16)
    sg = (jnp.arange(d)[:, None] // RW_HEAD_DIM == jnp.arange(128)[None, :]).astype(BF16)
    sgt = sg.T
    consts = [rw_mu[0], rw_w_rkv[0, 0].astype(BF16), rw_w_rkv[0, 1].astype(BF16), rw_w_rkv[0, 2].astype(BF16),
              rw_w0[0].reshape(1, 2 * d), w1c, w2c, rw_a0[0].reshape(1, 2 * d), a1c, a2c, g1p, g2p,
              row2(rw_k_k[0]), row2(rw_k_a[0]), rw_r_k[0].reshape(1, d), sg, sgt]
    mu8 = jnp.zeros((8, d), F32).at[:6].set(rw_mu[0])
    consts[0] = mu8
    wf, wb, kf, kb, bf, bb, r, v, kk, g, bonus = _rwkv_prep(u, consts, nb)
    yf, yb = _rwkv_scan(wf, kf, bf, wb, kb, bb, r, v, kk, nbatch)
    xs, f, ft = _rwkv_out(xs, yf, yb, bonus, g, mod1, row2(rw_ln_g[0]), row2(rw_ln_b[0]),
                          rw_w_o[0].astype(BF16), row2(norm2_g[1]), sg, sgt, nb, nbatch)
    p = _peer(f, ft, peer_w_q[1], peer_k1[1], peer_k2[1], u_all, vt_all, 1)
    y = _final(xs, p, mod1, row2(final_g), nb)
    return y.reshape(nbatch, t_lat, d)
```

```python
import functools

import numpy as np
import jax
import jax.numpy as jnp
from jax import lax
from jax.experimental import pallas as pl
from jax.experimental.pallas import tpu as pltpu

F32 = jnp.float32
BF16 = jnp.bfloat16
HI = lax.Precision.HIGHEST

NORM_EPS = 1e-6
GLA_HEADS, GLA_DK, GLA_DV, GLA_LOWRANK, GLA_TAU, GLA_CHUNK = 4, 64, 128, 16, 16.0, 64
SWA_HEADS, SWA_KV_HEADS, SWA_HEAD_DIM, WINDOW = 8, 2, 64, 128
ROPE_BASE = 10000.0
GRID_W = 64
RW_HEAD_DIM = 64
RW_GN_EPS = 64e-5
PEER_HEADS, PEER_NKEYS, PEER_TOPK = 8, 128, 16
NEG = -1e30

TM = 256
VMEM_LIMIT = 56 * 1024 * 1024


def _cparams(sem, vmem=None):
    return pltpu.CompilerParams(dimension_semantics=sem, vmem_limit_bytes=vmem)


def _bdot(a, b):
    return jnp.dot(a.astype(BF16), b.astype(BF16), preferred_element_type=F32)


def _hdot(a, b):
    return jnp.dot(a, b, precision=HI, preferred_element_type=F32)


def _sigmoid(x):
    return 1.0 / (1.0 + jnp.exp(-x))


def _rms(x):
    return x * lax.rsqrt(jnp.mean(x * x, axis=-1, keepdims=True) + NORM_EPS)


def _mod_rows(mod_ref, k, d):
    return mod_ref[0, :, k * d:(k + 1) * d]


def _mod_spec(mod, nb, nbatch):
    return pl.BlockSpec((1, 1, mod.shape[2]), lambda i: (jnp.where(i % nb == 0, nbatch, i // nb), 0, 0))


def _mod_spec_lat(mod, nbl):
    return pl.BlockSpec((1, 1, mod.shape[2]), lambda i: (i // nbl, 0, 0))


def _lat_tok_spec(n, nb):
    return pl.BlockSpec((TM, n), lambda i: ((i // (nb - 1)) * nb + 1 + i % (nb - 1), 0))


def _full(shape):
    n = len(shape)
    return pl.BlockSpec(shape, lambda *_: (0,) * n)


def _ada_kernel(c_ref, w_ref, b_ref, o_ref):
    c = c_ref[...]
    s = c * _sigmoid(c)
    o_ref[...] = _hdot(s, w_ref[...]) + b_ref[...]


def _ada_table(cc, w, b):
    rows, d = cc.shape
    n = w.shape[1]
    tn = 512
    return pl.pallas_call(
        _ada_kernel,
        grid=(n // tn,),
        in_specs=[_full((rows, d)), pl.BlockSpec((d, tn), lambda j: (0, j)),
                  pl.BlockSpec((1, tn), lambda j: (0, j))],
        out_specs=pl.BlockSpec((rows, tn), lambda j: (0, j)),
        out_shape=jax.ShapeDtypeStruct((rows, n), F32),
        compiler_params=_cparams(("arbitrary",)),
        name="ada_table",
    )(cc, w, b.reshape(1, n))


_P_GQ, _P_GK, _P_GV, _P_GG, _P_SQ, _P_SK, _P_SV, _P_LR, _P_END = 0, 256, 512, 1024, 1536, 2048, 2176, 2304, 2432


def _rope(x, cos, sin):
    lane = lax.broadcasted_iota(jnp.int32, x.shape, 1)
    up = pltpu.roll(x, 112, axis=1)
    dn = pltpu.roll(x, 16, axis=1)
    sw = jnp.where((lane % 32) < 16, up, dn)
    return x * cos + sw * sin


def _proj0_kernel(x_ref, mod_ref, g_ref, w_ref, w2_ref, db_ref, cos_ref, sin_ref,
                  gq_ref, gk_ref, gv_ref, gg_ref, laf_ref, lab_ref, sq_ref, sk_ref, sv_ref,
                  *, d):
    u = (_rms(x_ref[...]) * g_ref[...] * (1.0 + _mod_rows(mod_ref, 1, d))
         + _mod_rows(mod_ref, 0, d)).astype(BF16)

    def seg(a, b):
        return jnp.dot(u, w_ref[:, a:b], preferred_element_type=F32)

    gq_ref[...] = seg(_P_GQ, _P_GK) * (GLA_DK ** -0.5)
    gk_ref[...] = seg(_P_GK, _P_GV)
    gv_ref[...] = seg(_P_GV, _P_GG)
    gg_ref[...] = seg(_P_GG, _P_SQ)
    lr = seg(_P_LR, _P_END)
    z = _hdot(lr, w2_ref[...]) + db_ref[...]
    la = (jnp.minimum(z, 0.0) - jnp.log(1.0 + jnp.exp(-jnp.abs(z)))) * (1.0 / GLA_TAU)
    hk = GLA_HEADS * GLA_DK
    laf_ref[...] = la[:, :hk]
    lab_ref[...] = la[:, hk:]
    cos = cos_ref[...]
    sin = sin_ref[...]
    sq = seg(_P_SQ, _P_SK) * (SWA_HEAD_DIM ** -0.5)
    for c in range(4):
        sq_ref[:, 128 * c:128 * (c + 1)] = _rope(sq[:, 128 * c:128 * (c + 1)], cos, sin)
    sk_ref[...] = _rope(seg(_P_SK, _P_SV), cos, sin)
    sv_ref[...] = seg(_P_SV, _P_LR)


def _proj0(xs, mod, g, w, w2, db, cos, sin, nb, nbatch):
    m, d = xs.shape
    widths = (256, 256, 512, 512, 256, 256, 512, 128, 128)
    return pl.pallas_call(
        functools.partial(_proj0_kernel, d=d),
        grid=(m // TM,),
        in_specs=[pl.BlockSpec((TM, d), lambda i: (i, 0)), _mod_spec(mod, nb, nbatch), _full((1, d)),
                  _full(w.shape), _full(w2.shape), _full(db.shape),
                  pl.BlockSpec((TM, 128), lambda i: (i % nb, 0)),
                  pl.BlockSpec((TM, 128), lambda i: (i % nb, 0))],
        out_specs=[pl.BlockSpec((TM, n), lambda i: (i, 0)) for n in widths],
        out_shape=[jax.ShapeDtypeStruct((m, n), F32) for n in widths],
        compiler_params=_cparams(("arbitrary",), VMEM_LIMIT),
        name="proj0",
    )(xs, mod, g, w, w2, db, cos, sin)


def _gla_prepare(q_ref, k_ref, la_ref, st, cum, tot_row):
    c = GLA_CHUNK
    la = la_ref[...]
    la_hi = la.astype(BF16)
    la_lo = (la - la_hi.astype(F32)).astype(BF16)
    bc = (jnp.dot(cum, la_hi, preferred_element_type=F32)
          + jnp.dot(cum, la_lo, preferred_element_type=F32))
    tot = bc[tot_row:tot_row + 1, :]
    q_in = q_ref[...] * jnp.exp(bc)
    k_in = (k_ref[...] * jnp.exp(-bc)).astype(BF16)
    k_out = (k_ref[...] * jnp.exp(tot - bc)).astype(BF16)
    ones = jnp.ones((c, GLA_DV), BF16)
    tn = (((0,), (0,)), ((), ()))
    dcol = jnp.exp(lax.dot_general(la_hi, ones, tn, preferred_element_type=F32)
                   + lax.dot_general(la_lo, ones, tn, preferred_element_type=F32))
    return q_in, k_in, k_out, dcol, st[...].astype(BF16)


def _gla_head(h, prep, v_ref, o_ref, st, mask):
    q_in, k_in, k_out, dcol, s_prev = prep
    lane = lax.broadcasted_iota(jnp.int32, q_in.shape, 1)
    qh = jnp.where((lane // GLA_DK) == h, q_in, 0.0).astype(BF16)
    att = lax.dot_general(qh, k_in, (((1,), (1,)), ((), ())), preferred_element_type=F32)
    att = jnp.where(mask, att, 0.0).astype(BF16)
    vh = v_ref[:, h * GLA_DV:(h + 1) * GLA_DV].astype(BF16)
    o = (jnp.dot(att, vh, preferred_element_type=F32)
         + jnp.dot(qh, s_prev, preferred_element_type=F32))
    o_ref[:, h * GLA_DV:(h + 1) * GLA_DV] = o
    upd = lax.dot_general(k_out, vh, (((0,), (0,)), ((), ())), preferred_element_type=F32)
    r0, r1 = h * GLA_DK, (h + 1) * GLA_DK
    st[r0:r1, :] = st[r0:r1, :] * dcol[r0:r1, :] + upd[r0:r1, :]


def _gla_kernel(qf, kf, vf, laf, qb, kb, vb, lab, of_ref, ob_ref, sf, sb):
    s = pl.program_id(1)

    @pl.when(s == 0)
    def _():
        sf[...] = jnp.zeros_like(sf)
        sb[...] = jnp.zeros_like(sb)

    c = GLA_CHUNK
    ri = lax.broadcasted_iota(jnp.int32, (c, c), 0)
    ci = lax.broadcasted_iota(jnp.int32, (c, c), 1)
    nb = qf.shape[0]
    pf = [_gla_prepare(qf.at[b], kf.at[b], laf.at[b], sf.at[b], (ri >= ci).astype(BF16), c - 1) for b in range(nb)]
    pb = [_gla_prepare(qb.at[b], kb.at[b], lab.at[b], sb.at[b], (ri <= ci).astype(BF16), 0) for b in range(nb)]
    for h in range(GLA_HEADS):
        for b in range(nb):
            _gla_head(h, pf[b], vf.at[b], of_ref.at[b], sf.at[b], ri >= ci)
            _gla_head(h, pb[b], vb.at[b], ob_ref.at[b], sb.at[b], ci > ri)


GLA_NB = 4


def _gla(gq, gk, gv, laf, lab, nbatch, nctx_chunks):
    m = gq.shape[0]
    c = GLA_CHUNK
    nch = m // nbatch // c
    hk, hv = GLA_HEADS * GLA_DK, GLA_HEADS * GLA_DV
    nbs = GLA_NB if nbatch % GLA_NB == 0 else 1
    r3 = lambda a: a.reshape(nbatch, nch * c, a.shape[1])

    def fi(b, s):
        return (b, s, 0)

    def bi(b, s):
        return (b, jnp.where(s < nctx_chunks, nctx_chunks - 1 - s, nch - 1 + nctx_chunks - s), 0)

    sk = lambda im: pl.BlockSpec((nbs, c, hk), im)
    sv = lambda im: pl.BlockSpec((nbs, c, hv), im)
    of, ob = pl.pallas_call(
        _gla_kernel,
        grid=(nbatch // nbs, nch),
        in_specs=[sk(fi), sk(fi), sv(fi), sk(fi), sk(bi), sk(bi), sv(bi), sk(bi)],
        out_specs=[sv(fi), sv(bi)],
        out_shape=[jax.ShapeDtypeStruct((nbatch, nch * c, hv), F32)] * 2,
        scratch_shapes=[pltpu.VMEM((nbs, hk, GLA_DV), F32), pltpu.VMEM((nbs, hk, GLA_DV), F32)],
        compiler_params=_cparams(("arbitrary", "arbitrary")),
        name="gla_scan",
    )(r3(gq), r3(gk), r3(gv), r3(laf), r3(gq), r3(gk), r3(gv), r3(lab))
    return of.reshape(m, hv), ob.reshape(m, hv)


def _swa_kernel(q_ref, kc_ref, vc_ref, kp_ref, kcur_ref, kn_ref, vp_ref, vcur_ref, vn_ref, sink_ref,
                o_ref, *, nctx_blocks, nlat_blocks):
    j = pl.program_id(1)
    w = WINDOW
    n = j - nctx_blocks
    is_lat = j >= nctx_blocks
    nkc = kc_ref.shape[0]
    kall = jnp.concatenate([kc_ref[...], kp_ref[...], kcur_ref[...], kn_ref[...]], axis=0)
    vall = jnp.concatenate([vc_ref[...], vp_ref[...], vcur_ref[...], vn_ref[...]], axis=0)
    nk = kall.shape[0]
    lane = lax.broadcasted_iota(jnp.int32, kall.shape, 1)
    lo = lane < SWA_HEAD_DIM
    kroll = pltpu.roll(kall, SWA_HEAD_DIM, axis=1)
    vroll = pltpu.roll(vall, SWA_HEAD_DIM, axis=1)
    k2 = [jnp.where(lo, kall, kroll).astype(BF16), jnp.where(lo, kroll, kall).astype(BF16)]
    v2 = [jnp.where(lo, vall, vroll).astype(BF16), jnp.where(lo, vroll, vall).astype(BF16)]
    rep = SWA_HEADS // SWA_KV_HEADS
    qi = lax.broadcasted_iota(jnp.int32, (rep * w, nk), 0) % w
    kj = lax.broadcasted_iota(jnp.int32, (rep * w, nk), 1)
    qpos = n * w + qi
    kpos = (n - 1) * w + (kj - nkc)
    loc_ok = (jnp.abs(qpos - kpos) <= WINDOW) & (kpos >= 0) & (kpos < nlat_blocks * w) & is_lat
    valid = (kj < nkc) | ((kj >= nkc) & loc_ok)
    qlane = lax.broadcasted_iota(jnp.int32, (w, 128), 1)
    qlo = qlane < SWA_HEAD_DIM
    outs = []
    for g in range(SWA_KV_HEADS):
        qs, sinks = [], []
        for r in range(rep):
            pr, half = divmod(g * rep + r, 2)
            qp = q_ref[:, 128 * pr:128 * (pr + 1)]
            qs.append(jnp.where(qlo if half == 0 else ~qlo, qp, 0.0).astype(BF16))
            sinks.append(jnp.broadcast_to(sink_ref[g * rep + r:g * rep + r + 1, 0:1], (w, 1)))
        sink = jnp.concatenate(sinks, axis=0)
        s = lax.dot_general(jnp.concatenate(qs, axis=0), k2[g], (((1,), (1,)), ((), ())),
                            preferred_element_type=F32)
        s = jnp.where(valid, s, NEG)
        mx = jnp.maximum(jnp.max(s, axis=-1, keepdims=True), sink)
        p = jnp.exp(s - mx)
        den = jnp.sum(p, axis=-1, keepdims=True) + jnp.exp(sink - mx)
        o = jnp.dot(p.astype(BF16), v2[g], preferred_element_type=F32) / den
        outs += [o[r * w:(r + 1) * w, :] for r in range(rep)]
    for pr in range(SWA_HEADS // 2):
        o_ref[:, 128 * pr:128 * (pr + 1)] = jnp.where(qlo, outs[2 * pr], outs[2 * pr + 1])


def _swa(sq, sk, sv, sink, nbatch, nctx_blocks, nlat_blocks):
    m = sq.shape[0]
    w = WINDOW
    nblk = nctx_blocks + nlat_blocks
    nkc = nctx_blocks * w

    def qmap(b, j):
        return (b * nblk + j, 0)

    def nmap(off):
        def f(b, j):
            n = jnp.clip(j - nctx_blocks + off, 0, nlat_blocks - 1)
            return (b * nblk + nctx_blocks + n, 0)
        return f

    kvw = sk.shape[1]
    kcs = pl.BlockSpec((nkc, kvw), lambda b, j: (b * (nblk * w // nkc), 0))
    kvs = lambda off: pl.BlockSpec((w, kvw), nmap(off))
    return pl.pallas_call(
        functools.partial(_swa_kernel, nctx_blocks=nctx_blocks, nlat_blocks=nlat_blocks),
        grid=(nbatch, nblk),
        in_specs=[pl.BlockSpec((w, sq.shape[1]), qmap), kcs, kcs, kvs(-1), kvs(0), kvs(1),
                  kvs(-1), kvs(0), kvs(1), _full(sink.shape)],
        out_specs=pl.BlockSpec((w, sq.shape[1]), qmap),
        out_shape=jax.ShapeDtypeStruct(sq.shape, F32),
        compiler_params=_cparams(("arbitrary", "arbitrary")),
        name="swa_attn",
    )(sq, sk, sv, sk, sk, sk, sv, sv, sv, sink)


def _residual_epilogue(x, mix, mod_ref, n2_ref, xo_ref, f_ref, ft_ref, d):
    xn = x + _mod_rows(mod_ref, 2, d) * mix
    xo_ref[...] = xn
    f = _rms(xn) * n2_ref[...] * (1.0 + _mod_rows(mod_ref, 4, d)) + _mod_rows(mod_ref, 3, d)
    f_ref[...] = f
    ft_ref[...] = f.T.astype(BF16)


def _out0_kernel(x_ref, of_ref, ob_ref, gg_ref, a_ref, mod_ref, gn_ref, wo_ref, n2_ref,
                 xo_ref, f_ref, ft_ref, *, d):
    o = of_ref[...] + ob_ref[...]
    gate = gg_ref[...]
    gate = gate * _sigmoid(gate)
    parts = []
    for h in range(GLA_HEADS):
        oh = o[:, h * GLA_DV:(h + 1) * GLA_DV]
        parts.append(oh * lax.rsqrt(jnp.mean(oh * oh, axis=-1, keepdims=True) + NORM_EPS))
    on = jnp.concatenate(parts, axis=1) * gn_ref[...] * gate
    hv = GLA_HEADS * GLA_DV
    mix = _bdot(on, wo_ref[0:hv, :]) + _bdot(a_ref[...], wo_ref[hv:, :])
    _residual_epilogue(x_ref[...], mix, mod_ref, n2_ref, xo_ref, f_ref, ft_ref, d)


def _tok_spec(n):
    return pl.BlockSpec((TM, n), lambda i: (i, 0))


def _epilogue_specs(m, d):
    out_specs = [_tok_spec(d), _tok_spec(d), pl.BlockSpec((d, TM), lambda i: (0, i))]
    out_shape = [jax.ShapeDtypeStruct((m, d), F32), jax.ShapeDtypeStruct((m, d), F32),
                 jax.ShapeDtypeStruct((d, m), BF16)]
    return out_specs, out_shape


def _out0(xs, of, ob, gg, a, mod, gn, wo, n2, nb, nbatch):
    m, d = xs.shape
    out_specs, out_shape = _epilogue_specs(m, d)
    return pl.pallas_call(
        functools.partial(_out0_kernel, d=d),
        grid=(m // TM,),
        in_specs=[_tok_spec(d), _tok_spec(of.shape[1]), _tok_spec(ob.shape[1]), _tok_spec(gg.shape[1]),
                  _tok_spec(a.shape[1]), _mod_spec(mod, nb, nbatch), _full(gn.shape), _full(wo.shape),
                  _full(n2.shape)],
        out_specs=out_specs, out_shape=out_shape,
        compiler_params=_cparams(("arbitrary",), VMEM_LIMIT),
        name="out0",
    )(xs, of, ob, gg, a, mod, gn, wo, n2)


NO_RANK = 99.0
RANK_STEP = 1e27


def _top_vals(s, k, want_rank=False):
    lw = 128
    ng = s.shape[1] // lw
    rows = lax.broadcasted_iota(jnp.int32, (k, lw), 0)
    cur = [s[:, q * lw:(q + 1) * lw] for q in range(ng)]
    vals = [jnp.zeros((k, lw), F32) for _ in range(ng)]
    for t in range(k):
        for q in range(ng):
            mx = jnp.max(cur[q], axis=0, keepdims=True)
            vals[q] = jnp.where(rows == t, mx, vals[q])
            cur[q] = jnp.where(cur[q] == mx, NEG - t * RANK_STEP, cur[q])
    vals = jnp.concatenate(vals, axis=1)
    if not want_rank:
        return vals
    cur = jnp.concatenate(cur, axis=1)
    rank = jnp.where(cur < 0.5 * NEG, jnp.floor((NEG - cur) * (1.0 / RANK_STEP) + 0.5), NO_RANK)
    return vals, rank


def _peer_scores_kernel(f_ref, wh_ref, wl_ref, kh_ref, kl_ref, s1_ref, s2_ref):
    f = f_ref[...]
    fh = f.astype(BF16)
    fl = (f - fh.astype(F32)).astype(BF16)
    wh = wh_ref[...]
    q = (jnp.dot(fh, wh, preferred_element_type=F32) + jnp.dot(fl, wh, preferred_element_type=F32)
         + jnp.dot(fh, wl_ref[...], preferred_element_type=F32))
    qh = q.astype(BF16)
    ql = (q - qh.astype(F32)).astype(BF16)
    nt = (((1,), (1,)), ((), ()))
    for h in range(PEER_HEADS):
        for z, s_ref in enumerate((s1_ref, s2_ref)):
            c0 = (2 * h + z) * PEER_NKEYS
            kh = kh_ref[z, h]
            qhs = qh[:, c0:c0 + PEER_NKEYS]
            s_ref[h] = (lax.dot_general(kh, qhs, nt, preferred_element_type=F32)
                        + lax.dot_general(kh, ql[:, c0:c0 + PEER_NKEYS], nt, preferred_element_type=F32)
                        + lax.dot_general(kl_ref[z, h], qhs, nt, preferred_element_type=F32))


def _peer_q_kernel(s1_ref, s2_ref, r2_ref, cnt_ref, e1_ref, e2_ref):
    kk = PEER_TOPK
    for h in range(PEER_HEADS):
        s1 = s1_ref[h]
        s2 = s2_ref[h]
        v1 = _top_vals(s1, kk)
        v2, rank2 = _top_vals(s2, kk, want_rank=True)
        row8 = lax.broadcasted_iota(jnp.int32, (8, s1.shape[1]), 0)
        blocks = [v1[0:1, :] + v2]
        for r in range(1, 8):
            blocks.append(jnp.where(row8 < kk // (r + 1), v1[r:r + 1, :] + v2[0:8, :], NEG))
        blocks.append(v1[8:kk, :] + v2[0:1, :])
        top = _top_vals(jnp.concatenate(blocks, axis=0), kk)
        tau = top[kk - 1:kk, :]
        z = jnp.sum(jnp.exp(top - top[0:1, :]), axis=0, keepdims=True)
        for q in range(s1.shape[1] // 128):
            ln = slice(128 * q, 128 * (q + 1))
            v2b = [jnp.broadcast_to(v2[c:c + 1, ln], (8, 128)) for c in range(kk)]
            taub = jnp.broadcast_to(tau[:, ln], (8, 128))
            for rb in range(PEER_NKEYS // 8):
                x = s1[8 * rb:8 * (rb + 1), ln]
                cnt = jnp.zeros((8, 128), F32)
                for c in range(kk):
                    cnt = cnt + jnp.where(x + v2b[c] >= taub, 1.0, 0.0)
                cnt_ref[h, 8 * rb:8 * (rb + 1), ln] = cnt
        r2_ref[h] = rank2.astype(BF16)
        e1_ref[h] = jnp.exp(s1 - v1[0:1, :])
        e2_ref[h] = (jnp.exp(s2 - v2[0:1, :]) / z).astype(BF16)


def _peer_fused_kernel(f_ref, wh_ref, wl_ref, kh_ref, kl_ref, r2_ref, cnt_ref, e1_ref, e2_ref,
                       s1_a, s2_a, s1_b, s2_b):
    i = pl.program_id(0)

    def init():
        s1_b[...] = jnp.zeros_like(s1_b)
        s2_b[...] = jnp.zeros_like(s2_b)

    _run_if(i == 0, init)

    def step(s1_w, s2_w, s1_r, s2_r):
        _peer_scores_kernel(f_ref, wh_ref, wl_ref, kh_ref, kl_ref, s1_w, s2_w)
        _peer_q_kernel(s1_r, s2_r, r2_ref, cnt_ref, e1_ref, e2_ref)

    _run_if(i % 2 == 0, lambda: step(s1_a, s2_a, s1_b, s2_b))
    _run_if(i % 2 == 1, lambda: step(s1_b, s2_b, s1_a, s2_a))


def _peer_q(f, wh, wl, k1, k2):
    m, d = f.shape
    hh, nk = PEER_HEADS, PEER_NKEYS
    big = pl.BlockSpec((hh, nk, TM), lambda i: (0, 0, i))
    sh32 = jax.ShapeDtypeStruct((hh, nk, m), F32)
    sh16 = jax.ShapeDtypeStruct((hh, nk, m), BF16)
    kf = jnp.stack([k1, k2])
    kh = kf.astype(BF16)
    kl = (kf - kh.astype(F32)).astype(BF16)
    n = m // TM
    late = pl.BlockSpec((hh, nk, TM), lambda i: (0, 0, jnp.maximum(i - 1, 0)))
    sscr = pltpu.VMEM((hh, nk, TM), F32)
    return pl.pallas_call(
        _peer_fused_kernel,
        grid=(n + 1,),
        in_specs=[pl.BlockSpec((TM, d), lambda i: (jnp.minimum(i, n - 1), 0)),
                  _full(wh.shape), _full(wl.shape), _full(kh.shape), _full(kl.shape)],
        out_specs=[late, late, late, late],
        out_shape=[sh16, sh32, sh32, sh16],
        scratch_shapes=[sscr, sscr, sscr, sscr],
        compiler_params=_cparams(("arbitrary",), VMEM_LIMIT),
        name="peer_query",
    )(f, wh, wl, kh, kl)


PD_TM = 512
PD_TE = 2048
PD_SUBS = (1024, 1024)


def _gelu(x):
    return 0.5 * x * (1.0 + lax.erf(x * (2.0 ** -0.5)))


def _row_bf16(tile, r, rows):
    one = jnp.broadcast_to(tile[r:r + 1, :], (8, tile.shape[1]))
    one = jnp.concatenate([one, one], axis=0).astype(BF16)
    return jnp.concatenate([one] * (rows // 16), axis=0)


def _run_if(cond, fn):
    def body(_, carry):
        fn()
        return carry
    lax.fori_loop(0, cond.astype(jnp.int32), body, 0)


def _peer_dense_kernel(ft_ref, u_ref, vt_ref, r2_ref, cnt_ref, e1_ref, e2_ref, o_ref, acc_ref):
    k = pl.program_id(1)
    nk = PEER_NKEYS
    nslab = PD_TE // nk
    zero = jnp.zeros((nk, PD_TM), BF16)
    nsub = len(PD_SUBS)
    starts = [sum(PD_SUBS[:i]) for i in range(nsub + 1)]
    assert starts[-1] == PD_TE

    def first():
        acc_ref[...] = jnp.zeros_like(acc_ref)

    _run_if(k == 0, first)

    def hidden(sb):
        return jnp.dot(u_ref[starts[sb]:starts[sb + 1], :], ft_ref[...], preferred_element_type=F32)

    ht_next = hidden(0)
    for sb in range(nsub):
        e0, e1x = starts[sb], starts[sb + 1]
        ht = ht_next
        if sb + 1 < nsub:
            ht_next = hidden(sb + 1)
        gates = []
        for al in range(e0 // nk, e1x // nk):
            wg = zero
            a0 = pl.multiple_of(k * nslab + 8 * (al // 8), 8)
            for h in range(PEER_HEADS):
                cnt = _row_bf16(cnt_ref[h, pl.ds(a0, 8), :], al % 8, nk)
                e1 = _row_bf16(e1_ref[h, pl.ds(a0, 8), :], al % 8, nk)
                wg = wg + jnp.where(r2_ref[h] < cnt, e2_ref[h], zero) * e1
            gates.append(wg)
        ct = _gelu(ht).astype(BF16) * jnp.concatenate(gates, axis=0)
        acc_ref[...] += jnp.dot(vt_ref[:, e0:e1x], ct, preferred_element_type=F32)

    def last():
        o_ref[...] = acc_ref[...].T

    _run_if(k == pl.num_programs(1) - 1, last)


def _peer_dense(ft, u, vt, layer, r2, cnt, e1, e2):
    d, m = ft.shape
    ne = u.shape[1]
    hh, nk = PEER_HEADS, PEER_NKEYS
    big = pl.BlockSpec((hh, nk, PD_TM), lambda i, k: (0, 0, i))
    return pl.pallas_call(
        _peer_dense_kernel,
        grid=(m // PD_TM, ne // PD_TE),
        in_specs=[pl.BlockSpec((d, PD_TM), lambda i, k: (0, i)),
                  pl.BlockSpec((None, PD_TE, d), lambda i, k: (layer, k, 0)),
                  pl.BlockSpec((None, None, d, PD_TE), lambda i, k: (layer, k, 0, 0)),
                  big, big, big, big],
        out_specs=pl.BlockSpec((PD_TM, d), lambda i, k: (i, 0)),
        out_shape=jax.ShapeDtypeStruct((m, d), F32),
        scratch_shapes=[pltpu.VMEM((d, PD_TM), F32)],
        compiler_params=_cparams(("arbitrary", "arbitrary"), VMEM_LIMIT),
        name="peer_dense",
    )(ft, u, vt, r2, cnt, e1, e2)


def _res_kernel(x_ref, p_ref, mod_ref, modn_ref, g_ref, xo_ref, u_ref, *, d):
    xn = x_ref[...] + _mod_rows(mod_ref, 5, d) * p_ref[...]
    xo_ref[...] = xn
    u_ref[...] = _rms(xn) * g_ref[...] * (1.0 + _mod_rows(modn_ref, 1, d)) + _mod_rows(modn_ref, 0, d)


def _res(xs, p, mod, modn, g, nb, nbatch):
    m, d = xs.shape
    return pl.pallas_call(
        functools.partial(_res_kernel, d=d),
        grid=(m // TM,),
        in_specs=[_tok_spec(d), _tok_spec(d), _mod_spec(mod, nb, nbatch), _mod_spec(modn, nb, nbatch),
                  _full(g.shape)],
        out_specs=[_tok_spec(d), _tok_spec(d)],
        out_shape=[jax.ShapeDtypeStruct((m, d), F32)] * 2,
        compiler_params=_cparams(("arbitrary",)),
        name="peer_residual",
    )(xs, p, mod, modn, g)


def _final_kernel(x_ref, p_ref, mod_ref, g_ref, y_ref, *, d):
    y_ref[...] = _rms(x_ref[...] + _mod_rows(mod_ref, 5, d) * p_ref[...]) * g_ref[...]


def _final(xs, p, mod, g, nb):
    m, d = xs.shape
    return pl.pallas_call(
        functools.partial(_final_kernel, d=d),
        grid=(m // TM,),
        in_specs=[_tok_spec(d), _tok_spec(d), _mod_spec_lat(mod, nb - 1), _full(g.shape)],
        out_specs=_tok_spec(d),
        out_shape=jax.ShapeDtypeStruct((m, d), F32),
        compiler_params=_cparams(("arbitrary",)),
        name="final_norm",
    )(xs, p, mod, g)


def _split_dot(x, w):
    hi = x.astype(BF16)
    lo = (x - hi.astype(F32)).astype(BF16)
    return jnp.dot(hi, w, preferred_element_type=F32) + jnp.dot(lo, w, preferred_element_type=F32)


def _seg_sum(x, g_ref, gt_ref):
    return _split_dot(_split_dot(x, g_ref[...]), gt_ref[...])


def _rwkv_prep_kernel(u_ref, up_ref, un_ref, mu_ref, wr_ref, wk_ref, wv_ref, w0_ref, w1_ref, w2_ref,
                      a0_ref, a1_ref, a2_ref, g1_ref, g2_ref, kk_ref_, ka_ref, rk_ref, sg_ref, sgt_ref,
                      wf_o, wb_o, kf_o, kb_o, bf_o, bb_o, r_o, v_o, kk_o, g_o, bonus_o, *, nb, d):
    i = pl.program_id(0)
    p = i % nb
    u = u_ref[...]
    has_prev = (p >= 2).astype(F32)
    has_next = ((p >= 1) & (p <= nb - 2)).astype(F32)
    prev_row = up_ref[7:8, :] * has_prev
    next_row = un_ref[0:1, :] * has_next
    rows = lax.broadcasted_iota(jnp.int32, u.shape, 0)
    up = jnp.where(rows == 0, prev_row, pltpu.roll(u, 1, axis=0))
    un = jnp.where(rows == TM - 1, next_row, pltpu.roll(u, TM - 1, axis=0))
    xx = 0.5 * (up + un) - u

    def mix(zi):
        return (u + xx * mu_ref[zi:zi + 1, :]).astype(BF16)

    r = jnp.dot(mix(0), wr_ref[...], preferred_element_type=F32)
    k = jnp.dot(mix(2), wk_ref[...], preferred_element_type=F32)
    v = jnp.dot(mix(3), wv_ref[...], preferred_element_type=F32)
    lw = w0_ref[...] + _bdot(jnp.tanh(jnp.dot(mix(1), w1_ref[...], preferred_element_type=F32)), w2_ref[...])
    decay = jnp.exp(-_sigmoid(lw) * float(np.exp(-0.5)))
    a = _sigmoid(a0_ref[...] + _bdot(jnp.dot(mix(4), a1_ref[...], preferred_element_type=F32), a2_ref[...]))
    g = _bdot(_sigmoid(jnp.dot(mix(5), g1_ref[...], preferred_element_type=F32)), g2_ref[...])
    kk = k * kk_ref_[...]
    kk = kk * lax.rsqrt(jnp.maximum(_seg_sum(kk * kk, sg_ref, sgt_ref), 1e-24))
    ka = ka_ref[...]
    kf = k * (1.0 + (a[:, :d] - 1.0) * ka)
    kb = k * (1.0 + (a[:, d:] - 1.0) * ka)
    bonus = _seg_sum(r * (kf + kb) * rk_ref[...], sg_ref, sgt_ref) * v
    wf_o[...] = decay[:, :d]
    wb_o[...] = decay[:, d:]
    kf_o[...] = kf
    kb_o[...] = kb
    bf_o[...] = kk * a[:, :d]
    bb_o[...] = kk * a[:, d:]
    r_o[...] = r
    v_o[...] = v
    kk_o[...] = kk
    g_o[...] = g
    bonus_o[...] = bonus


def _rwkv_prep(u, consts, nb):
    m, d = u.shape
    r8 = TM // 8
    nb8 = m // 8
    in_specs = [_tok_spec(d),
                pl.BlockSpec((8, d), lambda i: (jnp.maximum(i * r8 - 1, 0), 0)),
                pl.BlockSpec((8, d), lambda i: (jnp.minimum((i + 1) * r8, nb8 - 1), 0))]
    in_specs += [_full(c.shape) for c in consts]
    return pl.pallas_call(
        functools.partial(_rwkv_prep_kernel, nb=nb, d=d),
        grid=(m // TM,),
        in_specs=in_specs,
        out_specs=[_tok_spec(d)] * 11,
        out_shape=[jax.ShapeDtypeStruct((m, d), F32)] * 11,
        compiler_params=_cparams(("arbitrary",), VMEM_LIMIT),
        name="rwkv_prep",
    )(u, u, u, *consts)


RW_TC = 256
RW_UNROLL = 8


def _rwkv_scan_kernel(wf, kf, bf, rf, vf, kkf, wb, kb, bb, rb, vb, kkb, yf_ref, yb_ref,
                      st, vk_a, vk_b, ycol_scr, sa_scr, *, npair):
    s = pl.program_id(1)
    n = RW_HEAD_DIM
    grp = 8

    @pl.when(s == 0)
    def _():
        st[...] = jnp.zeros_like(st)
        ycol_scr[...] = jnp.zeros_like(ycol_scr)

    w2 = 2 * n
    ri = lax.broadcasted_iota(jnp.int32, (n, w2), 0)
    li = lax.broadcasted_iota(jnp.int32, (n, w2), 1)
    eye_a = (li == ri).astype(BF16)
    bi = lax.broadcasted_iota(jnp.int32, (w2, w2), 0)
    bj = lax.broadcasted_iota(jnp.int32, (w2, w2), 1)
    ones_blk = ((bi // n) == (bj // n)).astype(BF16)
    b4i = lax.broadcasted_iota(jnp.int32, (2 * w2, 2 * w2), 0)
    b4j = lax.broadcasted_iota(jnp.int32, (2 * w2, 2 * w2), 1)
    ones_blk2 = ((b4i // n) == (b4j // n)).astype(BF16)
    lane8 = lax.broadcasted_iota(jnp.int32, (grp, w2), 1)
    rows16 = lax.broadcasted_iota(jnp.int32, (2 * grp, w2), 0)
    lane16 = lax.broadcasted_iota(jnp.int32, (2 * grp, w2), 1)
    untr = ((lane16 == rows16) | (lane16 == rows16 + n)).astype(BF16)
    dirs = ((wf, kf, bf, rf, vf, kkf, yf_ref), (wb, kb, bb, rb, vb, kkb, yb_ref))
    nch = 2 * npair
    ngrp = RW_TC // grp
    nt = (((1,), (1,)), ((), ()))

    def halves(tile):
        return jnp.concatenate([jnp.where(lane8 < n, tile, 0.0), jnp.where(lane8 >= n, tile, 0.0)], axis=0)

    assert nch == 2 * grp

    def group_base(g, dd):
        base = (g if dd == 0 else ngrp - 1 - g) * grp
        return base if isinstance(base, int) else pl.multiple_of(base, grp)

    def lanes(c):
        p = c % npair
        return slice(w2 * p, w2 * (p + 1))

    def precompute(g, chains, vk_dst):
        gq = jnp.minimum(g, ngrp - 1)
        v2, ks = [], []
        for c in chains:
            dd = c // npair
            base = group_base(gq, dd)
            v_t = dirs[dd][4][pl.ds(base, grp), lanes(c)]
            ks.append(dirs[dd][1][pl.ds(base, grp), lanes(c)])
            v2 += [jnp.where(lane8 < n, v_t, 0.0), jnp.where(lane8 < n, pltpu.roll(v_t, n, axis=1), 0.0)]
        vt = lax.dot_general(eye_a, jnp.concatenate(v2, axis=0).astype(BF16), nt,
                             preferred_element_type=F32).astype(BF16)
        for idx, c in enumerate(chains):
            k2 = halves(ks[idx])
            kblk = jnp.concatenate(
                [jnp.where((rows16 == j) | (rows16 == grp + j), k2, 0.0) for j in range(grp)], axis=1)
            vk_dst[c] = jnp.dot(vt[:, 2 * grp * idx:2 * grp * (idx + 1)], kblk.astype(BF16),
                                preferred_element_type=F32)

    def untranspose(g, chains, ycol_prev):
        for c in chains:
            dd = c // npair
            yc = ycol_prev[c]
            ycs = jnp.concatenate([jnp.where(li < n, yc, 0.0), jnp.where(li >= n, yc, 0.0)], axis=0)
            yt = lax.dot_general(untr, ycs.astype(BF16), nt, preferred_element_type=F32)
            dirs[dd][6][pl.ds(group_base(g, dd), grp), lanes(c)] = yt[0:grp, :]

    def run_group(g, vk_cur, vk_next, states, sa, ycol_prev):
        tiles = []
        for c in range(nch):
            dd = c // npair
            base = group_base(g, dd)
            tiles.append([dirs[dd][q][pl.ds(base, grp), lanes(c)] for q in (0, 2, 3, 5)])
        gprev = jnp.maximum(g - 1, 0)
        gnext = jnp.minimum(g + 1, ngrp - 1)
        kap_next = []
        for c in range(nch):
            dd = c // npair
            kap_next.append(dirs[dd][5][pl.ds(group_base(gnext, dd), grp), lanes(c)])
        states, sa = list(states), list(sa)
        ycol = [jnp.zeros((n, w2), F32) for _ in range(nch)]
        for j in range(grp):
            lhs = []
            for c in range(nch):
                w_t, b_t, r_t, kk_t = tiles[c]
                fwd = c < npair
                jj = j if fwd else grp - 1 - j
                row = lambda tl, q: jnp.broadcast_to(tl[q:q + 1, :], (n, w2))
                states[c] = (states[c] * row(w_t, jj) - sa[c] * row(b_t, jj)
                             + vk_cur[c, :, jj * w2:(jj + 1) * w2])
                if j + 1 < grp:
                    kap = row(kk_t, jj + 1 if fwd else jj - 1)
                else:
                    kap = row(kap_next[c], 0 if fwd else grp - 1)
                lhs.append(jnp.concatenate([(states[c] * kap).astype(BF16),
                                            (states[c] * row(r_t, jj)).astype(BF16)], axis=1))
            red = jnp.dot(jnp.concatenate(lhs, axis=0), ones_blk2, preferred_element_type=F32)
            for c in range(nch):
                jj = j if c < npair else grp - 1 - j
                sa[c] = red[c * n:(c + 1) * n, :w2]
                ycol[c] = jnp.where((li % n) == jj, red[c * n:(c + 1) * n, w2:], ycol[c])
            pair = (2 * j, 2 * j + 1)
            precompute(g + 1, pair, vk_next)
            untranspose(gprev, pair, ycol_prev)
        return states, sa, ycol

    precompute(0, tuple(range(nch)), vk_a)
    first = []
    for c in range(nch):
        dd = c // npair
        kk_t = dirs[dd][5][pl.ds(group_base(0, dd), grp), lanes(c)]
        q0 = 0 if dd == 0 else grp - 1
        first.append((st[c] * jnp.broadcast_to(kk_t[q0:q0 + 1, :], (n, w2))).astype(BF16))
    sa0 = jnp.dot(jnp.concatenate(first, axis=0), ones_blk, preferred_element_type=F32)
    for c in range(nch):
        sa_scr[c] = sa0[c * n:(c + 1) * n, :]

    def two_groups(i, carry):
        states = [st[c] for c in range(nch)]
        sa = [sa_scr[c] for c in range(nch)]
        ycol = [ycol_scr[c] for c in range(nch)]
        for u in range(0, RW_UNROLL, 2):
            states, sa, ycol = run_group(RW_UNROLL * i + u, vk_a, vk_b, states, sa, ycol)
            states, sa, ycol = run_group(RW_UNROLL * i + u + 1, vk_b, vk_a, states, sa, ycol)
        for c in range(nch):
            st[c] = states[c]
            sa_scr[c] = sa[c]
            ycol_scr[c] = ycol[c]
        return carry

    lax.fori_loop(0, ngrp // RW_UNROLL, two_groups, 0)
    untranspose(ngrp - 1, tuple(range(nch)), [ycol_scr[c] for c in range(nch)])


def _rwkv_scan(wf, kf, bf, wb, kb, bb, r, v, kk, nbatch):
    m, d = r.shape
    nch = m // nbatch // RW_TC
    npair = d // (2 * RW_HEAD_DIM)

    def fi(b, s):
        return (b * nch + s, 0)

    def bi(b, s):
        return (b * nch + jnp.where(s == 0, 0, nch - s), 0)

    sf = pl.BlockSpec((RW_TC, d), fi)
    sb = pl.BlockSpec((RW_TC, d), bi)
    return pl.pallas_call(
        functools.partial(_rwkv_scan_kernel, npair=npair),
        grid=(nbatch, nch),
        in_specs=[sf] * 6 + [sb] * 6,
        out_specs=[sf, sb],
        out_shape=[jax.ShapeDtypeStruct((m, d), F32)] * 2,
        scratch_shapes=[pltpu.VMEM((2 * npair, RW_HEAD_DIM, 2 * RW_HEAD_DIM), F32),
                        pltpu.VMEM((2 * npair, RW_HEAD_DIM, 8 * 2 * RW_HEAD_DIM), F32),
                        pltpu.VMEM((2 * npair, RW_HEAD_DIM, 8 * 2 * RW_HEAD_DIM), F32),
                        pltpu.VMEM((2 * npair, RW_HEAD_DIM, 2 * RW_HEAD_DIM), F32),
                        pltpu.VMEM((2 * npair, RW_HEAD_DIM, 2 * RW_HEAD_DIM), F32)],
        compiler_params=_cparams(("arbitrary", "arbitrary"), VMEM_LIMIT),
        name="rwkv_scan",
    )(wf, kf, bf, r, v, kk, wb, kb, bb, r, v, kk)


def _rwkv_out_kernel(x_ref, yf_ref, yb_ref, bonus_ref, g_ref, mod_ref, lg_ref, lb_ref, wo_ref, n2_ref,
                     sg_ref, sgt_ref, xo_ref, f_ref, ft_ref, *, d):
    y = yf_ref[...] + yb_ref[...]
    inv = 1.0 / RW_HEAD_DIM
    mean = _seg_sum(y, sg_ref, sgt_ref) * inv
    yc = y - mean
    var = _seg_sum(yc * yc, sg_ref, sgt_ref) * inv
    yn = yc * lax.rsqrt(var + RW_GN_EPS) * lg_ref[...] + lb_ref[...]
    out = (yn + bonus_ref[...]) * g_ref[...]
    mix = _bdot(out, wo_ref[...])
    _residual_epilogue(x_ref[...], mix, mod_ref, n2_ref, xo_ref, f_ref, ft_ref, d)


def _rwkv_out(xs, yf, yb, bonus, g, mod, lg, lb, wo, n2, sg, sgt, nb, nbatch):
    d = xs.shape[1]
    m = nbatch * (nb - 1) * TM
    out_specs, out_shape = _epilogue_specs(m, d)
    return pl.pallas_call(
        functools.partial(_rwkv_out_kernel, d=d),
        grid=(m // TM,),
        in_specs=[_lat_tok_spec(d, nb)] * 5 + [_mod_spec_lat(mod, nb - 1)]
        + [_full(a.shape) for a in (lg, lb, wo, n2, sg, sgt)],
        out_specs=out_specs, out_shape=out_shape,
        compiler_params=_cparams(("arbitrary",), VMEM_LIMIT),
        name="rwkv_out",
    )(xs, yf, yb, bonus, g, mod, lg, lb, wo, n2, sg, sgt)


def _rope_tables(t_lat, n_ctx):
    rows = t_lat // GRID_W
    row = jnp.broadcast_to(jnp.arange(rows, dtype=F32)[:, None], (rows, GRID_W)).reshape(-1)
    col = jnp.broadcast_to(jnp.arange(GRID_W, dtype=F32)[None, :], (rows, GRID_W)).reshape(-1)
    n_freq = SWA_HEAD_DIM // 4
    inv_freq = ROPE_BASE ** (-jnp.arange(n_freq, dtype=F32) / n_freq)
    ar = row[:, None] * inv_freq
    ac = col[:, None] * inv_freq
    cos = jnp.concatenate([jnp.cos(ar), jnp.cos(ar), jnp.cos(ac), jnp.cos(ac)], axis=1)
    sin = jnp.concatenate([-jnp.sin(ar), jnp.sin(ar), -jnp.sin(ac), jnp.sin(ac)], axis=1)
    cos = jnp.concatenate([jnp.ones((n_ctx, SWA_HEAD_DIM), F32), cos], axis=0)
    sin = jnp.concatenate([jnp.zeros((n_ctx, SWA_HEAD_DIM), F32), sin], axis=0)
    return jnp.tile(cos, (1, 2)), jnp.tile(sin, (1, 2))


def _peer(f, ft, w_q, k1, k2, u_all, vt_all, layer):
    wh = w_q.astype(BF16)
    wl = (w_q - wh.astype(F32)).astype(BF16)
    r2, cnt, e1, e2 = _peer_q(f, wh, wl, k1, k2)
    return _peer_dense(ft, u_all, vt_all, layer, r2, cnt, e1, e2)


def kernel(x, c, ctx, c_ctx, ada_w, ada_b, norm1_g, norm2_g, ab_w_in, gla_dec_w2, gla_dec_b, gla_norm_g,
           swa_sink, ab_w_out, rw_mu, rw_w_rkv, rw_w_o, rw_w0, rw_w1, rw_w2, rw_a0, rw_a1, rw_a2, rw_g1,
           rw_g2, rw_k_k, rw_k_a, rw_r_k, rw_ln_g, rw_ln_b, peer_w_q, peer_k1, peer_k2, peer_u, peer_v,
           final_g):
    nbatch, t_lat, d = x.shape
    n_ctx = ctx.shape[1]
    assert n_ctx == TM and t_lat % TM == 0 and d == 1024
    s_tok = n_ctx + t_lat
    nb = s_tok // TM
    m = nbatch * s_tok
    xs = jnp.concatenate([ctx, x], axis=1).reshape(m, d)

    cc = jnp.zeros((16, d), F32).at[:nbatch].set(c).at[nbatch].set(c_ctx)
    mod0 = _ada_table(cc, ada_w[0], ada_b[0]).reshape(16, 1, 6 * d)
    mod1 = _ada_table(cc, ada_w[1], ada_b[1]).reshape(16, 1, 6 * d)
    row2 = lambda a: a.reshape(1, -1)

    w_in = ab_w_in[0]
    cuts = np.cumsum((256, 256, 512, 512, 32, 512, 128, 128))[:-1].tolist()
    wgq, wgk, wgv, wgg, wlr, wsq, wsk, wsv = jnp.split(w_in, cuts, axis=1)
    wp = jnp.concatenate([wgq, wgk, wgv, wgg, wsq, wsk, wsv, wlr,
                          jnp.zeros((d, _P_END - _P_LR - 32), F32)], axis=1).astype(BF16)
    hk = GLA_HEADS * GLA_DK
    w2p = jnp.zeros((128, 2 * hk), F32)
    w2p = w2p.at[0:GLA_LOWRANK, 0:hk].set(gla_dec_w2[0, 0]).at[GLA_LOWRANK:2 * GLA_LOWRANK, hk:].set(gla_dec_w2[0, 1])
    db = gla_dec_b[0].reshape(1, 2 * hk)
    cos, sin = _rope_tables(t_lat, n_ctx)
    gq, gk, gv, gg, laf, lab, sq, sk, sv = _proj0(xs, mod0, row2(norm1_g[0]), wp, w2p, db, cos, sin, nb, nbatch)
    of, ob = _gla(gq, gk, gv, laf, lab, nbatch, n_ctx // GLA_CHUNK)
    sink = jnp.broadcast_to(swa_sink[0][:, None], (SWA_HEADS, 128))
    att = _swa(sq, sk, sv, sink, nbatch, n_ctx // WINDOW, t_lat // WINDOW)
    xs, f, ft = _out0(xs, of, ob, gg, att, mod0, row2(gla_norm_g[0]), ab_w_out[0].astype(BF16),
                      row2(norm2_g[0]), nb, nbatch)
    u_all = peer_u.astype(BF16)
    vt_all = jnp.swapaxes(peer_v.reshape(peer_v.shape[0], -1, PD_TE, d), 2, 3).astype(BF16)
    p = _peer(f, ft, peer_w_q[0], peer_k1[0], peer_k2[0], u_all, vt_all, 0)
    xs, u = _res(xs, p, mod0, mod1, row2(norm1_g[1]), nb, nbatch)

    lora = rw_w1.shape[-1]
    w1c = jnp.concatenate([rw_w1[0, 0], rw_w1[0, 1]], axis=1).astype(BF16)
    w2c = jnp.zeros((2 * lora, 2 * d), F32).at[:lora, :d].set(rw_w2[0, 0]).at[lora:, d:].set(rw_w2[0, 1]).astype(BF16)
    la = rw_a1.shape[-1]
    a1c = jnp.concatenate([rw_a1[0, 0], rw_a1[0, 1]], axis=1).astype(BF16)
    a2c = jnp.zeros((2 * la, 2 * d), F32).at[:la, :d].set(rw_a2[0, 0]).at[la:, d:].set(rw_a2[0, 1]).astype(BF16)
    lg = rw_g1.shape[-1]
    g1p = jnp.zeros((d, 256), F32).at[:, :lg].set(rw_g1[0]).astype(BF16)
    g2p = jnp.zeros((256, d), F32).at[:lg].set(rw_g2[0]).astype(BF16)
    sg = (jnp.arange(d)[:, None] // RW_HEAD_DIM == jnp.arange(128)[None, :]).astype(BF16)
    sgt = sg.T
    consts = [rw_mu[0], rw_w_rkv[0, 0].astype(BF16), rw_w_rkv[0, 1].astype(BF16), rw_w_rkv[0, 2].astype(BF16),
              rw_w0[0].reshape(1, 2 * d), w1c, w2c, rw_a0[0].reshape(1, 2 * d), a1c, a2c, g1p, g2p,
              row2(rw_k_k[0]), row2(rw_k_a[0]), rw_r_k[0].reshape(1, d), sg, sgt]
    mu8 = jnp.zeros((8, d), F32).at[:6].set(rw_mu[0])
    consts[0] = mu8
    wf, wb, kf, kb, bf, bb, r, v, kk, g, bonus = _rwkv_prep(u, consts, nb)
    yf, yb = _rwkv_scan(wf, kf, bf, wb, kb, bb, r, v, kk, nbatch)
    xs, f, ft = _rwkv_out(xs, yf, yb, bonus, g, mod1, row2(rw_ln_g[0]), row2(rw_ln_b[0]),
                          rw_w_o[0].astype(BF16), row2(norm2_g[1]), sg, sgt, nb, nbatch)
    p = _peer(f, ft, peer_w_q[1], peer_k1[1], peer_k2[1], u_all, vt_all, 1)
    y = _final(xs, p, mod1, row2(final_g), nb)
    return y.reshape(nbatch, t_lat, d)
```

```python
import functools

import numpy as np
import jax
import jax.numpy as jnp
from jax import lax
from jax.experimental import pallas as pl
from jax.experimental.pallas import tpu as pltpu

F32 = jnp.float32
BF16 = jnp.bfloat16
HI = lax.Precision.HIGHEST

NORM_EPS = 1e-6
GLA_HEADS, GLA_DK, GLA_DV, GLA_LOWRANK, GLA_TAU, GLA_CHUNK = 4, 64, 128, 16, 16.0, 64
SWA_HEADS, SWA_KV_HEADS, SWA_HEAD_DIM, WINDOW = 8, 2, 64, 128
ROPE_BASE = 10000.0
GRID_W = 64
RW_HEAD_DIM = 64
RW_GN_EPS = 64e-5
PEER_HEADS, PEER_NKEYS, PEER_TOPK = 8, 128, 16
NEG = -1e30

TM = 256
VMEM_LIMIT = 56 * 1024 * 1024


def _cparams(sem, vmem=None):
    return pltpu.CompilerParams(dimension_semantics=sem, vmem_limit_bytes=vmem)


def _bdot(a, b):
    return jnp.dot(a.astype(BF16), b.astype(BF16), preferred_element_type=F32)


def _hdot(a, b):
    return jnp.dot(a, b, precision=HI, preferred_element_type=F32)


def _sigmoid(x):
    return 1.0 / (1.0 + jnp.exp(-x))


def _rms(x):
    return x * lax.rsqrt(jnp.mean(x * x, axis=-1, keepdims=True) + NORM_EPS)


def _mod_rows(mod_ref, k, d):
    return mod_ref[0, :, k * d:(k + 1) * d]


def _mod_spec(mod, nb, nbatch):
    return pl.BlockSpec((1, 1, mod.shape[2]), lambda i: (jnp.where(i % nb == 0, nbatch, i // nb), 0, 0))


def _mod_spec_lat(mod, nbl):
    return pl.BlockSpec((1, 1, mod.shape[2]), lambda i: (i // nbl, 0, 0))


def _lat_tok_spec(n, nb):
    return pl.BlockSpec((TM, n), lambda i: ((i // (nb - 1)) * nb + 1 + i % (nb - 1), 0))


def _full(shape):
    n = len(shape)
    return pl.BlockSpec(shape, lambda *_: (0,) * n)


def _ada_kernel(c_ref, w_ref, b_ref, o_ref):
    c = c_ref[...]
    s = c * _sigmoid(c)
    o_ref[...] = _hdot(s, w_ref[...]) + b_ref[...]


def _ada_table(cc, w, b):
    rows, d = cc.shape
    n = w.shape[1]
    tn = 512
    return pl.pallas_call(
        _ada_kernel,
        grid=(n // tn,),
        in_specs=[_full((rows, d)), pl.BlockSpec((d, tn), lambda j: (0, j)),
                  pl.BlockSpec((1, tn), lambda j: (0, j))],
        out_specs=pl.BlockSpec((rows, tn), lambda j: (0, j)),
        out_shape=jax.ShapeDtypeStruct((rows, n), F32),
        compiler_params=_cparams(("arbitrary",)),
        name="ada_table",
    )(cc, w, b.reshape(1, n))


_P_GQ, _P_GK, _P_GV, _P_GG, _P_SQ, _P_SK, _P_SV, _P_LR, _P_END = 0, 256, 512, 1024, 1536, 2048, 2176, 2304, 2432


def _rope(x, cos, sin):
    lane = lax.broadcasted_iota(jnp.int32, x.shape, 1)
    up = pltpu.roll(x, 112, axis=1)
    dn = pltpu.roll(x, 16, axis=1)
    sw = jnp.where((lane % 32) < 16, up, dn)
    return x * cos + sw * sin


def _proj0_kernel(x_ref, mod_ref, g_ref, w_ref, w2_ref, db_ref, cos_ref, sin_ref,
                  gq_ref, gk_ref, gv_ref, gg_ref, laf_ref, lab_ref, sq_ref, sk_ref, sv_ref,
                  *, d):
    u = (_rms(x_ref[...]) * g_ref[...] * (1.0 + _mod_rows(mod_ref, 1, d))
         + _mod_rows(mod_ref, 0, d)).astype(BF16)

    def seg(a, b):
        return jnp.dot(u, w_ref[:, a:b], preferred_element_type=F32)

    gq_ref[...] = seg(_P_GQ, _P_GK) * (GLA_DK ** -0.5)
    gk_ref[...] = seg(_P_GK, _P_GV)
    gv_ref[...] = seg(_P_GV, _P_GG)
    gg_ref[...] = seg(_P_GG, _P_SQ)
    lr = seg(_P_LR, _P_END)
    z = _hdot(lr, w2_ref[...]) + db_ref[...]
    la = (jnp.minimum(z, 0.0) - jnp.log(1.0 + jnp.exp(-jnp.abs(z)))) * (1.0 / GLA_TAU)
    hk = GLA_HEADS * GLA_DK
    laf_ref[...] = la[:, :hk]
    lab_ref[...] = la[:, hk:]
    cos = cos_ref[...]
    sin = sin_ref[...]
    sq = seg(_P_SQ, _P_SK) * (SWA_HEAD_DIM ** -0.5)
    for c in range(4):
        sq_ref[:, 128 * c:128 * (c + 1)] = _rope(sq[:, 128 * c:128 * (c + 1)], cos, sin)
    sk_ref[...] = _rope(seg(_P_SK, _P_SV), cos, sin)
    sv_ref[...] = seg(_P_SV, _P_LR)


def _proj0(xs, mod, g, w, w2, db, cos, sin, nb, nbatch):
    m, d = xs.shape
    widths = (256, 256, 512, 512, 256, 256, 512, 128, 128)
    return pl.pallas_call(
        functools.partial(_proj0_kernel, d=d),
        grid=(m // TM,),
        in_specs=[pl.BlockSpec((TM, d), lambda i: (i, 0)), _mod_spec(mod, nb, nbatch), _full((1, d)),
                  _full(w.shape), _full(w2.shape), _full(db.shape),
                  pl.BlockSpec((TM, 128), lambda i: (i % nb, 0)),
                  pl.BlockSpec((TM, 128), lambda i: (i % nb, 0))],
        out_specs=[pl.BlockSpec((TM, n), lambda i: (i, 0)) for n in widths],
        out_shape=[jax.ShapeDtypeStruct((m, n), F32) for n in widths],
        compiler_params=_cparams(("arbitrary",), VMEM_LIMIT),
        name="proj0",
    )(xs, mod, g, w, w2, db, cos, sin)


def _gla_prepare(q_ref, k_ref, la_ref, st, cum, tot_row):
    c = GLA_CHUNK
    la = la_ref[...]
    la_hi = la.astype(BF16)
    la_lo = (la - la_hi.astype(F32)).astype(BF16)
    bc = (jnp.dot(cum, la_hi, preferred_element_type=F32)
          + jnp.dot(cum, la_lo, preferred_element_type=F32))
    tot = bc[tot_row:tot_row + 1, :]
    q_in = q_ref[...] * jnp.exp(bc)
    k_in = (k_ref[...] * jnp.exp(-bc)).astype(BF16)
    k_out = (k_ref[...] * jnp.exp(tot - bc)).astype(BF16)
    ones = jnp.ones((c, GLA_DV), BF16)
    tn = (((0,), (0,)), ((), ()))
    dcol = jnp.exp(lax.dot_general(la_hi, ones, tn, preferred_element_type=F32)
                   + lax.dot_general(la_lo, ones, tn, preferred_element_type=F32))
    return q_in, k_in, k_out, dcol, st[...].astype(BF16)


def _gla_head(h, prep, v_ref, o_ref, st, mask):
    q_in, k_in, k_out, dcol, s_prev = prep
    lane = lax.broadcasted_iota(jnp.int32, q_in.shape, 1)
    qh = jnp.where((lane // GLA_DK) == h, q_in, 0.0).astype(BF16)
    att = lax.dot_general(qh, k_in, (((1,), (1,)), ((), ())), preferred_element_type=F32)
    att = jnp.where(mask, att, 0.0).astype(BF16)
    vh = v_ref[:, h * GLA_DV:(h + 1) * GLA_DV].astype(BF16)
    o = (jnp.dot(att, vh, preferred_element_type=F32)
         + jnp.dot(qh, s_prev, preferred_element_type=F32))
    o_ref[:, h * GLA_DV:(h + 1) * GLA_DV] = o
    upd = lax.dot_general(k_out, vh, (((0,), (0,)), ((), ())), preferred_element_type=F32)
    r0, r1 = h * GLA_DK, (h + 1) * GLA_DK
    st[r0:r1, :] = st[r0:r1, :] * dcol[r0:r1, :] + upd[r0:r1, :]


def _gla_kernel(qf, kf, vf, laf, qb, kb, vb, lab, of_ref, ob_ref, sf, sb):
    s = pl.program_id(1)

    @pl.when(s == 0)
    def _():
        sf[...] = jnp.zeros_like(sf)
        sb[...] = jnp.zeros_like(sb)

    c = GLA_CHUNK
    ri = lax.broadcasted_iota(jnp.int32, (c, c), 0)
    ci = lax.broadcasted_iota(jnp.int32, (c, c), 1)
    nb = qf.shape[0]
    pf = [_gla_prepare(qf.at[b], kf.at[b], laf.at[b], sf.at[b], (ri >= ci).astype(BF16), c - 1) for b in range(nb)]
    pb = [_gla_prepare(qb.at[b], kb.at[b], lab.at[b], sb.at[b], (ri <= ci).astype(BF16), 0) for b in range(nb)]
    for h in range(GLA_HEADS):
        for b in range(nb):
            _gla_head(h, pf[b], vf.at[b], of_ref.at[b], sf.at[b], ri >= ci)
            _gla_head(h, pb[b], vb.at[b], ob_ref.at[b], sb.at[b], ci > ri)


GLA_NB = 4


def _gla(gq, gk, gv, laf, lab, nbatch, nctx_chunks):
    m = gq.shape[0]
    c = GLA_CHUNK
    nch = m // nbatch // c
    hk, hv = GLA_HEADS * GLA_DK, GLA_HEADS * GLA_DV
    nbs = GLA_NB if nbatch % GLA_NB == 0 else 1
    r3 = lambda a: a.reshape(nbatch, nch * c, a.shape[1])

    def fi(b, s):
        return (b, s, 0)

    def bi(b, s):
        return (b, jnp.where(s < nctx_chunks, nctx_chunks - 1 - s, nch - 1 + nctx_chunks - s), 0)

    sk = lambda im: pl.BlockSpec((nbs, c, hk), im)
    sv = lambda im: pl.BlockSpec((nbs, c, hv), im)
    of, ob = pl.pallas_call(
        _gla_kernel,
        grid=(nbatch // nbs, nch),
        in_specs=[sk(fi), sk(fi), sv(fi), sk(fi), sk(bi), sk(bi), sv(bi), sk(bi)],
        out_specs=[sv(fi), sv(bi)],
        out_shape=[jax.ShapeDtypeStruct((nbatch, nch * c, hv), F32)] * 2,
        scratch_shapes=[pltpu.VMEM((nbs, hk, GLA_DV), F32), pltpu.VMEM((nbs, hk, GLA_DV), F32)],
        compiler_params=_cparams(("arbitrary", "arbitrary")),
        name="gla_scan",
    )(r3(gq), r3(gk), r3(gv), r3(laf), r3(gq), r3(gk), r3(gv), r3(lab))
    return of.reshape(m, hv), ob.reshape(m, hv)


def _swa_kernel(q_ref, kc_ref, vc_ref, kp_ref, kcur_ref, kn_ref, vp_ref, vcur_ref, vn_ref, sink_ref,
                o_ref, *, nctx_blocks, nlat_blocks):
    j = pl.program_id(1)
    w = WINDOW
    n = j - nctx_blocks
    is_lat = j >= nctx_blocks
    nkc = kc_ref.shape[0]
    kall = jnp.concatenate([kc_ref[...], kp_ref[...], kcur_ref[...], kn_ref[...]], axis=0)
    vall = jnp.concatenate([vc_ref[...], vp_ref[...], vcur_ref[...], vn_ref[...]], axis=0)
    nk = kall.shape[0]
    lane = lax.broadcasted_iota(jnp.int32, kall.shape, 1)
    lo = lane < SWA_HEAD_DIM
    kroll = pltpu.roll(kall, SWA_HEAD_DIM, axis=1)
    vroll = pltpu.roll(vall, SWA_HEAD_DIM, axis=1)
    k2 = [jnp.where(lo, kall, kroll).astype(BF16), jnp.where(lo, kroll, kall).astype(BF16)]
    v2 = [jnp.where(lo, vall, vroll).astype(BF16), jnp.where(lo, vroll, vall).astype(BF16)]
    rep = SWA_HEADS // SWA_KV_HEADS
    qi = lax.broadcasted_iota(jnp.int32, (rep * w, nk), 0) % w
    kj = lax.broadcasted_iota(jnp.int32, (rep * w, nk), 1)
    qpos = n * w + qi
    kpos = (n - 1) * w + (kj - nkc)
    loc_ok = (jnp.abs(qpos - kpos) <= WINDOW) & (kpos >= 0) & (kpos < nlat_blocks * w) & is_lat
    valid = (kj < nkc) | ((kj >= nkc) & loc_ok)
    qlane = lax.broadcasted_iota(jnp.int32, (w, 128), 1)
    qlo = qlane < SWA_HEAD_DIM
    outs = []
    for g in range(SWA_KV_HEADS):
        qs, sinks = [], []
        for r in range(rep):
            pr, half = divmod(g * rep + r, 2)
            qp = q_ref[:, 128 * pr:128 * (pr + 1)]
            qs.append(jnp.where(qlo if half == 0 else ~qlo, qp, 0.0).astype(BF16))
            sinks.append(jnp.broadcast_to(sink_ref[g * rep + r:g * rep + r + 1, 0:1], (w, 1)))
        sink = jnp.concatenate(sinks, axis=0)
        s = lax.dot_general(jnp.concatenate(qs, axis=0), k2[g], (((1,), (1,)), ((), ())),
                            preferred_element_type=F32)
        s = jnp.where(valid, s, NEG)
        mx = jnp.maximum(jnp.max(s, axis=-1, keepdims=True), sink)
        p = jnp.exp(s - mx)
        den = jnp.sum(p, axis=-1, keepdims=True) + jnp.exp(sink - mx)
        o = jnp.dot(p.astype(BF16), v2[g], preferred_element_type=F32) / den
        outs += [o[r * w:(r + 1) * w, :] for r in range(rep)]
    for pr in range(SWA_HEADS // 2):
        o_ref[:, 128 * pr:128 * (pr + 1)] = jnp.where(qlo, outs[2 * pr], outs[2 * pr + 1])


def _swa(sq, sk, sv, sink, nbatch, nctx_blocks, nlat_blocks):
    m = sq.shape[0]
    w = WINDOW
    nblk = nctx_blocks + nlat_blocks
    nkc = nctx_blocks * w

    def qmap(b, j):
        return (b * nblk + j, 0)

    def nmap(off):
        def f(b, j):
            n = jnp.clip(j - nctx_blocks + off, 0, nlat_blocks - 1)
            return (b * nblk + nctx_blocks + n, 0)
        return f

    kvw = sk.shape[1]
    kcs = pl.BlockSpec((nkc, kvw), lambda b, j: (b * (nblk * w // nkc), 0))
    kvs = lambda off: pl.BlockSpec((w, kvw), nmap(off))
    return pl.pallas_call(
        functools.partial(_swa_kernel, nctx_blocks=nctx_blocks, nlat_blocks=nlat_blocks),
        grid=(nbatch, nblk),
        in_specs=[pl.BlockSpec((w, sq.shape[1]), qmap), kcs, kcs, kvs(-1), kvs(0), kvs(1),
                  kvs(-1), kvs(0), kvs(1), _full(sink.shape)],
        out_specs=pl.BlockSpec((w, sq.shape[1]), qmap),
        out_shape=jax.ShapeDtypeStruct(sq.shape, F32),
        compiler_params=_cparams(("arbitrary", "arbitrary")),
        name="swa_attn",
    )(sq, sk, sv, sk, sk, sk, sv, sv, sv, sink)


def _residual_epilogue(x, mix, mod_ref, n2_ref, xo_ref, f_ref, ft_ref, d):
    xn = x + _mod_rows(mod_ref, 2, d) * mix
    xo_ref[...] = xn
    f = _rms(xn) * n2_ref[...] * (1.0 + _mod_rows(mod_ref, 4, d)) + _mod_rows(mod_ref, 3, d)
    f_ref[...] = f
    ft_ref[...] = f.T.astype(BF16)


def _out0_kernel(x_ref, of_ref, ob_ref, gg_ref, a_ref, mod_ref, gn_ref, wo_ref, n2_ref,
                 xo_ref, f_ref, ft_ref, *, d):
    o = of_ref[...] + ob_ref[...]
    gate = gg_ref[...]
    gate = gate * _sigmoid(gate)
    parts = []
    for h in range(GLA_HEADS):
        oh = o[:, h * GLA_DV:(h + 1) * GLA_DV]
        parts.append(oh * lax.rsqrt(jnp.mean(oh * oh, axis=-1, keepdims=True) + NORM_EPS))
    on = jnp.concatenate(parts, axis=1) * gn_ref[...] * gate
    hv = GLA_HEADS * GLA_DV
    mix = _bdot(on, wo_ref[0:hv, :]) + _bdot(a_ref[...], wo_ref[hv:, :])
    _residual_epilogue(x_ref[...], mix, mod_ref, n2_ref, xo_ref, f_ref, ft_ref, d)


def _tok_spec(n):
    return pl.BlockSpec((TM, n), lambda i: (i, 0))


def _epilogue_specs(m, d):
    out_specs = [_tok_spec(d), _tok_spec(d), pl.BlockSpec((d, TM), lambda i: (0, i))]
    out_shape = [jax.ShapeDtypeStruct((m, d), F32), jax.ShapeDtypeStruct((m, d), F32),
                 jax.ShapeDtypeStruct((d, m), BF16)]
    return out_specs, out_shape


def _out0(xs, of, ob, gg, a, mod, gn, wo, n2, nb, nbatch):
    m, d = xs.shape
    out_specs, out_shape = _epilogue_specs(m, d)
    return pl.pallas_call(
        functools.partial(_out0_kernel, d=d),
        grid=(m // TM,),
        in_specs=[_tok_spec(d), _tok_spec(of.shape[1]), _tok_spec(ob.shape[1]), _tok_spec(gg.shape[1]),
                  _tok_spec(a.shape[1]), _mod_spec(mod, nb, nbatch), _full(gn.shape), _full(wo.shape),
                  _full(n2.shape)],
        out_specs=out_specs, out_shape=out_shape,
        compiler_params=_cparams(("arbitrary",), VMEM_LIMIT),
        name="out0",
    )(xs, of, ob, gg, a, mod, gn, wo, n2)


NO_RANK = 99.0
RANK_STEP = 1e27


def _top_vals(s, k, want_rank=False):
    lw = 128
    ng = s.shape[1] // lw
    rows = lax.broadcasted_iota(jnp.int32, (k, lw), 0)
    cur = [s[:, q * lw:(q + 1) * lw] for q in range(ng)]
    vals = [jnp.zeros((k, lw), F32) for _ in range(ng)]
    for t in range(k):
        for q in range(ng):
            mx = jnp.max(cur[q], axis=0, keepdims=True)
            vals[q] = jnp.where(rows == t, mx, vals[q])
            cur[q] = jnp.where(cur[q] == mx, NEG - t * RANK_STEP, cur[q])
    vals = jnp.concatenate(vals, axis=1)
    if not want_rank:
        return vals
    cur = jnp.concatenate(cur, axis=1)
    rank = jnp.where(cur < 0.5 * NEG, jnp.floor((NEG - cur) * (1.0 / RANK_STEP) + 0.5), NO_RANK)
    return vals, rank


def _peer_scores_kernel(f_ref, wh_ref, wl_ref, kh_ref, kl_ref, s1_ref, s2_ref):
    f = f_ref[...]
    fh = f.astype(BF16)
    fl = (f - fh.astype(F32)).astype(BF16)
    wh = wh_ref[...]
    q = (jnp.dot(fh, wh, preferred_element_type=F32) + jnp.dot(fl, wh, preferred_element_type=F32)
         + jnp.dot(fh, wl_ref[...], preferred_element_type=F32))
    qh = q.astype(BF16)
    ql = (q - qh.astype(F32)).astype(BF16)
    nt = (((1,), (1,)), ((), ()))
    for h in range(PEER_HEADS):
        for z, s_ref in enumerate((s1_ref, s2_ref)):
            c0 = (2 * h + z) * PEER_NKEYS
            kh = kh_ref[z, h]
            qhs = qh[:, c0:c0 + PEER_NKEYS]
            s_ref[h] = (lax.dot_general(kh, qhs, nt, preferred_element_type=F32)
                        + lax.dot_general(kh, ql[:, c0:c0 + PEER_NKEYS], nt, preferred_element_type=F32)
                        + lax.dot_general(kl_ref[z, h], qhs, nt, preferred_element_type=F32))


def _peer_q_kernel(s1_ref, s2_ref, r2_ref, cnt_ref, e1_ref, e2_ref):
    kk = PEER_TOPK
    for h in range(PEER_HEADS):
        s1 = s1_ref[h]
        s2 = s2_ref[h]
        v1 = _top_vals(s1, kk)
        v2, rank2 = _top_vals(s2, kk, want_rank=True)
        row8 = lax.broadcasted_iota(jnp.int32, (8, s1.shape[1]), 0)
        blocks = [v1[0:1, :] + v2]
        for r in range(1, 8):
            blocks.append(jnp.where(row8 < kk // (r + 1), v1[r:r + 1, :] + v2[0:8, :], NEG))
        blocks.append(v1[8:kk, :] + v2[0:1, :])
        top = _top_vals(jnp.concatenate(blocks, axis=0), kk)
        tau = top[kk - 1:kk, :]
        z = jnp.sum(jnp.exp(top - top[0:1, :]), axis=0, keepdims=True)
        for q in range(s1.shape[1] // 128):
            ln = slice(128 * q, 128 * (q + 1))
            v2b = [jnp.broadcast_to(v2[c:c + 1, ln], (8, 128)) for c in range(kk)]
            taub = jnp.broadcast_to(tau[:, ln], (8, 128))
            for rb in range(PEER_NKEYS // 8):
                x = s1[8 * rb:8 * (rb + 1), ln]
                cnt = jnp.zeros((8, 128), F32)
                for c in range(kk):
                    cnt = cnt + jnp.where(x + v2b[c] >= taub, 1.0, 0.0)
                cnt_ref[h, 8 * rb:8 * (rb + 1), ln] = cnt
        r2_ref[h] = rank2.astype(BF16)
        e1_ref[h] = jnp.exp(s1 - v1[0:1, :])
        e2_ref[h] = (jnp.exp(s2 - v2[0:1, :]) / z).astype(BF16)


def _peer_fused_kernel(f_ref, wh_ref, wl_ref, kh_ref, kl_ref, r2_ref, cnt_ref, e1_ref, e2_ref,
                       s1_a, s2_a, s1_b, s2_b):
    i = pl.program_id(0)

    def init():
        s1_b[...] = jnp.zeros_like(s1_b)
        s2_b[...] = jnp.zeros_like(s2_b)

    _run_if(i == 0, init)

    def step(s1_w, s2_w, s1_r, s2_r):
        _peer_scores_kernel(f_ref, wh_ref, wl_ref, kh_ref, kl_ref, s1_w, s2_w)
        _peer_q_kernel(s1_r, s2_r, r2_ref, cnt_ref, e1_ref, e2_ref)

    _run_if(i % 2 == 0, lambda: step(s1_a, s2_a, s1_b, s2_b))
    _run_if(i % 2 == 1, lambda: step(s1_b, s2_b, s1_a, s2_a))


def _peer_q(f, wh, wl, k1, k2):
    m, d = f.shape
    hh, nk = PEER_HEADS, PEER_NKEYS
    big = pl.BlockSpec((hh, nk, TM), lambda i: (0, 0, i))
    sh32 = jax.ShapeDtypeStruct((hh, nk, m), F32)
    sh16 = jax.ShapeDtypeStruct((hh, nk, m), BF16)
    kf = jnp.stack([k1, k2])
    kh = kf.astype(BF16)
    kl = (kf - kh.astype(F32)).astype(BF16)
    n = m // TM
    late = pl.BlockSpec((hh, nk, TM), lambda i: (0, 0, jnp.maximum(i - 1, 0)))
    sscr = pltpu.VMEM((hh, nk, TM), F32)
    return pl.pallas_call(
        _peer_fused_kernel,
        grid=(n + 1,),
        in_specs=[pl.BlockSpec((TM, d), lambda i: (jnp.minimum(i, n - 1), 0)),
                  _full(wh.shape), _full(wl.shape), _full(kh.shape), _full(kl.shape)],
        out_specs=[late, late, late, late],
        out_shape=[sh16, sh32, sh32, sh16],
        scratch_shapes=[sscr, sscr, sscr, sscr],
        compiler_params=_cparams(("arbitrary",), VMEM_LIMIT),
        name="peer_query",
    )(f, wh, wl, kh, kl)


PD_TM = 512
PD_TE = 2048
PD_SUBS = (1024, 1024)


def _gelu(x):
    return 0.5 * x * (1.0 + lax.erf(x * (2.0 ** -0.5)))


def _row_bf16(tile, r, rows):
    one = jnp.broadcast_to(tile[r:r + 1, :], (8, tile.shape[1]))
    one = jnp.concatenate([one, one], axis=0).astype(BF16)
    return jnp.concatenate([one] * (rows // 16), axis=0)


def _run_if(cond, fn):
    def body(_, carry):
        fn()
        return carry
    lax.fori_loop(0, cond.astype(jnp.int32), body, 0)


def _peer_dense_kernel(ft_ref, u_ref, vt_ref, r2_ref, cnt_ref, e1_ref, e2_ref, o_ref, acc_ref):
    k = pl.program_id(1)
    nk = PEER_NKEYS
    nslab = PD_TE // nk
    zero = jnp.zeros((nk, PD_TM), BF16)
    nsub = len(PD_SUBS)
    starts = [sum(PD_SUBS[:i]) for i in range(nsub + 1)]
    assert starts[-1] == PD_TE

    def first():
        acc_ref[...] = jnp.zeros_like(acc_ref)

    _run_if(k == 0, first)

    def hidden(sb):
        return jnp.dot(u_ref[starts[sb]:starts[sb + 1], :], ft_ref[...], preferred_element_type=F32)

    ht_next = hidden(0)
    for sb in range(nsub):
        e0, e1x = starts[sb], starts[sb + 1]
        ht = ht_next
        if sb + 1 < nsub:
            ht_next = hidden(sb + 1)
        gates = []
        for al in range(e0 // nk, e1x // nk):
            wg = zero
            a0 = pl.multiple_of(k * nslab + 8 * (al // 8), 8)
            for h in range(PEER_HEADS):
                cnt = _row_bf16(cnt_ref[h, pl.ds(a0, 8), :], al % 8, nk)
                e1 = _row_bf16(e1_ref[h, pl.ds(a0, 8), :], al % 8, nk)
                wg = wg + jnp.where(r2_ref[h] < cnt, e2_ref[h], zero) * e1
            gates.append(wg)
        ct = _gelu(ht).astype(BF16) * jnp.concatenate(gates, axis=0)
        acc_ref[...] += jnp.dot(vt_ref[:, e0:e1x], ct, preferred_element_type=F32)

    def last():
        o_ref[...] = acc_ref[...].T

    _run_if(k == pl.num_programs(1) - 1, last)


def _peer_dense(ft, u, vt, layer, r2, cnt, e1, e2):
    d, m = ft.shape
    ne = u.shape[1]
    hh, nk = PEER_HEADS, PEER_NKEYS
    big = pl.BlockSpec((hh, nk, PD_TM), lambda i, k: (0, 0, i))
    return pl.pallas_call(
        _peer_dense_kernel,
        grid=(m // PD_TM, ne // PD_TE),
        in_specs=[pl.BlockSpec((d, PD_TM), lambda i, k: (0, i)),
                  pl.BlockSpec((None, PD_TE, d), lambda i, k: (layer, k, 0)),
                  pl.BlockSpec((None, None, d, PD_TE), lambda i, k: (layer, k, 0, 0)),
                  big, big, big, big],
        out_specs=pl.BlockSpec((PD_TM, d), lambda i, k: (i, 0)),
        out_shape=jax.ShapeDtypeStruct((m, d), F32),
        scratch_shapes=[pltpu.VMEM((d, PD_TM), F32)],
        compiler_params=_cparams(("arbitrary", "arbitrary"), VMEM_LIMIT),
        name="peer_dense",
    )(ft, u, vt, r2, cnt, e1, e2)


def _res_kernel(x_ref, p_ref, mod_ref, modn_ref, g_ref, xo_ref, u_ref, *, d):
    xn = x_ref[...] + _mod_rows(mod_ref, 5, d) * p_ref[...]
    xo_ref[...] = xn
    u_ref[...] = _rms(xn) * g_ref[...] * (1.0 + _mod_rows(modn_ref, 1, d)) + _mod_rows(modn_ref, 0, d)


def _res(xs, p, mod, modn, g, nb, nbatch):
    m, d = xs.shape
    return pl.pallas_call(
        functools.partial(_res_kernel, d=d),
        grid=(m // TM,),
        in_specs=[_tok_spec(d), _tok_spec(d), _mod_spec(mod, nb, nbatch), _mod_spec(modn, nb, nbatch),
                  _full(g.shape)],
        out_specs=[_tok_spec(d), _tok_spec(d)],
        out_shape=[jax.ShapeDtypeStruct((m, d), F32)] * 2,
        compiler_params=_cparams(("arbitrary",)),
        name="peer_residual",
    )(xs, p, mod, modn, g)


def _final_kernel(x_ref, p_ref, mod_ref, g_ref, y_ref, *, d):
    y_ref[...] = _rms(x_ref[...] + _mod_rows(mod_ref, 5, d) * p_ref[...]) * g_ref[...]


def _final(xs, p, mod, g, nb):
    m, d = xs.shape
    return pl.pallas_call(
        functools.partial(_final_kernel, d=d),
        grid=(m // TM,),
        in_specs=[_tok_spec(d), _tok_spec(d), _mod_spec_lat(mod, nb - 1), _full(g.shape)],
        out_specs=_tok_spec(d),
        out_shape=jax.ShapeDtypeStruct((m, d), F32),
        compiler_params=_cparams(("arbitrary",)),
        name="final_norm",
    )(xs, p, mod, g)


def _split_dot(x, w):
    hi = x.astype(BF16)
    lo = (x - hi.astype(F32)).astype(BF16)
    return jnp.dot(hi, w, preferred_element_type=F32) + jnp.dot(lo, w, preferred_element_type=F32)


def _seg_sum(x, g_ref, gt_ref):
    return _split_dot(_split_dot(x, g_ref[...]), gt_ref[...])


def _rwkv_prep_kernel(u_ref, up_ref, un_ref, mu_ref, wr_ref, wk_ref, wv_ref, w0_ref, w1_ref, w2_ref,
                      a0_ref, a1_ref, a2_ref, g1_ref, g2_ref, kk_ref_, ka_ref, rk_ref, sg_ref, sgt_ref,
                      wf_o, wb_o, kf_o, kb_o, bf_o, bb_o, r_o, v_o, kk_o, g_o, bonus_o, *, nb, d):
    i = pl.program_id(0)
    p = i % nb
    u = u_ref[...]
    has_prev = (p >= 2).astype(F32)
    has_next = ((p >= 1) & (p <= nb - 2)).astype(F32)
    prev_row = up_ref[7:8, :] * has_prev
    next_row = un_ref[0:1, :] * has_next
    rows = lax.broadcasted_iota(jnp.int32, u.shape, 0)
    up = jnp.where(rows == 0, prev_row, pltpu.roll(u, 1, axis=0))
    un = jnp.where(rows == TM - 1, next_row, pltpu.roll(u, TM - 1, axis=0))
    xx = 0.5 * (up + un) - u

    def mix(zi):
        return (u + xx * mu_ref[zi:zi + 1, :]).astype(BF16)

    r = jnp.dot(mix(0), wr_ref[...], preferred_element_type=F32)
    k = jnp.dot(mix(2), wk_ref[...], preferred_element_type=F32)
    v = jnp.dot(mix(3), wv_ref[...], preferred_element_type=F32)
    lw = w0_ref[...] + _bdot(jnp.tanh(jnp.dot(mix(1), w1_ref[...], preferred_element_type=F32)), w2_ref[...])
    decay = jnp.exp(-_sigmoid(lw) * float(np.exp(-0.5)))
    a = _sigmoid(a0_ref[...] + _bdot(jnp.dot(mix(4), a1_ref[...], preferred_element_type=F32), a2_ref[...]))
    g = _bdot(_sigmoid(jnp.dot(mix(5), g1_ref[...], preferred_element_type=F32)), g2_ref[...])
    kk = k * kk_ref_[...]
    kk = kk * lax.rsqrt(jnp.maximum(_seg_sum(kk * kk, sg_ref, sgt_ref), 1e-24))
    ka = ka_ref[...]
    kf = k * (1.0 + (a[:, :d] - 1.0) * ka)
    kb = k * (1.0 + (a[:, d:] - 1.0) * ka)
    bonus = _seg_sum(r * (kf + kb) * rk_ref[...], sg_ref, sgt_ref) * v
    wf_o[...] = decay[:, :d]
    wb_o[...] = decay[:, d:]
    kf_o[...] = kf
    kb_o[...] = kb
    bf_o[...] = kk * a[:, :d]
    bb_o[...] = kk * a[:, d:]
    r_o[...] = r
    v_o[...] = v
    kk_o[...] = kk
    g_o[...] = g
    bonus_o[...] = bonus


def _rwkv_prep(u, consts, nb):
    m, d = u.shape
    r8 = TM // 8
    nb8 = m // 8
    in_specs = [_tok_spec(d),
                pl.BlockSpec((8, d), lambda i: (jnp.maximum(i * r8 - 1, 0), 0)),
                pl.BlockSpec((8, d), lambda i: (jnp.minimum((i + 1) * r8, nb8 - 1), 0))]
    in_specs += [_full(c.shape) for c in consts]
    return pl.pallas_call(
        functools.partial(_rwkv_prep_kernel, nb=nb, d=d),
        grid=(m // TM,),
        in_specs=in_specs,
        out_specs=[_tok_spec(d)] * 11,
        out_shape=[jax.ShapeDtypeStruct((m, d), F32)] * 11,
        compiler_params=_cparams(("arbitrary",), VMEM_LIMIT),
        name="rwkv_prep",
    )(u, u, u, *consts)


RW_TC = 256
RW_UNROLL = 8


def _rwkv_scan_kernel(wf, kf, bf, rf, vf, kkf, wb, kb, bb, rb, vb, kkb, yf_ref, yb_ref,
                      st, vk_a, vk_b, ycol_scr, sa_scr, *, npair):
    s = pl.program_id(1)
    n = RW_HEAD_DIM
    grp = 8

    @pl.when(s == 0)
    def _():
        st[...] = jnp.zeros_like(st)
        ycol_scr[...] = jnp.zeros_like(ycol_scr)

    w2 = 2 * n
    ri = lax.broadcasted_iota(jnp.int32, (n, w2), 0)
    li = lax.broadcasted_iota(jnp.int32, (n, w2), 1)
    eye_a = (li == ri).astype(BF16)
    bi = lax.broadcasted_iota(jnp.int32, (w2, w2), 0)
    bj = lax.broadcasted_iota(jnp.int32, (w2, w2), 1)
    ones_blk = ((bi // n) == (bj // n)).astype(BF16)
    b4i = lax.broadcasted_iota(jnp.int32, (2 * w2, 2 * w2), 0)
    b4j = lax.broadcasted_iota(jnp.int32, (2 * w2, 2 * w2), 1)
    ones_blk2 = ((b4i // n) == (b4j // n)).astype(BF16)
    lane8 = lax.broadcasted_iota(jnp.int32, (grp, w2), 1)
    rows16 = lax.broadcasted_iota(jnp.int32, (2 * grp, w2), 0)
    lane16 = lax.broadcasted_iota(jnp.int32, (2 * grp, w2), 1)
    untr = ((lane16 == rows16) | (lane16 == rows16 + n)).astype(BF16)
    dirs = ((wf, kf, bf, rf, vf, kkf, yf_ref), (wb, kb, bb, rb, vb, kkb, yb_ref))
    nch = 2 * npair
    ngrp = RW_TC // grp
    nt = (((1,), (1,)), ((), ()))

    def halves(tile):
        return jnp.concatenate([jnp.where(lane8 < n, tile, 0.0), jnp.where(lane8 >= n, tile, 0.0)], axis=0)

    assert nch == 2 * grp

    def group_base(g, dd):
        base = (g if dd == 0 else ngrp - 1 - g) * grp
        return base if isinstance(base, int) else pl.multiple_of(base, grp)

    def lanes(c):
        p = c % npair
        return slice(w2 * p, w2 * (p + 1))

    def precompute(g, chains, vk_dst):
        gq = jnp.minimum(g, ngrp - 1)
        v2, ks = [], []
        for c in chains:
            dd = c // npair
            base = group_base(gq, dd)
            v_t = dirs[dd][4][pl.ds(base, grp), lanes(c)]
            ks.append(dirs[dd][1][pl.ds(base, grp), lanes(c)])
            v2 += [jnp.where(lane8 < n, v_t, 0.0), jnp.where(lane8 < n, pltpu.roll(v_t, n, axis=1), 0.0)]
        vt = lax.dot_general(eye_a, jnp.concatenate(v2, axis=0).astype(BF16), nt,
                             preferred_element_type=F32).astype(BF16)
        for idx, c in enumerate(chains):
            k2 = halves(ks[idx])
            kblk = jnp.concatenate(
                [jnp.where((rows16 == j) | (rows16 == grp + j), k2, 0.0) for j in range(grp)], axis=1)
            vk_dst[c] = jnp.dot(vt[:, 2 * grp * idx:2 * grp * (idx + 1)], kblk.astype(BF16),
                                preferred_element_type=F32)

    def untranspose(g, chains, ycol_prev):
        for c in chains:
            dd = c // npair
            yc = ycol_prev[c]
            ycs = jnp.concatenate([jnp.where(li < n, yc, 0.0), jnp.where(li >= n, yc, 0.0)], axis=0)
            yt = lax.dot_general(untr, ycs.astype(BF16), nt, preferred_element_type=F32)
            dirs[dd][6][pl.ds(group_base(g, dd), grp), lanes(c)] = yt[0:grp, :]

    def run_group(g, vk_cur, vk_next, states, sa, ycol_prev):
        tiles = []
        for c in range(nch):
            dd = c // npair
            base = group_base(g, dd)
            tiles.append([dirs[dd][q][pl.ds(base, grp), lanes(c)] for q in (0, 2, 3, 5)])
        gprev = jnp.maximum(g - 1, 0)
        gnext = jnp.minimum(g + 1, ngrp - 1)
        kap_next = []
        for c in range(nch):
            dd = c // npair
            kap_next.append(dirs[dd][5][pl.ds(group_base(gnext, dd), grp), lanes(c)])
        states, sa = list(states), list(sa)
        ycol = [jnp.zeros((n, w2), F32) for _ in range(nch)]
        for j in range(grp):
            lhs = []
            for c in range(nch):
                w_t, b_t, r_t, kk_t = tiles[c]
                fwd = c < npair
                jj = j if fwd else grp - 1 - j
                row = lambda tl, q: jnp.broadcast_to(tl[q:q + 1, :], (n, w2))
                states[c] = (states[c] * row(w_t, jj) - sa[c] * row(b_t, jj)
                             + vk_cur[c, :, jj * w2:(jj + 1) * w2])
                if j + 1 < grp:
                    kap = row(kk_t, jj + 1 if fwd else jj - 1)
                else:
                    kap = row(kap_next[c], 0 if fwd else grp - 1)
                lhs.append(jnp.concatenate([(states[c] * kap).astype(BF16),
                                            (states[c] * row(r_t, jj)).astype(BF16)], axis=1))
            red = jnp.dot(jnp.concatenate(lhs, axis=0), ones_blk2, preferred_element_type=F32)
            for c in range(nch):
                jj = j if c < npair else grp - 1 - j
                sa[c] = red[c * n:(c + 1) * n, :w2]
                ycol[c] = jnp.where((li % n) == jj, red[c * n:(c + 1) * n, w2:], ycol[c])
            pair = (2 * j, 2 * j + 1)
            precompute(g + 1, pair, vk_next)
            untranspose(gprev, pair, ycol_prev)
        return states, sa, ycol

    precompute(0, tuple(range(nch)), vk_a)
    first = []
    for c in range(nch):
        dd = c // npair
        kk_t = dirs[dd][5][pl.ds(group_base(0, dd), grp), lanes(c)]
        q0 = 0 if dd == 0 else grp - 1
        first.append((st[c] * jnp.broadcast_to(kk_t[q0:q0 + 1, :], (n, w2))).astype(BF16))
    sa0 = jnp.dot(jnp.concatenate(first, axis=0), ones_blk, preferred_element_type=F32)
    for c in range(nch):
        sa_scr[c] = sa0[c * n:(c + 1) * n, :]

    def trip(i, carry):
        states = [st[c] for c in range(nch)]
        sa = [sa_scr[c] for c in range(nch)]
        ycol = [ycol_scr[c] for c in range(nch)]
        for u in range(0, RW_UNROLL, 2):
            states, sa, ycol = run_group(RW_UNROLL * i + u, vk_a, vk_b, states, sa, ycol)
            states, sa, ycol = run_group(RW_UNROLL * i + u + 1, vk_b, vk_a, states, sa, ycol)
        for c in range(nch):
            st[c] = states[c]
            sa_scr[c] = sa[c]
            ycol_scr[c] = ycol[c]
        return carry

    lax.fori_loop(0, ngrp // RW_UNROLL, trip, 0)
    untranspose(ngrp - 1, tuple(range(nch)), [ycol_scr[c] for c in range(nch)])


def _rwkv_scan(wf, kf, bf, wb, kb, bb, r, v, kk, nbatch):
    m, d = r.shape
    nch = m // nbatch // RW_TC
    npair = d // (2 * RW_HEAD_DIM)

    def fi(b, s):
        return (b * nch + s, 0)

    def bi(b, s):
        return (b * nch + jnp.where(s == 0, 0, nch - s), 0)

    sf = pl.BlockSpec((RW_TC, d), fi)
    sb = pl.BlockSpec((RW_TC, d), bi)
    return pl.pallas_call(
        functools.partial(_rwkv_scan_kernel, npair=npair),
        grid=(nbatch, nch),
        in_specs=[sf] * 6 + [sb] * 6,
        out_specs=[sf, sb],
        out_shape=[jax.ShapeDtypeStruct((m, d), F32)] * 2,
        scratch_shapes=[pltpu.VMEM((2 * npair, RW_HEAD_DIM, 2 * RW_HEAD_DIM), F32),
                        pltpu.VMEM((2 * npair, RW_HEAD_DIM, 8 * 2 * RW_HEAD_DIM), F32),
                        pltpu.VMEM((2 * npair, RW_HEAD_DIM, 8 * 2 * RW_HEAD_DIM), F32),
                        pltpu.VMEM((2 * npair, RW_HEAD_DIM, 2 * RW_HEAD_DIM), F32),
                        pltpu.VMEM((2 * npair, RW_HEAD_DIM, 2 * RW_HEAD_DIM), F32)],
        compiler_params=_cparams(("arbitrary", "arbitrary"), VMEM_LIMIT),
        name="rwkv_scan",
    )(wf, kf, bf, r, v, kk, wb, kb, bb, r, v, kk)


def _rwkv_out_kernel(x_ref, yf_ref, yb_ref, bonus_ref, g_ref, mod_ref, lg_ref, lb_ref, wo_ref, n2_ref,
                     sg_ref, sgt_ref, xo_ref, f_ref, ft_ref, *, d):
    y = yf_ref[...] + yb_ref[...]
    inv = 1.0 / RW_HEAD_DIM
    mean = _seg_sum(y, sg_ref, sgt_ref) * inv
    yc = y - mean
    var = _seg_sum(yc * yc, sg_ref, sgt_ref) * inv
    yn = yc * lax.rsqrt(var + RW_GN_EPS) * lg_ref[...] + lb_ref[...]
    out = (yn + bonus_ref[...]) * g_ref[...]
    mix = _bdot(out, wo_ref[...])
    _residual_epilogue(x_ref[...], mix, mod_ref, n2_ref, xo_ref, f_ref, ft_ref, d)


def _rwkv_out(xs, yf, yb, bonus, g, mod, lg, lb, wo, n2, sg, sgt, nb, nbatch):
    d = xs.shape[1]
    m = nbatch * (nb - 1) * TM
    out_specs, out_shape = _epilogue_specs(m, d)
    return pl.pallas_call(
        functools.partial(_rwkv_out_kernel, d=d),
        grid=(m // TM,),
        in_specs=[_lat_tok_spec(d, nb)] * 5 + [_mod_spec_lat(mod, nb - 1)]
        + [_full(a.shape) for a in (lg, lb, wo, n2, sg, sgt)],
        out_specs=out_specs, out_shape=out_shape,
        compiler_params=_cparams(("arbitrary",), VMEM_LIMIT),
        name="rwkv_out",
    )(xs, yf, yb, bonus, g, mod, lg, lb, wo, n2, sg, sgt)


def _rope_tables(t_lat, n_ctx):
    rows = t_lat // GRID_W
    row = jnp.broadcast_to(jnp.arange(rows, dtype=F32)[:, None], (rows, GRID_W)).reshape(-1)
    col = jnp.broadcast_to(jnp.arange(GRID_W, dtype=F32)[None, :], (rows, GRID_W)).reshape(-1)
    n_freq = SWA_HEAD_DIM // 4
    inv_freq = ROPE_BASE ** (-jnp.arange(n_freq, dtype=F32) / n_freq)
    ar = row[:, None] * inv_freq
    ac = col[:, None] * inv_freq
    cos = jnp.concatenate([jnp.cos(ar), jnp.cos(ar), jnp.cos(ac), jnp.cos(ac)], axis=1)
    sin = jnp.concatenate([-jnp.sin(ar), jnp.sin(ar), -jnp.sin(ac), jnp.sin(ac)], axis=1)
    cos = jnp.concatenate([jnp.ones((n_ctx, SWA_HEAD_DIM), F32), cos], axis=0)
    sin = jnp.concatenate([jnp.zeros((n_ctx, SWA_HEAD_DIM), F32), sin], axis=0)
    return jnp.tile(cos, (1, 2)), jnp.tile(sin, (1, 2))


def _peer(f, ft, w_q, k1, k2, u_all, vt_all, layer):
    wh = w_q.astype(BF16)
    wl = (w_q - wh.astype(F32)).astype(BF16)
    r2, cnt, e1, e2 = _peer_q(f, wh, wl, k1, k2)
    return _peer_dense(ft, u_all, vt_all, layer, r2, cnt, e1, e2)


def kernel(x, c, ctx, c_ctx, ada_w, ada_b, norm1_g, norm2_g, ab_w_in, gla_dec_w2, gla_dec_b, gla_norm_g,
           swa_sink, ab_w_out, rw_mu, rw_w_rkv, rw_w_o, rw_w0, rw_w1, rw_w2, rw_a0, rw_a1, rw_a2, rw_g1,
           rw_g2, rw_k_k, rw_k_a, rw_r_k, rw_ln_g, rw_ln_b, peer_w_q, peer_k1, peer_k2, peer_u, peer_v,
           final_g):
    nbatch, t_lat, d = x.shape
    n_ctx = ctx.shape[1]
    assert n_ctx == TM and t_lat % TM == 0 and d == 1024
    s_tok = n_ctx + t_lat
    nb = s_tok // TM
    m = nbatch * s_tok
    xs = jnp.concatenate([ctx, x], axis=1).reshape(m, d)

    cc = jnp.zeros((16, d), F32).at[:nbatch].set(c).at[nbatch].set(c_ctx)
    mod0 = _ada_table(cc, ada_w[0], ada_b[0]).reshape(16, 1, 6 * d)
    mod1 = _ada_table(cc, ada_w[1], ada_b[1]).reshape(16, 1, 6 * d)
    row2 = lambda a: a.reshape(1, -1)

    w_in = ab_w_in[0]
    cuts = np.cumsum((256, 256, 512, 512, 32, 512, 128, 128))[:-1].tolist()
    wgq, wgk, wgv, wgg, wlr, wsq, wsk, wsv = jnp.split(w_in, cuts, axis=1)
    wp = jnp.concatenate([wgq, wgk, wgv, wgg, wsq, wsk, wsv, wlr,
                          jnp.zeros((d, _P_END - _P_LR - 32), F32)], axis=1).astype(BF16)
    hk = GLA_HEADS * GLA_DK
    w2p = jnp.zeros((128, 2 * hk), F32)
    w2p = w2p.at[0:GLA_LOWRANK, 0:hk].set(gla_dec_w2[0, 0]).at[GLA_LOWRANK:2 * GLA_LOWRANK, hk:].set(gla_dec_w2[0, 1])
    db = gla_dec_b[0].reshape(1, 2 * hk)
    cos, sin = _rope_tables(t_lat, n_ctx)
    gq, gk, gv, gg, laf, lab, sq, sk, sv = _proj0(xs, mod0, row2(norm1_g[0]), wp, w2p, db, cos, sin, nb, nbatch)
    of, ob = _gla(gq, gk, gv, laf, lab, nbatch, n_ctx // GLA_CHUNK)
    sink = jnp.broadcast_to(swa_sink[0][:, None], (SWA_HEADS, 128))
    att = _swa(sq, sk, sv, sink, nbatch, n_ctx // WINDOW, t_lat // WINDOW)
    xs, f, ft = _out0(xs, of, ob, gg, att, mod0, row2(gla_norm_g[0]), ab_w_out[0].astype(BF16),
                      row2(norm2_g[0]), nb, nbatch)
    u_all = peer_u.astype(BF16)
    vt_all = jnp.swapaxes(peer_v.reshape(peer_v.shape[0], -1, PD_TE, d), 2, 3).astype(BF16)
    p = _peer(f, ft, peer_w_q[0], peer_k1[0], peer_k2[0], u_all, vt_all, 0)
    xs, u = _res(xs, p, mod0, mod1, row2(norm1_g[1]), nb, nbatch)

    lora = rw_w1.shape[-1]
    w1c = jnp.concatenate([rw_w1[0, 0], rw_w1[0, 1]], axis=1).astype(BF16)
    w2c = jnp.zeros((2 * lora, 2 * d), F32).at[:lora, :d].set(rw_w2[0, 0]).at[lora:, d:].set(rw_w2[0, 1]).astype(BF16)
    la = rw_a1.shape[-1]
    a1c = jnp.concatenate([rw_a1[0, 0], rw_a1[0, 1]], axis=1).astype(BF16)
    a2c = jnp.zeros((2 * la, 2 * d), F32).at[:la, :d].set(rw_a2[0, 0]).at[la:, d:].set(rw_a2[0, 1]).astype(BF16)
    lg = rw_g1.shape[-1]
    g1p = jnp.zeros((d, 256), F32).at[:, :lg].set(rw_g1[0]).astype(BF16)
    g2p = jnp.zeros((256, d), F32).at[:lg].set(rw_g2[0]).astype(BF16)
    sg = (jnp.arange(d)[:, None] // RW_HEAD_DIM == jnp.arange(128)[None, :]).astype(BF16)
    sgt = sg.T
    consts = [rw_mu[0], rw_w_rkv[0, 0].astype(BF16), rw_w_rkv[0, 1].astype(BF16), rw_w_rkv[0, 2].astype(BF16),
              rw_w0[0].reshape(1, 2 * d), w1c, w2c, rw_a0[0].reshape(1, 2 * d), a1c, a2c, g1p, g2p,
              row2(rw_k_k[0]), row2(rw_k_a[0]), rw_r_k[0].reshape(1, d), sg, sgt]
    mu8 = jnp.zeros((8, d), F32).at[:6].set(rw_mu[0])
    consts[0] = mu8
    wf, wb, kf, kb, bf, bb, r, v, kk, g, bonus = _rwkv_prep(u, consts, nb)
    yf, yb = _rwkv_scan(wf, kf, bf, wb, kb, bb, r, v, kk, nbatch)
    xs, f, ft = _rwkv_out(xs, yf, yb, bonus, g, mod1, row2(rw_ln_g[0]), row2(rw_ln_b[0]),
                          rw_w_o[0].astype(BF16), row2(norm2_g[1]), sg, sgt, nb, nbatch)
    p = _peer(f, ft, peer_w_q[1], peer_k1[1], peer_k2[1], u_all, vt_all, 1)
    y = _final(xs, p, mod1, row2(final_g), nb)
    return y.reshape(nbatch, t_lat, d)
```
